```python
import jax
import jax.numpy as jnp
from jax import lax
import numpy as np


D_MODEL = 1024
BATCH = 4
SEQ = 8192
DEPTH = 1

GRID_W = 64
CTX_LEN = 256
CONV_CH = 512
CONV_ROW_CH = CONV_CH // 2
CONV_WIDTH = 31
LRU_WIDTH = 512
LRU_HEADS = 8
LRU_HEAD_DIM = LRU_WIDTH // LRU_HEADS
LRU_CONV_WIDTH = 4
LRU_C = 8.0
N_DIRS = 2
D_IN = 2 * CONV_CH + 2 * LRU_WIDTH
D_MIX = CONV_CH + LRU_WIDTH
N_EXPERTS = 16
EC_CAPACITY = 2
D_EXPERT = 2816
N_MOD = 6
RMS_EPS = 1e-6
LN_EPS = 1e-5

kernel_name = "hybrid_conformer_rglru_ecmoe_dit"


def rms_norm(x, g):
    xf = x.astype(jnp.float32)
    y = xf * lax.rsqrt(jnp.mean(xf * xf, axis=-1, keepdims=True) + RMS_EPS)
    return (y * g.astype(jnp.float32)).astype(x.dtype)


def layer_norm(x, g, b):
    xf = x.astype(jnp.float32)
    mu = jnp.mean(xf, axis=-1, keepdims=True)
    var = jnp.mean(jnp.square(xf - mu), axis=-1, keepdims=True)
    y = (xf - mu) * lax.rsqrt(var + LN_EPS) * g.astype(jnp.float32) + b.astype(jnp.float32)
    return y.astype(x.dtype)


def adaln(cond, w, b):
    return jnp.split(jax.nn.silu(cond) @ w + b, N_MOD, axis=-1)


def modulate(h, shift, scale):
    return h * (1.0 + scale[:, None, :]) + shift[:, None, :]


def depthwise_conv_seq(u, w, pad):
    return lax.conv_general_dilated(u, w[:, None, :].astype(u.dtype), window_strides=(1,), padding=[pad],
                                    dimension_numbers=('NWC', 'WIO', 'NWC'), feature_group_count=u.shape[-1])


def depthwise_conv_grid(u, w):
    bsz, n, ch = u.shape
    rows = n // GRID_W
    pad = (CONV_WIDTH - 1) // 2
    g = u.reshape(bsz, rows, GRID_W, ch)
    w = w.astype(u.dtype)
    dn = ('NHWC', 'HWIO', 'NHWC')
    n_col = ch - CONV_ROW_CH
    y_row = lax.conv_general_dilated(g[..., :CONV_ROW_CH], w[:, :CONV_ROW_CH].reshape(1, CONV_WIDTH, 1, CONV_ROW_CH),
                                     (1, 1), [(0, 0), (pad, pad)], dimension_numbers=dn, feature_group_count=CONV_ROW_CH)
    y_col = lax.conv_general_dilated(g[..., CONV_ROW_CH:], w[:, CONV_ROW_CH:].reshape(CONV_WIDTH, 1, 1, n_col),
                                     (1, 1), [(pad, pad), (0, 0)], dimension_numbers=dn, feature_group_count=n_col)
    return jnp.concatenate([y_row, y_col], axis=-1).reshape(bsz, n, ch)


def conformer_tail(y, g, b):
    return jax.nn.silu(layer_norm(y, g, b))


def mixer_inputs(h, norm_g, shift, scale, w_in, b_in):
    u = modulate(rms_norm(h, norm_g), shift, scale)
    p = u @ w_in + b_in
    cv, cg, lx, lg = jnp.split(p, [CONV_CH, 2 * CONV_CH, 2 * CONV_CH + LRU_WIDTH], axis=-1)
    return cv * jax.nn.sigmoid(cg), lx, lg


def _linear_combine(left, right):
    a1, b1 = left
    a2, b2 = right
    return a1 * a2, a2 * b1 + b2


def rg_lru_dir(xr, lru_p, d, h0):
    conv_w, conv_b, wa, ba, wi, bi, lam = lru_p
    bsz, n, ch = xr.shape
    xc = depthwise_conv_seq(xr, conv_w[d], (LRU_CONV_WIDTH - 1, 0)) + conv_b[d]
    xh = xc.reshape(bsz, n, LRU_HEADS, LRU_HEAD_DIM)
    r = jax.nn.sigmoid((jnp.einsum('bnhi,hij->bnhj', xh, wa[d]).reshape(bsz, n, ch) + ba[d]).astype(jnp.float32))
    i = jax.nn.sigmoid((jnp.einsum('bnhi,hij->bnhj', xh, wi[d]).reshape(bsz, n, ch) + bi[d]).astype(jnp.float32))
    log_a = -LRU_C * r * jax.nn.softplus(-lam[d].astype(jnp.float32))
    a = jnp.exp(log_a)
    u = jnp.sqrt(-jnp.expm1(2.0 * log_a)) * (i * xc.astype(jnp.float32))
    a_cum, u_cum = lax.associative_scan(_linear_combine, (a, u), axis=1)
    return a_cum * h0[:, None, :] + u_cum


def merge_heads(conv_y, h_sum, lg, w_out, b_out):
    y_lru = (h_sum * jax.nn.gelu(lg.astype(jnp.float32))).astype(conv_y.dtype)
    return jnp.concatenate([conv_y, y_lru], axis=-1) @ w_out + b_out


def ec_moe(v, router_w, wg, wu, wd):
    bsz, n, _ = v.shape
    cap = EC_CAPACITY * n // N_EXPERTS
    aff = jax.nn.softmax(jnp.einsum('bnd,de->bne', v, router_w).astype(jnp.float32), axis=-1)
    gate, idx = lax.top_k(jnp.swapaxes(aff, 1, 2), cap)
    bidx = jnp.arange(bsz)[:, None, None]
    xs = jnp.swapaxes(v[bidx, idx], 0, 1)

    def expert(args):
        xe, wge, wue, wde = args
        return (jax.nn.silu(xe @ wge) * (xe @ wue)) @ wde

    ys = jnp.swapaxes(lax.map(expert, (xs, wg, wu, wd)), 0, 1)
    ys = ys * gate[..., None].astype(v.dtype)
    return jnp.zeros_like(v).at[bidx, idx].add(ys)


def hybrid_layer(x, ctx, c, c_ctx, norm1_g, norm2_g, ada_w, ada_b, w_in, b_in, conv_p, lru_p,
                 w_out, b_out, moe_p, update_ctx):
    conv_w, conv_b, ln_g, ln_b = conv_p
    mod = adaln(c, ada_w, ada_b)
    mod_c = adaln(c_ctx[None, :], ada_w, ada_b)

    c_glu, c_lx, c_lg = mixer_inputs(ctx, norm1_g, mod_c[0], mod_c[1], w_in, b_in)
    h0 = jnp.zeros((ctx.shape[0], LRU_WIDTH), jnp.float32)
    hf_c = rg_lru_dir(c_lx, lru_p, 0, h0)
    hb_c_rev = rg_lru_dir(c_lx[:, ::-1], lru_p, 1, h0)

    x_glu, x_lx, x_lg = mixer_inputs(x, norm1_g, mod[0], mod[1], w_in, b_in)
    conv_x = conformer_tail(depthwise_conv_grid(x_glu, conv_w) + conv_b, ln_g, ln_b)
    hf = rg_lru_dir(x_lx, lru_p, 0, hf_c[:, -1])
    hb = rg_lru_dir(x_lx[:, ::-1], lru_p, 1, hb_c_rev[:, -1])[:, ::-1]
    x = x + mod[2][:, None, :] * merge_heads(conv_x, hf + hb, x_lg, w_out, b_out)
    x = x + mod[5][:, None, :] * ec_moe(modulate(rms_norm(x, norm2_g), mod[3], mod[4]), *moe_p)

    if update_ctx:
        pad = (CONV_WIDTH - 1) // 2
        conv_c = conformer_tail(depthwise_conv_seq(c_glu, conv_w, (pad, pad)) + conv_b, ln_g, ln_b)
        ctx = ctx + mod_c[2][:, None, :] * merge_heads(conv_c, hf_c + hb_c_rev[:, ::-1], c_lg, w_out, b_out)
        ctx = ctx + mod_c[5][:, None, :] * ec_moe(modulate(rms_norm(ctx, norm2_g), mod_c[3], mod_c[4]), *moe_p)
    return x, ctx


def setup_inputs(seed: int = 0) -> dict:
    key = jax.random.key(seed)
    ks = jax.random.split(key, 28)
    f32 = jnp.float32

    def nrm(k, shape, scale):
        return scale * jax.random.normal(k, shape, f32)

    u = jax.random.uniform(ks[20], (DEPTH, N_DIRS, LRU_WIDTH), f32, 0.9, 0.999)
    a0 = u ** (1.0 / LRU_C)
    return {
        "x": nrm(ks[0], (BATCH, SEQ, D_MODEL), 1.0),
        "c": nrm(ks[1], (BATCH, D_MODEL), 1.0),
        "ctx": nrm(ks[2], (BATCH, CTX_LEN, D_MODEL), 1.0),
        "c_ctx": nrm(ks[3], (D_MODEL,), 1.0),
        "norm1_g": 1.0 + nrm(ks[4], (DEPTH, D_MODEL), 0.02),
        "norm2_g": 1.0 + nrm(ks[5], (DEPTH, D_MODEL), 0.02),
        "ada_w": nrm(ks[6], (DEPTH, D_MODEL, N_MOD * D_MODEL), D_MODEL ** -0.5),
        "ada_b": nrm(ks[7], (DEPTH, N_MOD * D_MODEL), 0.02),
        "w_in": nrm(ks[8], (DEPTH, D_MODEL, D_IN), D_MODEL ** -0.5),
        "b_in": nrm(ks[9], (DEPTH, D_IN), 0.02),
        "conv_dw_w": nrm(ks[10], (DEPTH, CONV_WIDTH, CONV_CH), CONV_WIDTH ** -0.5),
        "conv_dw_b": nrm(ks[11], (DEPTH, CONV_CH), 0.02),
        "conv_ln_g": 1.0 + nrm(ks[12], (DEPTH, CONV_CH), 0.02),
        "conv_ln_b": nrm(ks[13], (DEPTH, CONV_CH), 0.02),
        "lru_conv_w": nrm(ks[14], (DEPTH, N_DIRS, LRU_CONV_WIDTH, LRU_WIDTH), LRU_CONV_WIDTH ** -0.5),
        "lru_conv_b": nrm(ks[15], (DEPTH, N_DIRS, LRU_WIDTH), 0.02),
        "lru_wa": nrm(ks[16], (DEPTH, N_DIRS, LRU_HEADS, LRU_HEAD_DIM, LRU_HEAD_DIM), LRU_HEAD_DIM ** -0.5),
        "lru_ba": nrm(ks[17], (DEPTH, N_DIRS, LRU_WIDTH), 0.02),
        "lru_wi": nrm(ks[18], (DEPTH, N_DIRS, LRU_HEADS, LRU_HEAD_DIM, LRU_HEAD_DIM), LRU_HEAD_DIM ** -0.5),
        "lru_bi": nrm(ks[19], (DEPTH, N_DIRS, LRU_WIDTH), 0.02),
        "lru_lambda": jnp.log(a0) - jnp.log1p(-a0),
        "w_out": nrm(ks[21], (DEPTH, D_MIX, D_MODEL), D_MIX ** -0.5),
        "b_out": nrm(ks[22], (DEPTH, D_MODEL), 0.02),
        "router_w": nrm(ks[23], (DEPTH, D_MODEL, N_EXPERTS), D_MODEL ** -0.5),
        "exp_w_gate": nrm(ks[24], (DEPTH, N_EXPERTS, D_MODEL, D_EXPERT), D_MODEL ** -0.5),
        "exp_w_up": nrm(ks[25], (DEPTH, N_EXPERTS, D_MODEL, D_EXPERT), D_MODEL ** -0.5),
        "exp_w_down": nrm(ks[26], (DEPTH, N_EXPERTS, D_EXPERT, D_MODEL), D_EXPERT ** -0.5),
        "final_norm_g": 1.0 + nrm(ks[27], (D_MODEL,), 0.02),
    }


def reference(x, c, ctx, c_ctx, norm1_g, norm2_g, ada_w, ada_b, w_in, b_in, conv_dw_w, conv_dw_b,
              conv_ln_g, conv_ln_b, lru_conv_w, lru_conv_b, lru_wa, lru_ba, lru_wi, lru_bi, lru_lambda,
              w_out, b_out, router_w, exp_w_gate, exp_w_up, exp_w_down, final_norm_g):
    for l in range(DEPTH):
        conv_p = (conv_dw_w[l], conv_dw_b[l], conv_ln_g[l], conv_ln_b[l])
        lru_p = (lru_conv_w[l], lru_conv_b[l], lru_wa[l], lru_ba[l], lru_wi[l], lru_bi[l], lru_lambda[l])
        moe_p = (router_w[l], exp_w_gate[l], exp_w_up[l], exp_w_down[l])
        x, ctx = hybrid_layer(x, ctx, c, c_ctx, norm1_g[l], norm2_g[l], ada_w[l], ada_b[l], w_in[l], b_in[l],
                              conv_p, lru_p, w_out[l], b_out[l], moe_p, l < DEPTH - 1)
    return rms_norm(x, final_norm_g)
```

```python
import functools

import jax
import jax.numpy as jnp
from jax import lax
from jax.experimental import pallas as pl
from jax.experimental.pallas import tpu as pltpu

GRID_W = 64
CONV_WIDTH = 31
CONV_PAD = (CONV_WIDTH - 1) // 2
LRU_CONV_WIDTH = 4
LRU_C = 8.0
N_EXPERTS = 16
EC_CAPACITY = 2
N_MOD = 6
RMS_EPS = 1e-6
LN_EPS = 1e-5

LANES = 128
SUBLANES = 8
BF16_ROWS = 16
VMEM_LIMIT = 56 * 1024 * 1024

F32 = jnp.float32
BF16 = jnp.bfloat16
HIGHEST = lax.Precision.HIGHEST


def _params(*sem):
    return pltpu.CompilerParams(dimension_semantics=sem, vmem_limit_bytes=VMEM_LIMIT)


def _ada_body(c_ref, w_ref, b_ref, o_ref):
    s = c_ref[...]
    s = s * jax.nn.sigmoid(s)
    o_ref[...] = jnp.dot(s, w_ref[...], precision=HIGHEST, preferred_element_type=F32) + b_ref[...]


def _ada_mod(cond8, ada_w, ada_b):
    d, n = ada_w.shape
    tn = n // 4
    return pl.pallas_call(
        _ada_body,
        grid=(n // tn,),
        in_specs=[pl.BlockSpec((SUBLANES, d), lambda j: (0, 0)),
                  pl.BlockSpec((d, tn), lambda j: (0, j)),
                  pl.BlockSpec((1, tn), lambda j: (0, j))],
        out_specs=pl.BlockSpec((SUBLANES, tn), lambda j: (0, j)),
        out_shape=jax.ShapeDtypeStruct((SUBLANES, n), F32),
        compiler_params=_params("arbitrary"),
        name="ada_mod",
    )(cond8, ada_w, ada_b.reshape(1, n))


def _inproj_body(x_ref, g_ref, sh_ref, sc_ref, w_ref, b_ref, glu_ref, lx_ref, glg_ref):
    x = x_ref[0]
    ms = jnp.mean(x * x, axis=-1, keepdims=True)
    y = x * lax.rsqrt(ms + RMS_EPS) * g_ref[...]
    u = y * (1.0 + sc_ref[0]) + sh_ref[0]
    p = jnp.dot(u.astype(BF16), w_ref[...], preferred_element_type=F32) + b_ref[...]
    cc = p.shape[1] // 4
    glu_ref[0] = p[:, :cc] * jax.nn.sigmoid(p[:, cc:2 * cc])
    lx_ref[0] = p[:, 2 * cc:3 * cc]
    glg_ref[0] = jax.nn.gelu(p[:, 3 * cc:])


def _in_proj(x, norm_g, shift, scale, w_bf16, b_in, tm):
    bsz, n, d = x.shape
    n4 = w_bf16.shape[1]
    cc = n4 // 4
    tok = pl.BlockSpec((1, tm, cc), lambda b, i: (b, i, 0))
    return pl.pallas_call(
        _inproj_body,
        grid=(bsz, n // tm),
        in_specs=[pl.BlockSpec((1, tm, d), lambda b, i: (b, i, 0)),
                  pl.BlockSpec((1, d), lambda b, i: (0, 0)),
                  pl.BlockSpec((1, 1, d), lambda b, i: (b, 0, 0)),
                  pl.BlockSpec((1, 1, d), lambda b, i: (b, 0, 0)),
                  pl.BlockSpec((d, n4), lambda b, i: (0, 0)),
                  pl.BlockSpec((1, n4), lambda b, i: (0, 0))],
        out_specs=[tok, tok, tok],
        out_shape=[jax.ShapeDtypeStruct((bsz, n, cc), F32)] * 3,
        compiler_params=_params("arbitrary", "arbitrary"),
        name="in_proj",
    )(x, norm_g.reshape(1, d), shift, scale, w_bf16, b_in.reshape(1, n4))


def _lru_body(x_ref, h0_ref, cw_ref, cb_ref, wg_ref, bg_ref, lam_ref, h_ref, hl_ref,
              ext_ref, a_ref, u_ref, carry_ref, halo_ref, *, reverse, tl):
    i = pl.program_id(1)
    ch = x_ref.shape[2]

    @pl.when(i == 0)
    def _():
        carry_ref[...] = jnp.broadcast_to(h0_ref[0], carry_ref.shape)
        halo_ref[...] = jnp.zeros(halo_ref.shape, F32)

    x = x_ref[0]
    if not reverse:
        ext_ref[0:SUBLANES, :] = halo_ref[...]
        ext_ref[SUBLANES:SUBLANES + tl, :] = x
        taps = [ext_ref[SUBLANES - (LRU_CONV_WIDTH - 1) + j:SUBLANES - (LRU_CONV_WIDTH - 1) + j + tl, :]
                for j in range(LRU_CONV_WIDTH)]
        halo_ref[...] = x[tl - SUBLANES:tl, :]
    else:
        ext_ref[0:tl, :] = x
        ext_ref[tl:tl + SUBLANES, :] = halo_ref[...]
        taps = [ext_ref[(LRU_CONV_WIDTH - 1) - j:(LRU_CONV_WIDTH - 1) - j + tl, :]
                for j in range(LRU_CONV_WIDTH)]
        halo_ref[...] = x[0:SUBLANES, :]
    xc = cb_ref[...]
    for j in range(LRU_CONV_WIDTH):
        xc = xc + cw_ref[j:j + 1, :] * taps[j]

    z = jnp.dot(xc.astype(BF16), wg_ref[...], preferred_element_type=F32) + bg_ref[...]
    r = jax.nn.sigmoid(z[:, :ch])
    ig = jax.nn.sigmoid(z[:, ch:])
    log_a = (-LRU_C) * r * jax.nn.softplus(-lam_ref[...])
    th = jnp.tanh(log_a)
    a_ref[...] = jnp.exp(log_a)
    u_ref[...] = jnp.sqrt(-2.0 * th / (1.0 - th)) * (ig * xc)

    ng = tl // SUBLANES
    rowid = lax.broadcasted_iota(jnp.int32, (SUBLANES, ch), 0)

    def group(gi, h):
        g = (ng - 1 - gi) if reverse else gi
        off = pl.multiple_of(g * SUBLANES, SUBLANES)
        a = a_ref[pl.ds(off, SUBLANES), :]
        u = u_ref[pl.ds(off, SUBLANES), :]
        for s in (1, 2, 4):
            if reverse:
                m = rowid < SUBLANES - s
                sh = SUBLANES - s
            else:
                m = rowid >= s
                sh = s
            ap = jnp.where(m, pltpu.roll(a, sh, 0), 1.0)
            up = jnp.where(m, pltpu.roll(u, sh, 0), 0.0)
            u = a * up + u
            a = a * ap
        hh = a * h + u
        h_ref[0, pl.ds(off, SUBLANES), :] = hh
        edge = hh[0:1, :] if reverse else hh[SUBLANES - 1:SUBLANES, :]
        return jnp.broadcast_to(edge, (SUBLANES, ch))

    h = lax.fori_loop(0, ng, group, carry_ref[...], unroll=4)
    carry_ref[...] = h
    hl_ref[0] = h[0:1, :]


def _lru_dir(lx, h0, cw, cb, wgate, bgate, lam, reverse, tl):
    bsz, n, ch = lx.shape
    nc = n // tl
    cmap = (lambda b, i: (b, nc - 1 - i, 0)) if reverse else (lambda b, i: (b, i, 0))
    full = lambda shape: pl.BlockSpec(shape, lambda b, i: (0,) * len(shape))
    return pl.pallas_call(
        functools.partial(_lru_body, reverse=reverse, tl=tl),
        grid=(bsz, nc),
        in_specs=[pl.BlockSpec((1, tl, ch), cmap),
                  pl.BlockSpec((1, 1, ch), lambda b, i: (b, 0, 0)),
                  full((LRU_CONV_WIDTH, ch)), full((1, ch)),
                  full((ch, 2 * ch)), full((1, 2 * ch)), full((1, ch))],
        out_specs=[pl.BlockSpec((1, tl, ch), cmap),
                   pl.BlockSpec((1, 1, ch), lambda b, i: (b, 0, 0))],
        out_shape=[jax.ShapeDtypeStruct((bsz, n, ch), F32),
                   jax.ShapeDtypeStruct((bsz, 1, ch), F32)],
        scratch_shapes=[pltpu.VMEM((tl + SUBLANES, ch), F32),
                        pltpu.VMEM((tl, ch), F32),
                        pltpu.VMEM((tl, ch), F32),
                        pltpu.VMEM((SUBLANES, ch), F32),
                        pltpu.VMEM((SUBLANES, ch), F32)],
        compiler_params=_params("arbitrary", "arbitrary"),
        name="lru_rev" if reverse else "lru_fwd",
    )(lx, h0, cw, cb.reshape(1, ch), wgate, bgate.reshape(1, 2 * ch), lam.reshape(1, ch))


ROW_STRIDE = GRID_W + 2 * BF16_ROWS


def _conv_row_body(x_ref, w_ref, b_ref, o_ref, pad_ref):
    n = x_ref.shape[1]
    nrows = n // GRID_W
    gap = jnp.zeros((BF16_ROWS, LANES), F32)

    def fill(r, c):
        base = pl.multiple_of(r * ROW_STRIDE, SUBLANES)
        src = pl.multiple_of(r * GRID_W, SUBLANES)
        pad_ref[pl.ds(base, BF16_ROWS), :] = gap
        pad_ref[pl.ds(base + BF16_ROWS, GRID_W), :] = x_ref[0, pl.ds(src, GRID_W), :]
        pad_ref[pl.ds(base + BF16_ROWS + GRID_W, BF16_ROWS), :] = gap
        return c

    lax.fori_loop(0, nrows, fill, 0)

    def row(r, c):
        base = pl.multiple_of(r * ROW_STRIDE, SUBLANES)
        acc = jnp.broadcast_to(b_ref[...], (GRID_W, LANES))
        for k in range(CONV_WIDTH):
            acc = acc + w_ref[k:k + 1, :] * pad_ref[pl.ds(base + BF16_ROWS - CONV_PAD + k, GRID_W), :]
        o_ref[0, pl.ds(pl.multiple_of(r * GRID_W, SUBLANES), GRID_W), :] = acc
        return c

    lax.fori_loop(0, nrows, row, 0)


def _conv_col_body(x_ref, w_ref, b_ref, o_ref, pad_ref):
    n = x_ref.shape[1]
    nrows = n // GRID_W
    halo = CONV_PAD * GRID_W
    pad_ref[0:halo, :] = jnp.zeros((halo, LANES), F32)
    pad_ref[halo + n:halo + n + halo, :] = jnp.zeros((halo, LANES), F32)

    def fill(r, c):
        src = pl.multiple_of(r * GRID_W, SUBLANES)
        pad_ref[pl.ds(halo + src, GRID_W), :] = x_ref[0, pl.ds(src, GRID_W), :]
        return c

    lax.fori_loop(0, nrows, fill, 0)

    def row(r, c):
        base = pl.multiple_of(r * GRID_W, SUBLANES)
        acc = jnp.broadcast_to(b_ref[...], (GRID_W, LANES))
        for k in range(CONV_WIDTH):
            acc = acc + w_ref[k:k + 1, :] * pad_ref[pl.ds(base + k * GRID_W, GRID_W), :]
        o_ref[0, pl.ds(base, GRID_W), :] = acc
        return c

    lax.fori_loop(0, nrows, row, 0)


def _conv_grid(glu, w, b):
    bsz, n, ch = glu.shape
    half = ch // 2
    ng = half // LANES
    outs = []
    for body, first, pad_rows in ((_conv_row_body, 0, (n // GRID_W) * ROW_STRIDE),
                                  (_conv_col_body, ng, n + 2 * CONV_PAD * GRID_W)):
        outs.append(pl.pallas_call(
            body,
            grid=(bsz, ng),
            in_specs=[pl.BlockSpec((1, n, LANES), lambda bb, g, first=first: (bb, 0, g + first)),
                      pl.BlockSpec((CONV_WIDTH, LANES), lambda bb, g, first=first: (0, g + first)),
                      pl.BlockSpec((1, LANES), lambda bb, g, first=first: (0, g + first))],
            out_specs=pl.BlockSpec((1, n, LANES), lambda bb, g: (bb, 0, g)),
            out_shape=jax.ShapeDtypeStruct((bsz, n, half), F32),
            scratch_shapes=[pltpu.VMEM((pad_rows, LANES), F32)],
            compiler_params=_params("arbitrary", "arbitrary"),
            name="conv_row" if first == 0 else "conv_col",
        )(glu, w, b.reshape(1, ch)))
    return outs


def _out_body(cr_ref, cc_ref, lng_ref, lnb_ref, hf_ref, hb_ref, glg_ref, wo_ref, bo_ref, x_ref,
              g1_ref, n2g_ref, sh2_ref, sc2_ref, rw_ref, x1_ref, v_ref, aff_ref):
    cv = jnp.concatenate([cr_ref[0], cc_ref[0]], axis=-1)
    mu = jnp.mean(cv, axis=-1, keepdims=True)
    dv = cv - mu
    var = jnp.mean(dv * dv, axis=-1, keepdims=True)
    yn = dv * lax.rsqrt(var + LN_EPS) * lng_ref[...] + lnb_ref[...]
    cy = yn * jax.nn.sigmoid(yn)
    yl = (hf_ref[0] + hb_ref[0]) * glg_ref[0]
    cat = jnp.concatenate([cy, yl], axis=-1).astype(BF16)
    m = jnp.dot(cat, wo_ref[...], preferred_element_type=F32) + bo_ref[...]
    x1 = x_ref[0] + g1_ref[0] * m
    x1_ref[0] = x1
    ms = jnp.mean(x1 * x1, axis=-1, keepdims=True)
    v = x1 * lax.rsqrt(ms + RMS_EPS) * n2g_ref[...]
    v = v * (1.0 + sc2_ref[0]) + sh2_ref[0]
    v_ref[0] = v
    lg = lax.dot_general(rw_ref[...], v, (((1,), (1,)), ((), ())), precision=HIGHEST,
                         preferred_element_type=F32)
    ex = jnp.exp(lg - jnp.max(lg, axis=0, keepdims=True))
    aff_ref[0] = ex / jnp.sum(ex, axis=0, keepdims=True)


def _out_proj_route(conv_r, conv_c, ln_g, ln_b, hf, hb, glg, wo_bf16, b_out, x, g1, n2g, sh2, sc2, rw_t, tm):
    bsz, n, d = x.shape
    half = conv_r.shape[2]
    ch = hf.shape[2]
    ne = rw_t.shape[0]
    tok = lambda c: pl.BlockSpec((1, tm, c), lambda b, i: (b, i, 0))
    full = lambda shape: pl.BlockSpec(shape, lambda b, i: (0,) * len(shape))
    per_b = pl.BlockSpec((1, 1, d), lambda b, i: (b, 0, 0))
    return pl.pallas_call(
        _out_body,
        grid=(bsz, n // tm),
        in_specs=[tok(half), tok(half), full((1, 2 * half)), full((1, 2 * half)),
                  tok(ch), tok(ch), tok(ch), full((2 * half + ch, d)), full((1, d)), tok(d),
                  per_b, full((1, d)), per_b, per_b, full((ne, d))],
        out_specs=[tok(d), tok(d), pl.BlockSpec((1, ne, tm), lambda b, i: (b, 0, i))],
        out_shape=[jax.ShapeDtypeStruct((bsz, n, d), F32),
                   jax.ShapeDtypeStruct((bsz, n, d), F32),
                   jax.ShapeDtypeStruct((bsz, ne, n), F32)],
        compiler_params=_params("arbitrary", "arbitrary"),
        name="out_proj_route",
    )(conv_r, conv_c, ln_g.reshape(1, -1), ln_b.reshape(1, -1), hf, hb, glg, wo_bf16,
      b_out.reshape(1, d), x, g1, n2g.reshape(1, d), sh2, sc2, rw_t)


def _token_cumsum(m, rows_per_expert):
    er = m.shape[0]
    li = lax.broadcasted_iota(jnp.int32, (LANES, LANES), 0)
    lj = lax.broadcasted_iota(jnp.int32, (LANES, LANES), 1)
    upper = jnp.where(li <= lj, 1.0, 0.0).astype(BF16)
    cs = jnp.dot(m.astype(BF16), upper, preferred_element_type=F32)
    rt = jnp.broadcast_to(cs[:, LANES - 1:LANES], (er, LANES)).astype(BF16)
    ri = lax.broadcasted_iota(jnp.int32, (er, er), 0)
    ci = lax.broadcasted_iota(jnp.int32, (er, er), 1)
    same = (ri // rows_per_expert) == (ci // rows_per_expert)
    lower = jnp.where(same & (ci < ri), 1.0, 0.0).astype(BF16)
    rstart = jnp.dot(lower, rt, preferred_element_type=F32)
    return cs + rstart, rstart


def _topk_body(aff_ref, idx_ref, gate_ref, pos_ref, rs_ref, cpos_scr, *, cap):
    aff = aff_ref[0]
    ne, rows, _ = aff.shape

    def bisect(i, thr):
        cand = thr | jnp.left_shift(jnp.int32(1), 30 - i)
        cnt = jnp.sum((aff >= pltpu.bitcast(cand, F32)).astype(jnp.int32), axis=(1, 2), keepdims=True)
        return jnp.where(cnt >= cap, cand, thr)

    thr = lax.fori_loop(0, 31, bisect, jnp.zeros((ne, 1, 1), jnp.int32))
    gt = aff >= pltpu.bitcast(thr + 1, F32)
    eq = (aff >= pltpu.bitcast(thr, F32)) & jnp.logical_not(gt)
    need = (cap - jnp.sum(gt.astype(jnp.int32), axis=(1, 2), keepdims=True)).astype(F32)
    eqf = jnp.where(eq, 1.0, 0.0)
    rank_incl, _ = _token_cumsum(eqf.reshape(ne * rows, LANES), rows)
    rank_excl = rank_incl.reshape(ne, rows, LANES) - eqf
    sel = gt | (eq & (rank_excl < need))
    self_ = jnp.where(sel, 1.0, 0.0)
    cpos, rstart = _token_cumsum(self_.reshape(ne * rows, LANES), rows)
    cpos3 = cpos.reshape(ne, rows, LANES)
    cpos_scr[...] = cpos3
    pos_ref[0] = jnp.where(sel, cpos3 - 1.0, -1.0).astype(jnp.int32)
    rs_ref[0] = rstart.reshape(ne, rows, LANES).astype(jnp.int32)

    jrow = lax.broadcasted_iota(jnp.int32, (1, cap), 1).astype(F32)
    sub_r = lax.broadcasted_iota(jnp.int32, (rows, cap), 0).astype(F32)
    sub_l = lax.broadcasted_iota(jnp.int32, (LANES, cap), 0).astype(F32)

    def per_expert(e, c):
        cp = cpos_scr[e]
        af = aff_ref[0, e]
        rowend = cp[:, LANES - 1:LANES]
        rj = jnp.sum(jnp.where(rowend <= jrow, 1.0, 0.0), axis=0, keepdims=True)
        onehot = jnp.where(sub_r == rj, 1.0, 0.0)
        tn = (((0,), (0,)), ((), ()))
        g = lax.dot_general(cp, onehot, tn, precision=HIGHEST, preferred_element_type=F32)
        lanepos = jnp.sum(jnp.where(g <= jrow, 1.0, 0.0), axis=0, keepdims=True)
        ga = lax.dot_general(af, onehot, tn, precision=HIGHEST, preferred_element_type=F32)
        gate = jnp.sum(jnp.where(sub_l == lanepos, ga, 0.0), axis=0, keepdims=True)
        idx_ref[0, pl.ds(e, 1), :] = (rj * float(LANES) + lanepos).astype(jnp.int32)
        gate_ref[0, pl.ds(e, 1), :] = gate
        return c

    lax.fori_loop(0, ne, per_expert, 0)


def _topk_route(aff4, cap):
    bsz, ne, rows, _ = aff4.shape
    blk4 = pl.BlockSpec((1, ne, rows, LANES), lambda b: (b, 0, 0, 0))
    lst = pl.BlockSpec((1, ne, cap), lambda b: (b, 0, 0))
    return pl.pallas_call(
        functools.partial(_topk_body, cap=cap),
        grid=(bsz,),
        in_specs=[blk4],
        out_specs=[lst, lst, blk4, blk4],
        out_shape=[jax.ShapeDtypeStruct((bsz, ne, cap), jnp.int32),
                   jax.ShapeDtypeStruct((bsz, ne, cap), F32),
                   jax.ShapeDtypeStruct((bsz, ne, rows, LANES), jnp.int32),
                   jax.ShapeDtypeStruct((bsz, ne, rows, LANES), jnp.int32)],
        scratch_shapes=[pltpu.VMEM((ne, rows, LANES), F32)],
        compiler_params=_params("arbitrary"),
        name="topk_route",
    )(aff4)


def _ffn_body(idx_ref, v_hbm, gate_ref, wg_ref, wu_ref, wd_ref, ys_ref, ring, xb, acc, sems, *, mg, nf, chunk):
    e = pl.program_id(0)
    g = pl.program_id(1)
    f = pl.program_id(2)
    ngroups = pl.num_programs(1)
    grp = e * ngroups + g
    last_grp = pl.num_programs(0) * ngroups - 1
    cur = grp % 2
    first = jnp.logical_and(grp == 0, f == 0)

    def chunk_start(c):
        return pl.multiple_of(jnp.minimum(c * chunk, mg - chunk), BF16_ROWS)

    def row_copy(src_row, slot, r):
        return pltpu.make_async_copy(v_hbm.at[pl.ds(src_row, 1)], ring.at[slot, pl.ds(r, 1)], sems.at[slot])

    def land(c, slot, xslot):
        pltpu.make_async_copy(v_hbm.at[pl.ds(0, chunk)], ring.at[slot], sems.at[slot]).wait()
        xb[xslot, pl.ds(chunk_start(c), chunk), :] = ring[slot].astype(BF16)

    @pl.when(first)
    def _():
        def one_chunk(c, carry):
            base = chunk_start(c)

            def issue(r, cc):
                row_copy(idx_ref[base + r], 0, r).start()
                return cc

            lax.fori_loop(0, chunk, issue, 0, unroll=8)
            land(c, 0, 0)
            return carry

        lax.fori_loop(0, nf, one_chunk, 0)

    @pl.when(jnp.logical_not(first))
    def _():
        lc = jnp.where(f == 0, nf - 1, f - 1)
        land(lc, lc % 2, jnp.where(f == 0, cur, 1 - cur))

    @pl.when(f == 0)
    def _():
        acc[...] = jnp.zeros(acc.shape, F32)

    nxt_base = jnp.minimum(grp + 1, last_grp) * mg + chunk_start(f)
    for r in range(chunk):
        row_copy(idx_ref[nxt_base + r], f % 2, r).start()

    x = xb[cur]
    gg = jnp.dot(x, wg_ref[0].astype(BF16), preferred_element_type=F32)
    uu = jnp.dot(x, wu_ref[0].astype(BF16), preferred_element_type=F32)
    h = (gg * jax.nn.sigmoid(gg) * uu).astype(BF16)
    acc[...] += jnp.dot(h, wd_ref[0].astype(BF16), preferred_element_type=F32)

    @pl.when(f == nf - 1)
    def _():
        ys_ref[0] = (acc[...] * gate_ref[0]).astype(BF16)

    @pl.when(jnp.logical_and(grp == last_grp, f == nf - 1))
    def _():
        pltpu.make_async_copy(v_hbm.at[pl.ds(0, chunk)], ring.at[(nf - 1) % 2], sems.at[(nf - 1) % 2]).wait()


def _expert_ffn(idx_flat, v_flat, gate_col, wg, wu, wd, mg, tf):
    ne, d, fdim = wg.shape
    rows = gate_col.shape[1]
    nf = fdim // tf
    chunk = pl.cdiv(pl.cdiv(mg, nf), BF16_ROWS) * BF16_ROWS
    return pl.pallas_call(
        functools.partial(_ffn_body, mg=mg, nf=nf, chunk=chunk),
        grid_spec=pltpu.PrefetchScalarGridSpec(
            num_scalar_prefetch=1,
            grid=(ne, rows // mg, nf),
            in_specs=[pl.BlockSpec(memory_space=pl.ANY),
                      pl.BlockSpec((1, mg, 1), lambda e, g, f, idx: (e, g, 0)),
                      pl.BlockSpec((1, d, tf), lambda e, g, f, idx: (e, 0, f)),
                      pl.BlockSpec((1, d, tf), lambda e, g, f, idx: (e, 0, f)),
                      pl.BlockSpec((1, tf, d), lambda e, g, f, idx: (e, f, 0))],
            out_specs=pl.BlockSpec((1, mg, d), lambda e, g, f, idx: (e, g, 0)),
            scratch_shapes=[pltpu.VMEM((2, chunk, d), F32),
                            pltpu.VMEM((2, mg, d), BF16),
                            pltpu.VMEM((mg, d), F32),
                            pltpu.SemaphoreType.DMA((2,))]),
        out_shape=jax.ShapeDtypeStruct((ne, rows, d), BF16),
        compiler_params=_params("arbitrary", "arbitrary", "arbitrary"),
        name="expert_ffn",
    )(idx_flat, v_flat, gate_col, wg, wu, wd)


SLAB = 48


def _combine_body(rs_ref, pos_ref, ys_hbm, x1_ref, g5_ref, fng_ref, o_ref, stk, stk_x, sems, sem_x, *, cap):
    b = pl.program_id(0)
    r = pl.program_id(1)
    nb = pl.num_programs(0)
    nr = pl.num_programs(1)
    ne = pos_ref.shape[2]
    step = b * nr + r
    slot = step % 2

    def first_pos(bb, rr, e):
        return rs_ref[(bb * ne + e) * (nr + 1) + rr]

    def slab_start(p0, rnd):
        a = lax.shift_left(lax.shift_right_logical(p0, 4), 4) + rnd * SLAB
        return pl.multiple_of(jnp.minimum(a, cap - SLAB), BF16_ROWS)

    def slab_copy(bb, e, a, dst, sem):
        return pltpu.make_async_copy(ys_hbm.at[e, pl.ds(bb * cap + a, SLAB)],
                                     dst.at[pl.ds(e * SLAB, SLAB)], sem)

    def issue(bb, rr, sl):
        for e in range(ne):
            slab_copy(bb, e, slab_start(first_pos(bb, rr, e), 0), stk.at[sl], sems.at[sl]).start()

    @pl.when(step == 0)
    def _():
        issue(b, r, slot)

    @pl.when(step + 1 < nb * nr)
    def _():
        nxt = step + 1
        issue(nxt // nr, nxt % nr, 1 - slot)

    pos = pos_ref[0, 0]
    kiota = lax.broadcasted_iota(jnp.int32, (SLAB, LANES), 0)
    tn = (((0,), (0,)), ((), ()))

    def onehot(rnd):
        blocks = []
        for e in range(ne):
            p0 = first_pos(b, r, e)
            a0 = lax.shift_left(lax.shift_right_logical(p0, 4), 4)
            rel = pos[e:e + 1, :] - slab_start(p0, rnd)
            fresh = (pos[e:e + 1, :] - a0) >= rnd * SLAB
            blocks.append(jnp.where((kiota == rel) & fresh, 1.0, 0.0))
        return jnp.concatenate(blocks, axis=0).astype(BF16)

    pltpu.make_async_copy(ys_hbm.at[0, pl.ds(0, ne * SLAB)], stk.at[slot], sems.at[slot]).wait()
    moe = lax.dot_general(onehot(0), stk[slot], tn, preferred_element_type=F32)

    rounds = jnp.int32(1)
    for e in range(ne):
        p0 = first_pos(b, r, e)
        a0 = lax.shift_left(lax.shift_right_logical(p0, 4), 4)
        span = first_pos(b, r + 1, e) - a0
        rounds = jnp.maximum(rounds, (span + SLAB - 1) // SLAB)

    def extra(rnd, m):
        for e in range(ne):
            slab_copy(b, e, slab_start(first_pos(b, r, e), rnd), stk_x, sem_x).start()
        pltpu.make_async_copy(ys_hbm.at[0, pl.ds(0, ne * SLAB)], stk_x, sem_x).wait()
        return m + lax.dot_general(onehot(rnd), stk_x[...], tn, preferred_element_type=F32)

    moe = lax.fori_loop(1, rounds, extra, moe)

    y = x1_ref[0] + g5_ref[0] * moe
    ms = jnp.mean(y * y, axis=-1, keepdims=True)
    o_ref[0] = y * lax.rsqrt(ms + RMS_EPS) * fng_ref[...]


def _combine_norm(rs_ext, pos_r, ys, x1, g5, fng, cap):
    bsz, n, d = x1.shape
    nr = n // LANES
    ne = pos_r.shape[2]
    return pl.pallas_call(
        functools.partial(_combine_body, cap=cap),
        grid_spec=pltpu.PrefetchScalarGridSpec(
            num_scalar_prefetch=1,
            grid=(bsz, nr),
            in_specs=[pl.BlockSpec((1, 1, ne, LANES), lambda b, r, rs: (b, r, 0, 0)),
                      pl.BlockSpec(memory_space=pl.ANY),
                      pl.BlockSpec((1, LANES, d), lambda b, r, rs: (b, r, 0)),
                      pl.BlockSpec((1, 1, d), lambda b, r, rs: (b, 0, 0)),
                      pl.BlockSpec((1, d), lambda b, r, rs: (0, 0))],
            out_specs=pl.BlockSpec((1, LANES, d), lambda b, r, rs: (b, r, 0)),
            scratch_shapes=[pltpu.VMEM((2, ne * SLAB, d), BF16),
                            pltpu.VMEM((ne * SLAB, d), BF16),
                            pltpu.SemaphoreType.DMA((2,)),
                            pltpu.SemaphoreType.DMA(())]),
        out_shape=jax.ShapeDtypeStruct((bsz, n, d), F32),
        compiler_params=_params("arbitrary", "arbitrary"),
        name="combine_norm",
    )(rs_ext, pos_r, ys, x1, g5, fng.reshape(1, d))


def _block_diag(w):
    heads, hd, _ = w.shape
    eye = jnp.eye(heads, dtype=w.dtype)
    return (eye[:, None, :, None] * w[:, :, None, :]).reshape(heads * hd, heads * hd)


def _tile(n, pref):
    return pref if n % pref == 0 else n


def kernel(x, c, ctx, c_ctx, norm1_g, norm2_g, ada_w, ada_b, w_in, b_in, conv_dw_w, conv_dw_b, conv_ln_g, conv_ln_b, lru_conv_w, lru_conv_b, lru_wa, lru_ba, lru_wi, lru_bi, lru_lambda, w_out, b_out, router_w, exp_w_gate, exp_w_up, exp_w_down, final_norm_g):
    assert norm1_g.shape[0] == 1
    mod, x1, v, aff = _mixer(x, c, ctx, c_ctx, norm1_g[0], norm2_g[0], ada_w[0], ada_b[0], w_in[0], b_in[0],
                             conv_dw_w[0], conv_dw_b[0], conv_ln_g[0], conv_ln_b[0], lru_conv_w[0],
                             lru_conv_b[0], lru_wa[0], lru_ba[0], lru_wi[0], lru_bi[0], lru_lambda[0],
                             w_out[0], b_out[0], router_w[0])
    return _moe_norm(x1, v, aff, mod[5], exp_w_gate[0], exp_w_up[0], exp_w_down[0], final_norm_g)


def _mixer(x, c, ctx, c_ctx, norm1_g, norm2_g, ada_w, ada_b, w_in, b_in, conv_w, conv_b, ln_g, ln_b,
           lru_cw, lru_cb, lru_wa, lru_ba, lru_wi, lru_bi, lru_lam, w_out, b_out, router_w):
    bsz, n, d = x.shape
    cond8 = jnp.zeros((SUBLANES, d), F32).at[:bsz].set(c).at[bsz].set(c_ctx)
    mods = _ada_mod(cond8, ada_w, ada_b)
    mod = [mods[:bsz, k * d:(k + 1) * d].reshape(bsz, 1, d) for k in range(N_MOD)]
    mod_c = [jnp.broadcast_to(mods[bsz:bsz + 1, k * d:(k + 1) * d].reshape(1, 1, d), (bsz, 1, d))
             for k in range(2)]
    w_in_b = w_in.astype(BF16)
    w_out_b = w_out.astype(BF16)
    ch = lru_cb.shape[1]
    wgate = [jnp.concatenate([_block_diag(lru_wa[dd]), _block_diag(lru_wi[dd])], axis=1).astype(BF16)
             for dd in range(2)]
    bgate = [jnp.concatenate([lru_ba[dd], lru_bi[dd]]) for dd in range(2)]

    def lru(lx, h0, dd, reverse):
        return _lru_dir(lx, h0, lru_cw[dd], lru_cb[dd], wgate[dd], bgate[dd], lru_lam[dd], reverse,
                        _tile(lx.shape[1], 256))

    _, c_lx, _ = _in_proj(ctx, norm1_g, mod_c[0], mod_c[1], w_in_b, b_in, _tile(ctx.shape[1], 256))
    zero_h = jnp.zeros((bsz, 1, ch), F32)
    _, hf0 = lru(c_lx, zero_h, 0, False)
    _, hb0 = lru(c_lx, zero_h, 1, True)

    x_glu, x_lx, x_glg = _in_proj(x, norm1_g, mod[0], mod[1], w_in_b, b_in, _tile(n, 512))
    hf, _ = lru(x_lx, hf0, 0, False)
    hb, _ = lru(x_lx, hb0, 1, True)
    conv_r, conv_c = _conv_grid(x_glu, conv_w, conv_b)
    x1, v, aff = _out_proj_route(conv_r, conv_c, ln_g, ln_b, hf, hb, x_glg, w_out_b, b_out, x, mod[2],
                                 norm2_g, mod[3], mod[4], router_w.T, _tile(n, 512))
    return mod, x1, v, aff


def _moe_norm(x1, v, aff, gate2, wg, wu, wd, final_norm_g):
    bsz, n, d = x1.shape
    ne = aff.shape[1]
    cap = EC_CAPACITY * n // ne
    idx, gate, pos, rstart = _topk_route(aff.reshape(bsz, ne, n // LANES, LANES), cap)
    tok_base = (jnp.arange(bsz, dtype=jnp.int32) * n)[:, None, None]
    idx_flat = jnp.swapaxes(idx + tok_base, 0, 1).reshape(ne * bsz * cap)
    gate_col = jnp.swapaxes(gate, 0, 1).reshape(ne, bsz * cap, 1)
    ys = _expert_ffn(idx_flat, v.reshape(bsz * n, d), gate_col, wg, wu, wd, 2 * cap, 256)
    rs_ext = jnp.concatenate([rstart[..., 0], jnp.full((bsz, ne, 1), cap, jnp.int32)], axis=-1)
    return _combine_norm(rs_ext.reshape(-1), jnp.swapaxes(pos, 1, 2), ys, x1, gate2, final_norm_g, cap)
```

```python
import functools

import jax
import jax.numpy as jnp
from jax import lax
from jax.experimental import pallas as pl
from jax.experimental.pallas import tpu as pltpu

GRID_W = 64
CONV_WIDTH = 31
CONV_PAD = (CONV_WIDTH - 1) // 2
LRU_CONV_WIDTH = 4
LRU_C = 8.0
N_EXPERTS = 16
EC_CAPACITY = 2
N_MOD = 6
RMS_EPS = 1e-6
LN_EPS = 1e-5

LANES = 128
SUBLANES = 8
BF16_ROWS = 16
VMEM_LIMIT = 56 * 1024 * 1024

F32 = jnp.float32
BF16 = jnp.bfloat16
HIGHEST = lax.Precision.HIGHEST


def _params(*sem):
    return pltpu.CompilerParams(dimension_semantics=sem, vmem_limit_bytes=VMEM_LIMIT)


def _ada_body(c_ref, w_ref, b_ref, o_ref):
    s = c_ref[...]
    s = s * jax.nn.sigmoid(s)
    o_ref[...] = jnp.dot(s, w_ref[...], precision=HIGHEST, preferred_element_type=F32) + b_ref[...]


def _ada_mod(cond8, ada_w, ada_b):
    d, n = ada_w.shape
    tn = n // 4
    return pl.pallas_call(
        _ada_body,
        grid=(n // tn,),
        in_specs=[pl.BlockSpec((SUBLANES, d), lambda j: (0, 0)),
                  pl.BlockSpec((d, tn), lambda j: (0, j)),
                  pl.BlockSpec((1, tn), lambda j: (0, j))],
        out_specs=pl.BlockSpec((SUBLANES, tn), lambda j: (0, j)),
        out_shape=jax.ShapeDtypeStruct((SUBLANES, n), F32),
        compiler_params=_params("arbitrary"),
        name="ada_mod",
    )(cond8, ada_w, ada_b.reshape(1, n))


def _inproj_body(x_ref, g_ref, sh_ref, sc_ref, w_ref, b_ref, glu_ref, lx_ref, glg_ref):
    x = x_ref[0]
    ms = jnp.mean(x * x, axis=-1, keepdims=True)
    y = x * lax.rsqrt(ms + RMS_EPS) * g_ref[...]
    u = y * (1.0 + sc_ref[0]) + sh_ref[0]
    p = jnp.dot(u.astype(BF16), w_ref[...], preferred_element_type=F32) + b_ref[...]
    cc = p.shape[1] // 4
    glu_ref[0] = p[:, :cc] * jax.nn.sigmoid(p[:, cc:2 * cc])
    lx_ref[0] = p[:, 2 * cc:3 * cc]
    glg_ref[0] = jax.nn.gelu(p[:, 3 * cc:])


def _in_proj(x, norm_g, shift, scale, w_bf16, b_in, tm):
    bsz, n, d = x.shape
    n4 = w_bf16.shape[1]
    cc = n4 // 4
    tok = pl.BlockSpec((1, tm, cc), lambda b, i: (b, i, 0))
    return pl.pallas_call(
        _inproj_body,
        grid=(bsz, n // tm),
        in_specs=[pl.BlockSpec((1, tm, d), lambda b, i: (b, i, 0)),
                  pl.BlockSpec((1, d), lambda b, i: (0, 0)),
                  pl.BlockSpec((1, 1, d), lambda b, i: (b, 0, 0)),
                  pl.BlockSpec((1, 1, d), lambda b, i: (b, 0, 0)),
                  pl.BlockSpec((d, n4), lambda b, i: (0, 0)),
                  pl.BlockSpec((1, n4), lambda b, i: (0, 0))],
        out_specs=[tok, tok, tok],
        out_shape=[jax.ShapeDtypeStruct((bsz, n, cc), F32)] * 3,
        compiler_params=_params("arbitrary", "arbitrary"),
        name="in_proj",
    )(x, norm_g.reshape(1, d), shift, scale, w_bf16, b_in.reshape(1, n4))


def _lru_body(x_ref, h0_ref, cw_ref, cb_ref, wg_ref, bg_ref, lam_ref, h_ref, hl_ref,
              ext_ref, a_ref, u_ref, carry_ref, halo_ref, *, reverse, tl):
    i = pl.program_id(1)
    ch = x_ref.shape[2]

    @pl.when(i == 0)
    def _():
        carry_ref[...] = jnp.broadcast_to(h0_ref[0], carry_ref.shape)
        halo_ref[...] = jnp.zeros(halo_ref.shape, F32)

    x = x_ref[0]
    if not reverse:
        ext_ref[0:SUBLANES, :] = halo_ref[...]
        ext_ref[SUBLANES:SUBLANES + tl, :] = x
        taps = [ext_ref[SUBLANES - (LRU_CONV_WIDTH - 1) + j:SUBLANES - (LRU_CONV_WIDTH - 1) + j + tl, :]
                for j in range(LRU_CONV_WIDTH)]
        halo_ref[...] = x[tl - SUBLANES:tl, :]
    else:
        ext_ref[0:tl, :] = x
        ext_ref[tl:tl + SUBLANES, :] = halo_ref[...]
        taps = [ext_ref[(LRU_CONV_WIDTH - 1) - j:(LRU_CONV_WIDTH - 1) - j + tl, :]
                for j in range(LRU_CONV_WIDTH)]
        halo_ref[...] = x[0:SUBLANES, :]
    xc = cb_ref[...]
    for j in range(LRU_CONV_WIDTH):
        xc = xc + cw_ref[j:j + 1, :] * taps[j]

    z = jnp.dot(xc.astype(BF16), wg_ref[...], preferred_element_type=F32) + bg_ref[...]
    r = jax.nn.sigmoid(z[:, :ch])
    ig = jax.nn.sigmoid(z[:, ch:])
    log_a = (-LRU_C) * r * jax.nn.softplus(-lam_ref[...])
    th = jnp.tanh(log_a)
    a_ref[...] = jnp.exp(log_a)
    m2 = -2.0 * th / (1.0 - th)
    u_ref[...] = jnp.where(m2 > 0.0, m2 * lax.rsqrt(m2), 0.0) * (ig * xc)

    ng = tl // SUBLANES
    rowid = lax.broadcasted_iota(jnp.int32, (SUBLANES, ch), 0)

    def group(gi, h):
        g = (ng - 1 - gi) if reverse else gi
        off = pl.multiple_of(g * SUBLANES, SUBLANES)
        a = a_ref[pl.ds(off, SUBLANES), :]
        u = u_ref[pl.ds(off, SUBLANES), :]
        for s in (1, 2, 4):
            if reverse:
                m = rowid < SUBLANES - s
                sh = SUBLANES - s
            else:
                m = rowid >= s
                sh = s
            ap = jnp.where(m, pltpu.roll(a, sh, 0), 1.0)
            up = jnp.where(m, pltpu.roll(u, sh, 0), 0.0)
            u = a * up + u
            a = a * ap
        hh = a * h + u
        h_ref[0, pl.ds(off, SUBLANES), :] = hh
        edge = hh[0:1, :] if reverse else hh[SUBLANES - 1:SUBLANES, :]
        return jnp.broadcast_to(edge, (SUBLANES, ch))

    h = lax.fori_loop(0, ng, group, carry_ref[...], unroll=4)
    carry_ref[...] = h
    hl_ref[0] = h[0:1, :]


def _lru_dir(lx, h0, cw, cb, wgate, bgate, lam, reverse, tl):
    bsz, n, ch = lx.shape
    nc = n // tl
    cmap = (lambda b, i: (b, nc - 1 - i, 0)) if reverse else (lambda b, i: (b, i, 0))
    full = lambda shape: pl.BlockSpec(shape, lambda b, i: (0,) * len(shape))
    return pl.pallas_call(
        functools.partial(_lru_body, reverse=reverse, tl=tl),
        grid=(bsz, nc),
        in_specs=[pl.BlockSpec((1, tl, ch), cmap),
                  pl.BlockSpec((1, 1, ch), lambda b, i: (b, 0, 0)),
                  full((LRU_CONV_WIDTH, ch)), full((1, ch)),
                  full((ch, 2 * ch)), full((1, 2 * ch)), full((1, ch))],
        out_specs=[pl.BlockSpec((1, tl, ch), cmap),
                   pl.BlockSpec((1, 1, ch), lambda b, i: (b, 0, 0))],
        out_shape=[jax.ShapeDtypeStruct((bsz, n, ch), F32),
                   jax.ShapeDtypeStruct((bsz, 1, ch), F32)],
        scratch_shapes=[pltpu.VMEM((tl + SUBLANES, ch), F32),
                        pltpu.VMEM((tl, ch), F32),
                        pltpu.VMEM((tl, ch), F32),
                        pltpu.VMEM((SUBLANES, ch), F32),
                        pltpu.VMEM((SUBLANES, ch), F32)],
        compiler_params=_params("arbitrary", "arbitrary"),
        name="lru_rev" if reverse else "lru_fwd",
    )(lx, h0, cw, cb.reshape(1, ch), wgate, bgate.reshape(1, 2 * ch), lam.reshape(1, ch))


ROW_STRIDE = GRID_W + 2 * BF16_ROWS


def _conv_row_body(x_ref, w_ref, b_ref, o_ref, pad_ref):
    n = x_ref.shape[1]
    nrows = n // GRID_W
    gap = jnp.zeros((BF16_ROWS, LANES), F32)

    def fill(r, c):
        base = pl.multiple_of(r * ROW_STRIDE, SUBLANES)
        src = pl.multiple_of(r * GRID_W, SUBLANES)
        pad_ref[pl.ds(base, BF16_ROWS), :] = gap
        pad_ref[pl.ds(base + BF16_ROWS, GRID_W), :] = x_ref[0, pl.ds(src, GRID_W), :]
        pad_ref[pl.ds(base + BF16_ROWS + GRID_W, BF16_ROWS), :] = gap
        return c

    lax.fori_loop(0, nrows, fill, 0)

    def row(r, c):
        base = pl.multiple_of(r * ROW_STRIDE, SUBLANES)
        acc = jnp.broadcast_to(b_ref[...], (GRID_W, LANES))
        for k in range(CONV_WIDTH):
            acc = acc + w_ref[k:k + 1, :] * pad_ref[pl.ds(base + BF16_ROWS - CONV_PAD + k, GRID_W), :]
        o_ref[0, pl.ds(pl.multiple_of(r * GRID_W, SUBLANES), GRID_W), :] = acc
        return c

    lax.fori_loop(0, nrows, row, 0)


def _conv_col_body(x_ref, w_ref, b_ref, o_ref, pad_ref):
    n = x_ref.shape[1]
    nrows = n // GRID_W
    halo = CONV_PAD * GRID_W
    pad_ref[0:halo, :] = jnp.zeros((halo, LANES), F32)
    pad_ref[halo + n:halo + n + halo, :] = jnp.zeros((halo, LANES), F32)

    def fill(r, c):
        src = pl.multiple_of(r * GRID_W, SUBLANES)
        pad_ref[pl.ds(halo + src, GRID_W), :] = x_ref[0, pl.ds(src, GRID_W), :]
        return c

    lax.fori_loop(0, nrows, fill, 0)

    def row(r, c):
        base = pl.multiple_of(r * GRID_W, SUBLANES)
        acc = jnp.broadcast_to(b_ref[...], (GRID_W, LANES))
        for k in range(CONV_WIDTH):
            acc = acc + w_ref[k:k + 1, :] * pad_ref[pl.ds(base + k * GRID_W, GRID_W), :]
        o_ref[0, pl.ds(base, GRID_W), :] = acc
        return c

    lax.fori_loop(0, nrows, row, 0)


def _conv_grid(glu, w, b):
    bsz, n, ch = glu.shape
    half = ch // 2
    ng = half // LANES
    outs = []
    for body, first, pad_rows in ((_conv_row_body, 0, (n // GRID_W) * ROW_STRIDE),
                                  (_conv_col_body, ng, n + 2 * CONV_PAD * GRID_W)):
        outs.append(pl.pallas_call(
            body,
            grid=(bsz, ng),
            in_specs=[pl.BlockSpec((1, n, LANES), lambda bb, g, first=first: (bb, 0, g + first)),
                      pl.BlockSpec((CONV_WIDTH, LANES), lambda bb, g, first=first: (0, g + first)),
                      pl.BlockSpec((1, LANES), lambda bb, g, first=first: (0, g + first))],
            out_specs=pl.BlockSpec((1, n, LANES), lambda bb, g: (bb, 0, g)),
            out_shape=jax.ShapeDtypeStruct((bsz, n, half), F32),
            scratch_shapes=[pltpu.VMEM((pad_rows, LANES), F32)],
            compiler_params=_params("arbitrary", "arbitrary"),
            name="conv_row" if first == 0 else "conv_col",
        )(glu, w, b.reshape(1, ch)))
    return outs


def _out_body(cr_ref, cc_ref, lng_ref, lnb_ref, hf_ref, hb_ref, glg_ref, wo_ref, bo_ref, x_ref,
              g1_ref, n2g_ref, sh2_ref, sc2_ref, rw_ref, x1_ref, v_ref, aff_ref):
    cv = jnp.concatenate([cr_ref[0], cc_ref[0]], axis=-1)
    mu = jnp.mean(cv, axis=-1, keepdims=True)
    dv = cv - mu
    var = jnp.mean(dv * dv, axis=-1, keepdims=True)
    yn = dv * lax.rsqrt(var + LN_EPS) * lng_ref[...] + lnb_ref[...]
    cy = yn * jax.nn.sigmoid(yn)
    yl = (hf_ref[0] + hb_ref[0]) * glg_ref[0]
    cat = jnp.concatenate([cy, yl], axis=-1).astype(BF16)
    m = jnp.dot(cat, wo_ref[...], preferred_element_type=F32) + bo_ref[...]
    x1 = x_ref[0] + g1_ref[0] * m
    x1_ref[0] = x1
    ms = jnp.mean(x1 * x1, axis=-1, keepdims=True)
    v = x1 * lax.rsqrt(ms + RMS_EPS) * n2g_ref[...]
    v = v * (1.0 + sc2_ref[0]) + sh2_ref[0]
    v_ref[0] = v
    lg = lax.dot_general(rw_ref[...], v, (((1,), (1,)), ((), ())), precision=HIGHEST,
                         preferred_element_type=F32)
    ex = jnp.exp(lg - jnp.max(lg, axis=0, keepdims=True))
    aff_ref[0] = ex / jnp.sum(ex, axis=0, keepdims=True)


def _out_proj_route(conv_r, conv_c, ln_g, ln_b, hf, hb, glg, wo_bf16, b_out, x, g1, n2g, sh2, sc2, rw_t, tm):
    bsz, n, d = x.shape
    half = conv_r.shape[2]
    ch = hf.shape[2]
    ne = rw_t.shape[0]
    tok = lambda c: pl.BlockSpec((1, tm, c), lambda b, i: (b, i, 0))
    full = lambda shape: pl.BlockSpec(shape, lambda b, i: (0,) * len(shape))
    per_b = pl.BlockSpec((1, 1, d), lambda b, i: (b, 0, 0))
    return pl.pallas_call(
        _out_body,
        grid=(bsz, n // tm),
        in_specs=[tok(half), tok(half), full((1, 2 * half)), full((1, 2 * half)),
                  tok(ch), tok(ch), tok(ch), full((2 * half + ch, d)), full((1, d)), tok(d),
                  per_b, full((1, d)), per_b, per_b, full((ne, d))],
        out_specs=[tok(d), tok(d), pl.BlockSpec((1, ne, tm), lambda b, i: (b, 0, i))],
        out_shape=[jax.ShapeDtypeStruct((bsz, n, d), F32),
                   jax.ShapeDtypeStruct((bsz, n, d), F32),
                   jax.ShapeDtypeStruct((bsz, ne, n), F32)],
        compiler_params=_params("arbitrary", "arbitrary"),
        name="out_proj_route",
    )(conv_r, conv_c, ln_g.reshape(1, -1), ln_b.reshape(1, -1), hf, hb, glg, wo_bf16,
      b_out.reshape(1, d), x, g1, n2g.reshape(1, d), sh2, sc2, rw_t)


def _token_cumsum(m, rows_per_expert):
    er = m.shape[0]
    li = lax.broadcasted_iota(jnp.int32, (LANES, LANES), 0)
    lj = lax.broadcasted_iota(jnp.int32, (LANES, LANES), 1)
    upper = jnp.where(li <= lj, 1.0, 0.0).astype(BF16)
    cs = jnp.dot(m.astype(BF16), upper, preferred_element_type=F32)
    rt = jnp.broadcast_to(cs[:, LANES - 1:LANES], (er, LANES)).astype(BF16)
    ri = lax.broadcasted_iota(jnp.int32, (er, er), 0)
    ci = lax.broadcasted_iota(jnp.int32, (er, er), 1)
    same = (ri // rows_per_expert) == (ci // rows_per_expert)
    lower = jnp.where(same & (ci < ri), 1.0, 0.0).astype(BF16)
    rstart = jnp.dot(lower, rt, preferred_element_type=F32)
    return cs + rstart, rstart


def _topk_body(aff_ref, idx_ref, gate_ref, pos_ref, rs_ref, cpos_scr, *, cap):
    aff = aff_ref[0]
    ne, rows, _ = aff.shape

    def bisect(i, thr):
        cand = thr | jnp.left_shift(jnp.int32(1), 30 - i)
        cnt = jnp.sum((aff >= pltpu.bitcast(cand, F32)).astype(jnp.int32), axis=(1, 2), keepdims=True)
        return jnp.where(cnt >= cap, cand, thr)

    thr = lax.fori_loop(0, 31, bisect, jnp.zeros((ne, 1, 1), jnp.int32))
    gt = aff >= pltpu.bitcast(thr + 1, F32)
    eq = (aff >= pltpu.bitcast(thr, F32)) & jnp.logical_not(gt)
    need = (cap - jnp.sum(gt.astype(jnp.int32), axis=(1, 2), keepdims=True)).astype(F32)
    eqf = jnp.where(eq, 1.0, 0.0)
    rank_incl, _ = _token_cumsum(eqf.reshape(ne * rows, LANES), rows)
    rank_excl = rank_incl.reshape(ne, rows, LANES) - eqf
    sel = gt | (eq & (rank_excl < need))
    self_ = jnp.where(sel, 1.0, 0.0)
    cpos, rstart = _token_cumsum(self_.reshape(ne * rows, LANES), rows)
    cpos3 = cpos.reshape(ne, rows, LANES)
    cpos_scr[...] = cpos3
    pos_ref[0] = jnp.where(sel, cpos3 - 1.0, -1.0).astype(jnp.int32)
    rs_ref[0] = rstart.reshape(ne, rows, LANES).astype(jnp.int32)

    jrow = lax.broadcasted_iota(jnp.int32, (1, cap), 1).astype(F32)
    sub_r = lax.broadcasted_iota(jnp.int32, (rows, cap), 0).astype(F32)
    sub_l = lax.broadcasted_iota(jnp.int32, (LANES, cap), 0).astype(F32)

    def per_expert(e, c):
        cp = cpos_scr[e]
        af = aff_ref[0, e]
        rowend = cp[:, LANES - 1:LANES]
        rj = jnp.sum(jnp.where(rowend <= jrow, 1.0, 0.0), axis=0, keepdims=True)
        onehot = jnp.where(sub_r == rj, 1.0, 0.0)
        tn = (((0,), (0,)), ((), ()))
        g = lax.dot_general(cp, onehot, tn, precision=HIGHEST, preferred_element_type=F32)
        lanepos = jnp.sum(jnp.where(g <= jrow, 1.0, 0.0), axis=0, keepdims=True)
        ga = lax.dot_general(af, onehot, tn, precision=HIGHEST, preferred_element_type=F32)
        gate = jnp.sum(jnp.where(sub_l == lanepos, ga, 0.0), axis=0, keepdims=True)
        idx_ref[0, pl.ds(e, 1), :] = (rj * float(LANES) + lanepos).astype(jnp.int32)
        gate_ref[0, pl.ds(e, 1), :] = gate
        return c

    lax.fori_loop(0, ne, per_expert, 0)


def _topk_route(aff4, cap):
    bsz, ne, rows, _ = aff4.shape
    blk4 = pl.BlockSpec((1, ne, rows, LANES), lambda b: (b, 0, 0, 0))
    lst = pl.BlockSpec((1, ne, cap), lambda b: (b, 0, 0))
    return pl.pallas_call(
        functools.partial(_topk_body, cap=cap),
        grid=(bsz,),
        in_specs=[blk4],
        out_specs=[lst, lst, blk4, blk4],
        out_shape=[jax.ShapeDtypeStruct((bsz, ne, cap), jnp.int32),
                   jax.ShapeDtypeStruct((bsz, ne, cap), F32),
                   jax.ShapeDtypeStruct((bsz, ne, rows, LANES), jnp.int32),
                   jax.ShapeDtypeStruct((bsz, ne, rows, LANES), jnp.int32)],
        scratch_shapes=[pltpu.VMEM((ne, rows, LANES), F32)],
        compiler_params=_params("arbitrary"),
        name="topk_route",
    )(aff4)


def _ffn_body(idx_ref, v_hbm, gate_ref, wg_ref, wu_ref, wd_ref, ys_ref, ring, xb, acc, sems, *, mg, nf, chunk):
    e = pl.program_id(0)
    g = pl.program_id(1)
    f = pl.program_id(2)
    ngroups = pl.num_programs(1)
    grp = e * ngroups + g
    last_grp = pl.num_programs(0) * ngroups - 1
    cur = grp % 2
    first = jnp.logical_and(grp == 0, f == 0)

    def chunk_start(c):
        return pl.multiple_of(jnp.minimum(c * chunk, mg - chunk), BF16_ROWS)

    def row_copy(src_row, slot, r):
        return pltpu.make_async_copy(v_hbm.at[pl.ds(src_row, 1)], ring.at[slot, pl.ds(r, 1)], sems.at[slot])

    def land(c, slot, xslot):
        pltpu.make_async_copy(v_hbm.at[pl.ds(0, chunk)], ring.at[slot], sems.at[slot]).wait()
        xb[xslot, pl.ds(chunk_start(c), chunk), :] = ring[slot].astype(BF16)

    @pl.when(first)
    def _():
        def one_chunk(c, carry):
            base = chunk_start(c)

            def issue(r, cc):
                row_copy(idx_ref[base + r], 0, r).start()
                return cc

            lax.fori_loop(0, chunk, issue, 0, unroll=8)
            land(c, 0, 0)
            return carry

        lax.fori_loop(0, nf, one_chunk, 0)

    @pl.when(jnp.logical_not(first))
    def _():
        lc = jnp.where(f == 0, nf - 1, f - 1)
        land(lc, lc % 2, jnp.where(f == 0, cur, 1 - cur))

    @pl.when(f == 0)
    def _():
        acc[...] = jnp.zeros(acc.shape, F32)

    nxt_base = jnp.minimum(grp + 1, last_grp) * mg + chunk_start(f)
    for r in range(chunk):
        row_copy(idx_ref[nxt_base + r], f % 2, r).start(priority=r % 2)

    x = xb[cur]
    gg = jnp.dot(x, wg_ref[0].astype(BF16), preferred_element_type=F32)
    uu = jnp.dot(x, wu_ref[0].astype(BF16), preferred_element_type=F32)
    h = (gg * jax.nn.sigmoid(gg) * uu).astype(BF16)
    acc[...] += jnp.dot(h, wd_ref[0].astype(BF16), preferred_element_type=F32)

    @pl.when(f == nf - 1)
    def _():
        gate_t = gate_ref[0].T
        for i in range(mg // LANES):
            rows = slice(i * LANES, (i + 1) * LANES)
            ys_ref[0, rows, :] = (acc[rows, :] * gate_t[:, i:i + 1]).astype(BF16)

    @pl.when(jnp.logical_and(grp == last_grp, f == nf - 1))
    def _():
        pltpu.make_async_copy(v_hbm.at[pl.ds(0, chunk)], ring.at[(nf - 1) % 2], sems.at[(nf - 1) % 2]).wait()


def _expert_ffn(idx_flat, v_flat, gate_col, wg, wu, wd, mg, tf):
    ne, d, fdim = wg.shape
    rows = gate_col.shape[1] * LANES
    nf = fdim // tf
    chunk = pl.cdiv(pl.cdiv(mg, nf), BF16_ROWS) * BF16_ROWS
    return pl.pallas_call(
        functools.partial(_ffn_body, mg=mg, nf=nf, chunk=chunk),
        grid_spec=pltpu.PrefetchScalarGridSpec(
            num_scalar_prefetch=1,
            grid=(ne, rows // mg, nf),
            in_specs=[pl.BlockSpec(memory_space=pl.ANY),
                      pl.BlockSpec((1, mg // LANES, LANES), lambda e, g, f, idx: (e, g, 0)),
                      pl.BlockSpec((1, d, tf), lambda e, g, f, idx: (e, 0, f)),
                      pl.BlockSpec((1, d, tf), lambda e, g, f, idx: (e, 0, f)),
                      pl.BlockSpec((1, tf, d), lambda e, g, f, idx: (e, f, 0))],
            out_specs=pl.BlockSpec((1, mg, d), lambda e, g, f, idx: (e, g, 0)),
            scratch_shapes=[pltpu.VMEM((2, chunk, d), F32),
                            pltpu.VMEM((2, mg, d), BF16),
                            pltpu.VMEM((mg, d), F32),
                            pltpu.SemaphoreType.DMA((2,))]),
        out_shape=jax.ShapeDtypeStruct((ne, rows, d), BF16),
        compiler_params=_params("arbitrary", "arbitrary", "arbitrary"),
        name="expert_ffn",
    )(idx_flat, v_flat, gate_col, wg, wu, wd)


SLAB = 48


def _combine_body(rs_ref, pos_ref, ys_hbm, x1_ref, g5_ref, fng_ref, o_ref, stk, stk_x, sems, sem_x, *, cap, tps):
    b = pl.program_id(0)
    s = pl.program_id(1)
    nb = pl.num_programs(0)
    ns = pl.num_programs(1)
    ne = pos_ref.shape[2]
    nr = ns * tps
    step = b * ns + s
    slot = step % 2
    tile_rows = ne * SLAB

    def first_pos(bb, rr, e):
        return rs_ref[(bb * ne + e) * (nr + 1) + rr]

    def aligned(p0):
        return lax.shift_left(lax.shift_right_logical(p0, 4), 4)

    def slab_start(a0, rnd):
        return pl.multiple_of(jnp.minimum(a0 + rnd * SLAB, cap - SLAB), BF16_ROWS)

    def slab_copy(bb, e, a, dst, row, sem):
        return pltpu.make_async_copy(ys_hbm.at[e, pl.ds(bb * cap + a, SLAB)], dst.at[pl.ds(row, SLAB)], sem)

    def issue(bb, ss, sl):
        for j in range(tps):
            for e in range(ne):
                a = slab_start(aligned(first_pos(bb, ss * tps + j, e)), 0)
                slab_copy(bb, e, a, stk.at[sl], j * tile_rows + e * SLAB, sems.at[sl]).start()

    @pl.when(step == 0)
    def _():
        issue(b, s, slot)

    @pl.when(step + 1 < nb * ns)
    def _():
        nxt = step + 1
        issue(nxt // ns, nxt % ns, 1 - slot)

    kiota = lax.broadcasted_iota(jnp.int32, (SLAB, LANES), 0)
    tn = (((0,), (0,)), ((), ()))
    pltpu.make_async_copy(stk.at[slot], stk.at[slot], sems.at[slot]).wait()

    for j in range(tps):
        r = s * tps + j
        pos = pos_ref[0, j]
        a0 = [aligned(first_pos(b, r, e)) for e in range(ne)]

        def onehot(rnd, pos=pos, a0=a0):
            blocks = []
            for e in range(ne):
                rel = pos[e:e + 1, :] - slab_start(a0[e], rnd)
                fresh = (pos[e:e + 1, :] - a0[e]) >= rnd * SLAB
                blocks.append(jnp.where((kiota == rel) & fresh, 1.0, 0.0))
            return jnp.concatenate(blocks, axis=0).astype(BF16)

        moe = lax.dot_general(onehot(0), stk[slot, j * tile_rows:(j + 1) * tile_rows, :], tn,
                              preferred_element_type=F32)

        span = first_pos(b, r + 1, 0) - a0[0]
        for e in range(1, ne):
            span = jnp.maximum(span, first_pos(b, r + 1, e) - a0[e])
        rounds = jnp.maximum(1, (span + (SLAB - 1)) // SLAB)

        def extra(rnd, m, a0=a0, onehot=onehot):
            for e in range(ne):
                slab_copy(b, e, slab_start(a0[e], rnd), stk_x, e * SLAB, sem_x).start()
            pltpu.make_async_copy(stk_x, stk_x, sem_x).wait()
            return m + lax.dot_general(onehot(rnd), stk_x[...], tn, preferred_element_type=F32)

        moe = lax.fori_loop(1, rounds, extra, moe)

        y = x1_ref[0, j * LANES:(j + 1) * LANES, :] + g5_ref[0] * moe
        ms = jnp.mean(y * y, axis=-1, keepdims=True)
        o_ref[0, j * LANES:(j + 1) * LANES, :] = y * lax.rsqrt(ms + RMS_EPS) * fng_ref[...]


def _combine_norm(rs_ext, pos_r, ys, x1, g5, fng, cap, tps):
    bsz, n, d = x1.shape
    ns = n // (LANES * tps)
    ne = pos_r.shape[2]
    return pl.pallas_call(
        functools.partial(_combine_body, cap=cap, tps=tps),
        grid_spec=pltpu.PrefetchScalarGridSpec(
            num_scalar_prefetch=1,
            grid=(bsz, ns),
            in_specs=[pl.BlockSpec((1, tps, ne, LANES), lambda b, s, rs: (b, s, 0, 0)),
                      pl.BlockSpec(memory_space=pl.ANY),
                      pl.BlockSpec((1, tps * LANES, d), lambda b, s, rs: (b, s, 0)),
                      pl.BlockSpec((1, 1, d), lambda b, s, rs: (b, 0, 0)),
                      pl.BlockSpec((1, d), lambda b, s, rs: (0, 0))],
            out_specs=pl.BlockSpec((1, tps * LANES, d), lambda b, s, rs: (b, s, 0)),
            scratch_shapes=[pltpu.VMEM((2, tps * ne * SLAB, d), BF16),
                            pltpu.VMEM((ne * SLAB, d), BF16),
                            pltpu.SemaphoreType.DMA((2,)),
                            pltpu.SemaphoreType.DMA(())]),
        out_shape=jax.ShapeDtypeStruct((bsz, n, d), F32),
        compiler_params=_params("arbitrary", "arbitrary"),
        name="combine_norm",
    )(rs_ext, pos_r, ys, x1, g5, fng.reshape(1, d))


def _block_diag(w):
    heads, hd, _ = w.shape
    eye = jnp.eye(heads, dtype=w.dtype)
    return (eye[:, None, :, None] * w[:, :, None, :]).reshape(heads * hd, heads * hd)


def _tile(n, pref):
    return pref if n % pref == 0 else n


def kernel(x, c, ctx, c_ctx, norm1_g, norm2_g, ada_w, ada_b, w_in, b_in, conv_dw_w, conv_dw_b, conv_ln_g, conv_ln_b, lru_conv_w, lru_conv_b, lru_wa, lru_ba, lru_wi, lru_bi, lru_lambda, w_out, b_out, router_w, exp_w_gate, exp_w_up, exp_w_down, final_norm_g):
    assert norm1_g.shape[0] == 1
    mod, x1, v, aff = _mixer(x, c, ctx, c_ctx, norm1_g[0], norm2_g[0], ada_w[0], ada_b[0], w_in[0], b_in[0],
                             conv_dw_w[0], conv_dw_b[0], conv_ln_g[0], conv_ln_b[0], lru_conv_w[0],
                             lru_conv_b[0], lru_wa[0], lru_ba[0], lru_wi[0], lru_bi[0], lru_lambda[0],
                             w_out[0], b_out[0], router_w[0])
    return _moe_norm(x1, v, aff, mod[5], exp_w_gate[0], exp_w_up[0], exp_w_down[0], final_norm_g)


def _mixer(x, c, ctx, c_ctx, norm1_g, norm2_g, ada_w, ada_b, w_in, b_in, conv_w, conv_b, ln_g, ln_b,
           lru_cw, lru_cb, lru_wa, lru_ba, lru_wi, lru_bi, lru_lam, w_out, b_out, router_w):
    bsz, n, d = x.shape
    cond8 = jnp.zeros((SUBLANES, d), F32).at[:bsz].set(c).at[bsz].set(c_ctx)
    mods = _ada_mod(cond8, ada_w, ada_b)
    mod = [mods[:bsz, k * d:(k + 1) * d].reshape(bsz, 1, d) for k in range(N_MOD)]
    mod_c = [jnp.broadcast_to(mods[bsz:bsz + 1, k * d:(k + 1) * d].reshape(1, 1, d), (bsz, 1, d))
             for k in range(2)]
    w_in_b = w_in.astype(BF16)
    w_out_b = w_out.astype(BF16)
    ch = lru_cb.shape[1]
    wgate = [jnp.concatenate([_block_diag(lru_wa[dd]), _block_diag(lru_wi[dd])], axis=1).astype(BF16)
             for dd in range(2)]
    bgate = [jnp.concatenate([lru_ba[dd], lru_bi[dd]]) for dd in range(2)]

    def lru(lx, h0, dd, reverse):
        return _lru_dir(lx, h0, lru_cw[dd], lru_cb[dd], wgate[dd], bgate[dd], lru_lam[dd], reverse,
                        _tile(lx.shape[1], 256))

    _, c_lx, _ = _in_proj(ctx, norm1_g, mod_c[0], mod_c[1], w_in_b, b_in, _tile(ctx.shape[1], 256))
    zero_h = jnp.zeros((bsz, 1, ch), F32)
    _, hf0 = lru(c_lx, zero_h, 0, False)
    _, hb0 = lru(c_lx, zero_h, 1, True)

    x_glu, x_lx, x_glg = _in_proj(x, norm1_g, mod[0], mod[1], w_in_b, b_in, _tile(n, 512))
    hf, _ = lru(x_lx, hf0, 0, False)
    hb, _ = lru(x_lx, hb0, 1, True)
    conv_r, conv_c = _conv_grid(x_glu, conv_w, conv_b)
    x1, v, aff = _out_proj_route(conv_r, conv_c, ln_g, ln_b, hf, hb, x_glg, w_out_b, b_out, x, mod[2],
                                 norm2_g, mod[3], mod[4], router_w.T, _tile(n, 512))
    return mod, x1, v, aff


def _moe_norm(x1, v, aff, gate2, wg, wu, wd, final_norm_g):
    bsz, n, d = x1.shape
    ne = aff.shape[1]
    cap = EC_CAPACITY * n // ne
    idx, gate, pos, rstart = _topk_route(aff.reshape(bsz, ne, n // LANES, LANES), cap)
    tok_base = (jnp.arange(bsz, dtype=jnp.int32) * n)[:, None, None]
    idx_flat = jnp.swapaxes(idx + tok_base, 0, 1).reshape(ne * bsz * cap)
    gate_rows = jnp.swapaxes(gate, 0, 1).reshape(ne, bsz * cap // LANES, LANES)
    ys = _expert_ffn(idx_flat, v.reshape(bsz * n, d), gate_rows, wg, wu, wd, 2 * cap, 256)
    rs_ext = jnp.concatenate([rstart[..., 0], jnp.full((bsz, ne, 1), cap, jnp.int32)], axis=-1)
    return _combine_norm(rs_ext.reshape(-1), jnp.swapaxes(pos, 1, 2), ys, x1, gate2, final_norm_g, cap, 4)
```

```python
import functools

import jax
import jax.numpy as jnp
from jax import lax
from jax.experimental import pallas as pl
from jax.experimental.pallas import tpu as pltpu

GRID_W = 64
CONV_WIDTH = 31
CONV_PAD = (CONV_WIDTH - 1) // 2
LRU_CONV_WIDTH = 4
LRU_C = 8.0
N_EXPERTS = 16
EC_CAPACITY = 2
N_MOD = 6
RMS_EPS = 1e-6
LN_EPS = 1e-5

LANES = 128
SUBLANES = 8
BF16_ROWS = 16
VMEM_LIMIT = 56 * 1024 * 1024

F32 = jnp.float32
BF16 = jnp.bfloat16
HIGHEST = lax.Precision.HIGHEST


def _params(*sem):
    return pltpu.CompilerParams(dimension_semantics=sem, vmem_limit_bytes=VMEM_LIMIT)


def _ada_body(c_ref, w_ref, b_ref, o_ref):
    s = c_ref[...]
    s = s * jax.nn.sigmoid(s)
    o_ref[...] = jnp.dot(s, w_ref[...], precision=HIGHEST, preferred_element_type=F32) + b_ref[...]


def _ada_mod(cond8, ada_w, ada_b):
    d, n = ada_w.shape
    tn = n // 4
    return pl.pallas_call(
        _ada_body,
        grid=(n // tn,),
        in_specs=[pl.BlockSpec((SUBLANES, d), lambda j: (0, 0)),
                  pl.BlockSpec((d, tn), lambda j: (0, j)),
                  pl.BlockSpec((1, tn), lambda j: (0, j))],
        out_specs=pl.BlockSpec((SUBLANES, tn), lambda j: (0, j)),
        out_shape=jax.ShapeDtypeStruct((SUBLANES, n), F32),
        compiler_params=_params("arbitrary"),
        name="ada_mod",
    )(cond8, ada_w, ada_b.reshape(1, n))


def _inproj_body(x_ref, g_ref, sh_ref, sc_ref, w_ref, b_ref, glu_ref, lx_ref, glg_ref):
    x = x_ref[0]
    ms = jnp.mean(x * x, axis=-1, keepdims=True)
    y = x * lax.rsqrt(ms + RMS_EPS) * g_ref[...]
    u = y * (1.0 + sc_ref[0]) + sh_ref[0]
    p = jnp.dot(u.astype(BF16), w_ref[...], preferred_element_type=F32) + b_ref[...]
    cc = p.shape[1] // 4
    glu_ref[0] = p[:, :cc] * jax.nn.sigmoid(p[:, cc:2 * cc])
    lx_ref[0] = p[:, 2 * cc:3 * cc]
    glg_ref[0] = jax.nn.gelu(p[:, 3 * cc:])


def _in_proj(x, norm_g, shift, scale, w_bf16, b_in, tm):
    bsz, n, d = x.shape
    n4 = w_bf16.shape[1]
    cc = n4 // 4
    tok = pl.BlockSpec((1, tm, cc), lambda b, i: (b, i, 0))
    return pl.pallas_call(
        _inproj_body,
        grid=(bsz, n // tm),
        in_specs=[pl.BlockSpec((1, tm, d), lambda b, i: (b, i, 0)),
                  pl.BlockSpec((1, d), lambda b, i: (0, 0)),
                  pl.BlockSpec((1, 1, d), lambda b, i: (b, 0, 0)),
                  pl.BlockSpec((1, 1, d), lambda b, i: (b, 0, 0)),
                  pl.BlockSpec((d, n4), lambda b, i: (0, 0)),
                  pl.BlockSpec((1, n4), lambda b, i: (0, 0))],
        out_specs=[tok, tok, tok],
        out_shape=[jax.ShapeDtypeStruct((bsz, n, cc), F32)] * 3,
        compiler_params=_params("arbitrary", "arbitrary"),
        name="in_proj",
    )(x, norm_g.reshape(1, d), shift, scale, w_bf16, b_in.reshape(1, n4))


def _lru_body(x_ref, h0_ref, cw_ref, cb_ref, wg_ref, bg_ref, lam_ref, h_ref, hl_ref,
              ext_ref, a_ref, u_ref, carry_ref, halo_ref, *, reverse, tl):
    i = pl.program_id(1)
    ch = x_ref.shape[2]

    @pl.when(i == 0)
    def _():
        carry_ref[...] = jnp.broadcast_to(h0_ref[0], carry_ref.shape)
        halo_ref[...] = jnp.zeros(halo_ref.shape, F32)

    x = x_ref[0]
    if not reverse:
        ext_ref[0:SUBLANES, :] = halo_ref[...]
        ext_ref[SUBLANES:SUBLANES + tl, :] = x
        taps = [ext_ref[SUBLANES - (LRU_CONV_WIDTH - 1) + j:SUBLANES - (LRU_CONV_WIDTH - 1) + j + tl, :]
                for j in range(LRU_CONV_WIDTH)]
        halo_ref[...] = x[tl - SUBLANES:tl, :]
    else:
        ext_ref[0:tl, :] = x
        ext_ref[tl:tl + SUBLANES, :] = halo_ref[...]
        taps = [ext_ref[(LRU_CONV_WIDTH - 1) - j:(LRU_CONV_WIDTH - 1) - j + tl, :]
                for j in range(LRU_CONV_WIDTH)]
        halo_ref[...] = x[0:SUBLANES, :]
    xc = cb_ref[...]
    for j in range(LRU_CONV_WIDTH):
        xc = xc + cw_ref[j:j + 1, :] * taps[j]

    z = jnp.dot(xc.astype(BF16), wg_ref[...], preferred_element_type=F32) + bg_ref[...]
    r = jax.nn.sigmoid(z[:, :ch])
    ig = jax.nn.sigmoid(z[:, ch:])
    log_a = (-LRU_C) * r * jax.nn.softplus(-lam_ref[...])
    th = jnp.tanh(log_a)
    a_ref[...] = jnp.exp(log_a)
    m2 = -2.0 * th / (1.0 - th)
    u_ref[...] = jnp.where(m2 > 0.0, m2 * lax.rsqrt(m2), 0.0) * (ig * xc)

    ng = tl // SUBLANES
    rowid = lax.broadcasted_iota(jnp.int32, (SUBLANES, ch), 0)

    def group(gi, h):
        g = (ng - 1 - gi) if reverse else gi
        off = pl.multiple_of(g * SUBLANES, SUBLANES)
        a = a_ref[pl.ds(off, SUBLANES), :]
        u = u_ref[pl.ds(off, SUBLANES), :]
        for s in (1, 2, 4):
            if reverse:
                m = rowid < SUBLANES - s
                sh = SUBLANES - s
            else:
                m = rowid >= s
                sh = s
            ap = jnp.where(m, pltpu.roll(a, sh, 0), 1.0)
            up = jnp.where(m, pltpu.roll(u, sh, 0), 0.0)
            u = a * up + u
            a = a * ap
        hh = a * h + u
        h_ref[0, pl.ds(off, SUBLANES), :] = hh
        edge = hh[0:1, :] if reverse else hh[SUBLANES - 1:SUBLANES, :]
        return jnp.broadcast_to(edge, (SUBLANES, ch))

    h = lax.fori_loop(0, ng, group, carry_ref[...], unroll=8)
    carry_ref[...] = h
    hl_ref[0] = h[0:1, :]


def _lru_dir(lx, h0, cw, cb, wgate, bgate, lam, reverse, tl):
    bsz, n, ch = lx.shape
    nc = n // tl
    cmap = (lambda b, i: (b, nc - 1 - i, 0)) if reverse else (lambda b, i: (b, i, 0))
    full = lambda shape: pl.BlockSpec(shape, lambda b, i: (0,) * len(shape))
    return pl.pallas_call(
        functools.partial(_lru_body, reverse=reverse, tl=tl),
        grid=(bsz, nc),
        in_specs=[pl.BlockSpec((1, tl, ch), cmap),
                  pl.BlockSpec((1, 1, ch), lambda b, i: (b, 0, 0)),
                  full((LRU_CONV_WIDTH, ch)), full((1, ch)),
                  full((ch, 2 * ch)), full((1, 2 * ch)), full((1, ch))],
        out_specs=[pl.BlockSpec((1, tl, ch), cmap),
                   pl.BlockSpec((1, 1, ch), lambda b, i: (b, 0, 0))],
        out_shape=[jax.ShapeDtypeStruct((bsz, n, ch), F32),
                   jax.ShapeDtypeStruct((bsz, 1, ch), F32)],
        scratch_shapes=[pltpu.VMEM((tl + SUBLANES, ch), F32),
                        pltpu.VMEM((tl, ch), F32),
                        pltpu.VMEM((tl, ch), F32),
                        pltpu.VMEM((SUBLANES, ch), F32),
                        pltpu.VMEM((SUBLANES, ch), F32)],
        compiler_params=_params("arbitrary", "arbitrary"),
        name="lru_rev" if reverse else "lru_fwd",
    )(lx, h0, cw, cb.reshape(1, ch), wgate, bgate.reshape(1, 2 * ch), lam.reshape(1, ch))


ROW_STRIDE = GRID_W + 2 * BF16_ROWS


def _conv_row_body(x_ref, w_ref, b_ref, o_ref, pad_ref):
    n = x_ref.shape[1]
    nrows = n // GRID_W
    gap = jnp.zeros((BF16_ROWS, LANES), F32)

    def fill(r, c):
        base = pl.multiple_of(r * ROW_STRIDE, SUBLANES)
        src = pl.multiple_of(r * GRID_W, SUBLANES)
        pad_ref[pl.ds(base, BF16_ROWS), :] = gap
        pad_ref[pl.ds(base + BF16_ROWS, GRID_W), :] = x_ref[0, pl.ds(src, GRID_W), :]
        pad_ref[pl.ds(base + BF16_ROWS + GRID_W, BF16_ROWS), :] = gap
        return c

    lax.fori_loop(0, nrows, fill, 0)

    def row(r, c):
        base = pl.multiple_of(r * ROW_STRIDE, SUBLANES)
        acc = jnp.broadcast_to(b_ref[...], (GRID_W, LANES))
        for k in range(CONV_WIDTH):
            acc = acc + w_ref[k:k + 1, :] * pad_ref[pl.ds(base + BF16_ROWS - CONV_PAD + k, GRID_W), :]
        o_ref[0, pl.ds(pl.multiple_of(r * GRID_W, SUBLANES), GRID_W), :] = acc
        return c

    lax.fori_loop(0, nrows, row, 0, unroll=2)


def _conv_col_body(x_ref, w_ref, b_ref, o_ref, pad_ref):
    n = x_ref.shape[1]
    nrows = n // GRID_W
    halo = CONV_PAD * GRID_W
    pad_ref[0:halo, :] = jnp.zeros((halo, LANES), F32)
    pad_ref[halo + n:halo + n + halo, :] = jnp.zeros((halo, LANES), F32)

    def fill(r, c):
        src = pl.multiple_of(r * GRID_W, SUBLANES)
        pad_ref[pl.ds(halo + src, GRID_W), :] = x_ref[0, pl.ds(src, GRID_W), :]
        return c

    lax.fori_loop(0, nrows, fill, 0)

    def row(r, c):
        base = pl.multiple_of(r * GRID_W, SUBLANES)
        acc = jnp.broadcast_to(b_ref[...], (GRID_W, LANES))
        for k in range(CONV_WIDTH):
            acc = acc + w_ref[k:k + 1, :] * pad_ref[pl.ds(base + k * GRID_W, GRID_W), :]
        o_ref[0, pl.ds(base, GRID_W), :] = acc
        return c

    lax.fori_loop(0, nrows, row, 0, unroll=2)


def _conv_grid(glu, w, b):
    bsz, n, ch = glu.shape
    half = ch // 2
    ng = half // LANES
    outs = []
    for body, first, pad_rows in ((_conv_row_body, 0, (n // GRID_W) * ROW_STRIDE),
                                  (_conv_col_body, ng, n + 2 * CONV_PAD * GRID_W)):
        outs.append(pl.pallas_call(
            body,
            grid=(bsz, ng),
            in_specs=[pl.BlockSpec((1, n, LANES), lambda bb, g, first=first: (bb, 0, g + first)),
                      pl.BlockSpec((CONV_WIDTH, LANES), lambda bb, g, first=first: (0, g + first)),
                      pl.BlockSpec((1, LANES), lambda bb, g, first=first: (0, g + first))],
            out_specs=pl.BlockSpec((1, n, LANES), lambda bb, g: (bb, 0, g)),
            out_shape=jax.ShapeDtypeStruct((bsz, n, half), F32),
            scratch_shapes=[pltpu.VMEM((pad_rows, LANES), F32)],
            compiler_params=_params("arbitrary", "arbitrary"),
            name="conv_row" if first == 0 else "conv_col",
        )(glu, w, b.reshape(1, ch)))
    return outs


def _out_body(cr_ref, cc_ref, lng_ref, lnb_ref, hf_ref, hb_ref, glg_ref, wo_ref, bo_ref, x_ref,
              g1_ref, n2g_ref, sh2_ref, sc2_ref, rw_ref, x1_ref, v_ref, aff_ref):
    cv = jnp.concatenate([cr_ref[0], cc_ref[0]], axis=-1)
    mu = jnp.mean(cv, axis=-1, keepdims=True)
    dv = cv - mu
    var = jnp.mean(dv * dv, axis=-1, keepdims=True)
    yn = dv * lax.rsqrt(var + LN_EPS) * lng_ref[...] + lnb_ref[...]
    cy = yn * jax.nn.sigmoid(yn)
    yl = (hf_ref[0] + hb_ref[0]) * glg_ref[0]
    cat = jnp.concatenate([cy, yl], axis=-1).astype(BF16)
    m = jnp.dot(cat, wo_ref[...], preferred_element_type=F32) + bo_ref[...]
    x1 = x_ref[0] + g1_ref[0] * m
    x1_ref[0] = x1
    ms = jnp.mean(x1 * x1, axis=-1, keepdims=True)
    v = x1 * lax.rsqrt(ms + RMS_EPS) * n2g_ref[...]
    v = v * (1.0 + sc2_ref[0]) + sh2_ref[0]
    v_ref[0] = v
    v_hi = v.astype(BF16)
    v_lo = (v - v_hi.astype(F32)).astype(BF16)
    rw = rw_ref[...]
    rw_hi = rw.astype(BF16)
    rw_lo = (rw - rw_hi.astype(F32)).astype(BF16)
    nt = (((1,), (1,)), ((), ()))
    ne = rw.shape[0]
    both = lax.dot_general(jnp.concatenate([rw_hi, rw_lo], axis=0), v_hi, nt, preferred_element_type=F32)
    lg = both[:ne] + both[ne:] + lax.dot_general(rw_hi, v_lo, nt, preferred_element_type=F32)
    ex = jnp.exp(lg - jnp.max(lg, axis=0, keepdims=True))
    aff_ref[0] = ex / jnp.sum(ex, axis=0, keepdims=True)


def _out_proj_route(conv_r, conv_c, ln_g, ln_b, hf, hb, glg, wo_bf16, b_out, x, g1, n2g, sh2, sc2, rw_t, tm):
    bsz, n, d = x.shape
    half = conv_r.shape[2]
    ch = hf.shape[2]
    ne = rw_t.shape[0]
    tok = lambda c: pl.BlockSpec((1, tm, c), lambda b, i: (b, i, 0))
    full = lambda shape: pl.BlockSpec(shape, lambda b, i: (0,) * len(shape))
    per_b = pl.BlockSpec((1, 1, d), lambda b, i: (b, 0, 0))
    return pl.pallas_call(
        _out_body,
        grid=(bsz, n // tm),
        in_specs=[tok(half), tok(half), full((1, 2 * half)), full((1, 2 * half)),
                  tok(ch), tok(ch), tok(ch), full((2 * half + ch, d)), full((1, d)), tok(d),
                  per_b, full((1, d)), per_b, per_b, full((ne, d))],
        out_specs=[tok(d), tok(d), pl.BlockSpec((1, ne, tm), lambda b, i: (b, 0, i))],
        out_shape=[jax.ShapeDtypeStruct((bsz, n, d), F32),
                   jax.ShapeDtypeStruct((bsz, n, d), F32),
                   jax.ShapeDtypeStruct((bsz, ne, n), F32)],
        compiler_params=_params("arbitrary", "arbitrary"),
        name="out_proj_route",
    )(conv_r, conv_c, ln_g.reshape(1, -1), ln_b.reshape(1, -1), hf, hb, glg, wo_bf16,
      b_out.reshape(1, d), x, g1, n2g.reshape(1, d), sh2, sc2, rw_t)


def _token_cumsum(m, rows_per_expert):
    er = m.shape[0]
    li = lax.broadcasted_iota(jnp.int32, (LANES, LANES), 0)
    lj = lax.broadcasted_iota(jnp.int32, (LANES, LANES), 1)
    upper = jnp.where(li <= lj, 1.0, 0.0).astype(BF16)
    cs = jnp.dot(m.astype(BF16), upper, preferred_element_type=F32)
    rt = jnp.broadcast_to(cs[:, LANES - 1:LANES], (er, LANES)).astype(BF16)
    ri = lax.broadcasted_iota(jnp.int32, (er, er), 0)
    ci = lax.broadcasted_iota(jnp.int32, (er, er), 1)
    same = (ri // rows_per_expert) == (ci // rows_per_expert)
    lower = jnp.where(same & (ci < ri), 1.0, 0.0).astype(BF16)
    rstart = jnp.dot(lower, rt, preferred_element_type=F32)
    return cs, rstart


def _topk_body(aff_ref, idx_ref, gate_ref, pos_ref, rs_ref, cs_scr, rs_scr, *, cap):
    aff = aff_ref[0]
    ne, rows, _ = aff.shape

    def bisect(i, thr):
        cand = thr | jnp.left_shift(jnp.int32(1), 30 - i)
        cnt = jnp.sum((aff >= pltpu.bitcast(cand, F32)).astype(jnp.int32), axis=(1, 2), keepdims=True)
        return jnp.where(cnt >= cap, cand, thr)

    thr = lax.fori_loop(0, 31, bisect, jnp.zeros((ne, 1, 1), jnp.int32))
    gt = aff >= pltpu.bitcast(thr + 1, F32)
    eq = (aff >= pltpu.bitcast(thr, F32)) & jnp.logical_not(gt)
    need = (cap - jnp.sum(gt.astype(jnp.int32), axis=(1, 2), keepdims=True)).astype(F32)
    eqf = jnp.where(eq, 1.0, 0.0)
    rank_in_row, rank_row0 = _token_cumsum(eqf.reshape(ne * rows, LANES), rows)
    rank_excl = (rank_in_row + rank_row0).reshape(ne, rows, LANES) - eqf
    sel = gt | (eq & (rank_excl < need))
    self_ = jnp.where(sel, 1.0, 0.0)
    cs, rstart = _token_cumsum(self_.reshape(ne * rows, LANES), rows)
    cs3 = cs.reshape(ne, rows, LANES)
    rstart3 = rstart.reshape(ne, rows, LANES)
    cs_scr[...] = cs3
    rs_scr[...] = rstart3
    pos_ref[0] = jnp.where(sel, cs3 + rstart3 - 1.0, -1.0).astype(jnp.int32)
    rs_ref[0] = rstart3.astype(jnp.int32)

    jrow = lax.broadcasted_iota(jnp.int32, (1, cap), 1).astype(F32)
    sub_r = lax.broadcasted_iota(jnp.int32, (rows, cap), 0).astype(F32)
    sub_l = lax.broadcasted_iota(jnp.int32, (LANES, cap), 0).astype(F32)

    def per_expert(e, c):
        cl = cs_scr[e]
        af = aff_ref[0, e]
        rowtot = cl[:, LANES - 1:LANES]
        before = rowtot + rs_scr[e][:, 0:1] <= jrow
        rj = jnp.sum(jnp.where(before, 1.0, 0.0), axis=0, keepdims=True)
        rowbase = jnp.sum(jnp.where(before, rowtot, 0.0), axis=0, keepdims=True)
        onehot = jnp.where(sub_r == rj, 1.0, 0.0).astype(BF16)
        a1 = af.astype(BF16)
        r1 = af - a1.astype(F32)
        a2 = r1.astype(BF16)
        a3 = (r1 - a2.astype(F32)).astype(BF16)
        lhs = jnp.concatenate([cl.astype(BF16), a1, a2, a3], axis=1)
        gathered = lax.dot_general(lhs, onehot, (((0,), (0,)), ((), ())), preferred_element_type=F32)
        g = gathered[:LANES]
        ga = gathered[LANES:2 * LANES] + gathered[2 * LANES:3 * LANES] + gathered[3 * LANES:]
        lanepos = jnp.sum(jnp.where(g <= jrow - rowbase, 1.0, 0.0), axis=0, keepdims=True)
        gate = jnp.sum(jnp.where(sub_l == lanepos, ga, 0.0), axis=0, keepdims=True)
        idx_ref[0, pl.ds(e, 1), :] = (rj * float(LANES) + lanepos).astype(jnp.int32)
        gate_ref[0, pl.ds(e, 1), :] = gate
        return c

    lax.fori_loop(0, ne, per_expert, 0)


def _topk_route(aff4, cap):
    bsz, ne, rows, _ = aff4.shape
    blk4 = pl.BlockSpec((1, ne, rows, LANES), lambda b: (b, 0, 0, 0))
    lst = pl.BlockSpec((1, ne, cap), lambda b: (b, 0, 0))
    return pl.pallas_call(
        functools.partial(_topk_body, cap=cap),
        grid=(bsz,),
        in_specs=[blk4],
        out_specs=[lst, lst, blk4, blk4],
        out_shape=[jax.ShapeDtypeStruct((bsz, ne, cap), jnp.int32),
                   jax.ShapeDtypeStruct((bsz, ne, cap), F32),
                   jax.ShapeDtypeStruct((bsz, ne, rows, LANES), jnp.int32),
                   jax.ShapeDtypeStruct((bsz, ne, rows, LANES), jnp.int32)],
        scratch_shapes=[pltpu.VMEM((ne, rows, LANES), F32), pltpu.VMEM((ne, rows, LANES), F32)],
        compiler_params=_params("arbitrary"),
        name="topk_route",
    )(aff4)


def _ffn_body(idx_ref, v_hbm, gate_ref, wg_ref, wu_ref, wd_ref, ys_ref, ring, xb, acc, sems, *, mg, nf, chunk):
    e = pl.program_id(0)
    g = pl.program_id(1)
    f = pl.program_id(2)
    ngroups = pl.num_programs(1)
    grp = e * ngroups + g
    last_grp = pl.num_programs(0) * ngroups - 1
    cur = grp % 2
    first = jnp.logical_and(grp == 0, f == 0)

    def chunk_start(c):
        return pl.multiple_of(jnp.minimum(c * chunk, mg - chunk), BF16_ROWS)

    def row_copy(src_row, slot, r):
        return pltpu.make_async_copy(v_hbm.at[pl.ds(src_row, 1)], ring.at[slot, pl.ds(r, 1)], sems.at[slot])

    def land(c, slot, xslot):
        pltpu.make_async_copy(v_hbm.at[pl.ds(0, chunk)], ring.at[slot], sems.at[slot]).wait()
        xb[xslot, pl.ds(chunk_start(c), chunk), :] = ring[slot].astype(BF16)

    @pl.when(first)
    def _():
        def one_chunk(c, carry):
            base = chunk_start(c)

            def issue(r, cc):
                row_copy(idx_ref[base + r], 0, r).start()
                return cc

            lax.fori_loop(0, chunk, issue, 0, unroll=8)
            land(c, 0, 0)
            return carry

        lax.fori_loop(0, nf, one_chunk, 0)

    @pl.when(jnp.logical_not(first))
    def _():
        lc = jnp.where(f == 0, nf - 1, f - 1)
        land(lc, lc % 2, jnp.where(f == 0, cur, 1 - cur))

    @pl.when(f == 0)
    def _():
        acc[...] = jnp.zeros(acc.shape, F32)

    nxt_base = jnp.minimum(grp + 1, last_grp) * mg + chunk_start(f)
    for r in range(chunk):
        row_copy(idx_ref[nxt_base + r], f % 2, r).start(priority=r % 2)

    x = xb[cur]
    gg = jnp.dot(x, wg_ref[0].astype(BF16), preferred_element_type=F32)
    uu = jnp.dot(x, wu_ref[0].astype(BF16), preferred_element_type=F32)
    h = (gg * jax.nn.sigmoid(gg) * uu).astype(BF16)
    acc[...] += jnp.dot(h, wd_ref[0].astype(BF16), preferred_element_type=F32)

    @pl.when(f == nf - 1)
    def _():
        gate_t = gate_ref[0].T
        for i in range(mg // LANES):
            rows = slice(i * LANES, (i + 1) * LANES)
            ys_ref[0, rows, :] = (acc[rows, :] * gate_t[:, i:i + 1]).astype(BF16)

    @pl.when(jnp.logical_and(grp == last_grp, f == nf - 1))
    def _():
        pltpu.make_async_copy(v_hbm.at[pl.ds(0, chunk)], ring.at[(nf - 1) % 2], sems.at[(nf - 1) % 2]).wait()


def _expert_ffn(idx_flat, v_flat, gate_col, wg, wu, wd, mg, tf):
    ne, d, fdim = wg.shape
    rows = gate_col.shape[1] * LANES
    nf = fdim // tf
    chunk = pl.cdiv(pl.cdiv(mg, nf), BF16_ROWS) * BF16_ROWS
    return pl.pallas_call(
        functools.partial(_ffn_body, mg=mg, nf=nf, chunk=chunk),
        grid_spec=pltpu.PrefetchScalarGridSpec(
            num_scalar_prefetch=1,
            grid=(ne, rows // mg, nf),
            in_specs=[pl.BlockSpec(memory_space=pl.ANY),
                      pl.BlockSpec((1, mg // LANES, LANES), lambda e, g, f, idx: (e, g, 0)),
                      pl.BlockSpec((1, d, tf), lambda e, g, f, idx: (e, 0, f)),
                      pl.BlockSpec((1, d, tf), lambda e, g, f, idx: (e, 0, f)),
                      pl.BlockSpec((1, tf, d), lambda e, g, f, idx: (e, f, 0))],
            out_specs=pl.BlockSpec((1, mg, d), lambda e, g, f, idx: (e, g, 0)),
            scratch_shapes=[pltpu.VMEM((2, chunk, d), F32),
                            pltpu.VMEM((2, mg, d), BF16),
                            pltpu.VMEM((mg, d), F32),
                            pltpu.SemaphoreType.DMA((2,))]),
        out_shape=jax.ShapeDtypeStruct((ne, rows, d), BF16),
        compiler_params=_params("arbitrary", "arbitrary", "arbitrary"),
        name="expert_ffn",
    )(idx_flat, v_flat, gate_col, wg, wu, wd)


SLAB = 48


def _combine_body(rs_ref, pos_ref, ys_hbm, x1_ref, g5_ref, fng_ref, o_ref, stk, stk_x, sems, sem_x, *, cap, tps):
    b = pl.program_id(0)
    s = pl.program_id(1)
    nb = pl.num_programs(0)
    ns = pl.num_programs(1)
    ne = pos_ref.shape[2]
    nr = ns * tps
    step = b * ns + s
    slot = step % 2
    tile_rows = ne * SLAB

    def first_pos(bb, rr, e):
        return rs_ref[(bb * ne + e) * (nr + 1) + rr]

    def aligned(p0):
        return lax.shift_left(lax.shift_right_logical(p0, 4), 4)

    def slab_start(a0, rnd):
        return pl.multiple_of(jnp.minimum(a0 + rnd * SLAB, cap - SLAB), BF16_ROWS)

    def slab_copy(bb, e, a, dst, row, sem):
        return pltpu.make_async_copy(ys_hbm.at[e, pl.ds(bb * cap + a, SLAB)], dst.at[pl.ds(row, SLAB)], sem)

    def issue(bb, ss, sl):
        for j in range(tps):
            for e in range(ne):
                a = slab_start(aligned(first_pos(bb, ss * tps + j, e)), 0)
                slab_copy(bb, e, a, stk.at[sl], j * tile_rows + e * SLAB, sems.at[sl]).start()

    @pl.when(step == 0)
    def _():
        issue(b, s, slot)

    @pl.when(step + 1 < nb * ns)
    def _():
        nxt = step + 1
        issue(nxt // ns, nxt % ns, 1 - slot)

    kiota = lax.broadcasted_iota(jnp.int32, (SLAB, LANES), 0)
    tn = (((0,), (0,)), ((), ()))
    pltpu.make_async_copy(stk.at[slot], stk.at[slot], sems.at[slot]).wait()

    for j in range(tps):
        r = s * tps + j
        pos = pos_ref[0, j]
        a0 = [aligned(first_pos(b, r, e)) for e in range(ne)]

        def onehot(rnd, pos=pos, a0=a0):
            blocks = []
            for e in range(ne):
                rel = pos[e:e + 1, :] - slab_start(a0[e], rnd)
                fresh = (pos[e:e + 1, :] - a0[e]) >= rnd * SLAB
                blocks.append(jnp.where((kiota == rel) & fresh, 1.0, 0.0))
            return jnp.concatenate(blocks, axis=0).astype(BF16)

        moe = lax.dot_general(onehot(0), stk[slot, j * tile_rows:(j + 1) * tile_rows, :], tn,
                              preferred_element_type=F32)

        span = first_pos(b, r + 1, 0) - a0[0]
        for e in range(1, ne):
            span = jnp.maximum(span, first_pos(b, r + 1, e) - a0[e])
        rounds = jnp.maximum(1, (span + (SLAB - 1)) // SLAB)

        def extra(rnd, m, a0=a0, onehot=onehot):
            for e in range(ne):
                slab_copy(b, e, slab_start(a0[e], rnd), stk_x, e * SLAB, sem_x).start()
            pltpu.make_async_copy(stk_x, stk_x, sem_x).wait()
            return m + lax.dot_general(onehot(rnd), stk_x[...], tn, preferred_element_type=F32)

        moe = lax.fori_loop(1, rounds, extra, moe)

        y = x1_ref[0, j * LANES:(j + 1) * LANES, :] + g5_ref[0] * moe
        ms = jnp.mean(y * y, axis=-1, keepdims=True)
        o_ref[0, j * LANES:(j + 1) * LANES, :] = y * lax.rsqrt(ms + RMS_EPS) * fng_ref[...]


def _combine_norm(rs_ext, pos_r, ys, x1, g5, fng, cap, tps):
    bsz, n, d = x1.shape
    ns = n // (LANES * tps)
    ne = pos_r.shape[2]
    return pl.pallas_call(
        functools.partial(_combine_body, cap=cap, tps=tps),
        grid_spec=pltpu.PrefetchScalarGridSpec(
            num_scalar_prefetch=1,
            grid=(bsz, ns),
            in_specs=[pl.BlockSpec((1, tps, ne, LANES), lambda b, s, rs: (b, s, 0, 0)),
                      pl.BlockSpec(memory_space=pl.ANY),
                      pl.BlockSpec((1, tps * LANES, d), lambda b, s, rs: (b, s, 0)),
                      pl.BlockSpec((1, 1, d), lambda b, s, rs: (b, 0, 0)),
                      pl.BlockSpec((1, d), lambda b, s, rs: (0, 0))],
            out_specs=pl.BlockSpec((1, tps * LANES, d), lambda b, s, rs: (b, s, 0)),
            scratch_shapes=[pltpu.VMEM((2, tps * ne * SLAB, d), BF16),
                            pltpu.VMEM((ne * SLAB, d), BF16),
                            pltpu.SemaphoreType.DMA((2,)),
                            pltpu.SemaphoreType.DMA(())]),
        out_shape=jax.ShapeDtypeStruct((bsz, n, d), F32),
        compiler_params=_params("arbitrary", "arbitrary"),
        name="combine_norm",
    )(rs_ext, pos_r, ys, x1, g5, fng.reshape(1, d))


def _block_diag(w):
    heads, hd, _ = w.shape
    eye = jnp.eye(heads, dtype=w.dtype)
    return (eye[:, None, :, None] * w[:, :, None, :]).reshape(heads * hd, heads * hd)


def _tile(n, pref):
    return pref if n % pref == 0 else n


def kernel(x, c, ctx, c_ctx, norm1_g, norm2_g, ada_w, ada_b, w_in, b_in, conv_dw_w, conv_dw_b, conv_ln_g, conv_ln_b, lru_conv_w, lru_conv_b, lru_wa, lru_ba, lru_wi, lru_bi, lru_lambda, w_out, b_out, router_w, exp_w_gate, exp_w_up, exp_w_down, final_norm_g):
    assert norm1_g.shape[0] == 1
    mod, x1, v, aff = _mixer(x, c, ctx, c_ctx, norm1_g[0], norm2_g[0], ada_w[0], ada_b[0], w_in[0], b_in[0],
                             conv_dw_w[0], conv_dw_b[0], conv_ln_g[0], conv_ln_b[0], lru_conv_w[0],
                             lru_conv_b[0], lru_wa[0], lru_ba[0], lru_wi[0], lru_bi[0], lru_lambda[0],
                             w_out[0], b_out[0], router_w[0])
    return _moe_norm(x1, v, aff, mod[5], exp_w_gate[0], exp_w_up[0], exp_w_down[0], final_norm_g)


def _mixer(x, c, ctx, c_ctx, norm1_g, norm2_g, ada_w, ada_b, w_in, b_in, conv_w, conv_b, ln_g, ln_b,
           lru_cw, lru_cb, lru_wa, lru_ba, lru_wi, lru_bi, lru_lam, w_out, b_out, router_w):
    bsz, n, d = x.shape
    cond8 = jnp.zeros((SUBLANES, d), F32).at[:bsz].set(c).at[bsz].set(c_ctx)
    mods = _ada_mod(cond8, ada_w, ada_b)
    mod = [mods[:bsz, k * d:(k + 1) * d].reshape(bsz, 1, d) for k in range(N_MOD)]
    mod_c = [jnp.broadcast_to(mods[bsz:bsz + 1, k * d:(k + 1) * d].reshape(1, 1, d), (bsz, 1, d))
             for k in range(2)]
    w_in_b = w_in.astype(BF16)
    w_out_b = w_out.astype(BF16)
    ch = lru_cb.shape[1]
    wgate = [jnp.concatenate([_block_diag(lru_wa[dd]), _block_diag(lru_wi[dd])], axis=1).astype(BF16)
             for dd in range(2)]
    bgate = [jnp.concatenate([lru_ba[dd], lru_bi[dd]]) for dd in range(2)]

    def lru(lx, h0, dd, reverse):
        return _lru_dir(lx, h0, lru_cw[dd], lru_cb[dd], wgate[dd], bgate[dd], lru_lam[dd], reverse,
                        _tile(lx.shape[1], 256))

    _, c_lx, _ = _in_proj(ctx, norm1_g, mod_c[0], mod_c[1], w_in_b, b_in, _tile(ctx.shape[1], 256))
    zero_h = jnp.zeros((bsz, 1, ch), F32)
    _, hf0 = lru(c_lx, zero_h, 0, False)
    _, hb0 = lru(c_lx, zero_h, 1, True)

    x_glu, x_lx, x_glg = _in_proj(x, norm1_g, mod[0], mod[1], w_in_b, b_in, _tile(n, 512))
    hf, _ = lru(x_lx, hf0, 0, False)
    hb, _ = lru(x_lx, hb0, 1, True)
    conv_r, conv_c = _conv_grid(x_glu, conv_w, conv_b)
    x1, v, aff = _out_proj_route(conv_r, conv_c, ln_g, ln_b, hf, hb, x_glg, w_out_b, b_out, x, mod[2],
                                 norm2_g, mod[3], mod[4], router_w.T, _tile(n, 512))
    return mod, x1, v, aff


def _moe_norm(x1, v, aff, gate2, wg, wu, wd, final_norm_g):
    bsz, n, d = x1.shape
    ne = aff.shape[1]
    cap = EC_CAPACITY * n // ne
    idx, gate, pos, rstart = _topk_route(aff.reshape(bsz, ne, n // LANES, LANES), cap)
    tok_base = (jnp.arange(bsz, dtype=jnp.int32) * n)[:, None, None]
    idx_flat = jnp.swapaxes(idx + tok_base, 0, 1).reshape(ne * bsz * cap)
    gate_rows = jnp.swapaxes(gate, 0, 1).reshape(ne, bsz * cap // LANES, LANES)
    ys = _expert_ffn(idx_flat, v.reshape(bsz * n, d), gate_rows, wg, wu, wd, 2 * cap, 256)
    rs_ext = jnp.concatenate([rstart[..., 0], jnp.full((bsz, ne, 1), cap, jnp.int32)], axis=-1)
    return _combine_norm(rs_ext.reshape(-1), jnp.swapaxes(pos, 1, 2), ys, x1, gate2, final_norm_g, cap, 4)
```

```python
import functools

import jax
import jax.numpy as jnp
from jax import lax
from jax.experimental import pallas as pl
from jax.experimental.pallas import tpu as pltpu

GRID_W = 64
CONV_WIDTH = 31
CONV_PAD = (CONV_WIDTH - 1) // 2
LRU_CONV_WIDTH = 4
LRU_C = 8.0
N_EXPERTS = 16
EC_CAPACITY = 2
N_MOD = 6
RMS_EPS = 1e-6
LN_EPS = 1e-5

LANES = 128
SUBLANES = 8
BF16_ROWS = 16
VMEM_LIMIT = 56 * 1024 * 1024

F32 = jnp.float32
BF16 = jnp.bfloat16
HIGHEST = lax.Precision.HIGHEST


def _params(*sem):
    return pltpu.CompilerParams(dimension_semantics=sem, vmem_limit_bytes=VMEM_LIMIT)


def _ada_body(c_ref, w_ref, b_ref, o_ref):
    s = c_ref[...]
    s = s * jax.nn.sigmoid(s)
    o_ref[...] = jnp.dot(s, w_ref[...], precision=HIGHEST, preferred_element_type=F32) + b_ref[...]


def _ada_mod(cond8, ada_w, ada_b):
    d, n = ada_w.shape
    tn = n // 4
    return pl.pallas_call(
        _ada_body,
        grid=(n // tn,),
        in_specs=[pl.BlockSpec((SUBLANES, d), lambda j: (0, 0)),
                  pl.BlockSpec((d, tn), lambda j: (0, j)),
                  pl.BlockSpec((1, tn), lambda j: (0, j))],
        out_specs=pl.BlockSpec((SUBLANES, tn), lambda j: (0, j)),
        out_shape=jax.ShapeDtypeStruct((SUBLANES, n), F32),
        compiler_params=_params("arbitrary"),
        name="ada_mod",
    )(cond8, ada_w, ada_b.reshape(1, n))


def _inproj_body(x_ref, g_ref, sh_ref, sc_ref, w_ref, b_ref, glu_ref, lx_ref, glg_ref):
    x = x_ref[0]
    ms = jnp.mean(x * x, axis=-1, keepdims=True)
    y = x * lax.rsqrt(ms + RMS_EPS) * g_ref[...]
    u = y * (1.0 + sc_ref[0]) + sh_ref[0]
    p = jnp.dot(u.astype(BF16), w_ref[...], preferred_element_type=F32) + b_ref[...]
    cc = p.shape[1] // 4
    glu_ref[0] = p[:, :cc] * jax.nn.sigmoid(p[:, cc:2 * cc])
    lx_ref[0] = p[:, 2 * cc:3 * cc]
    glg_ref[0] = jax.nn.gelu(p[:, 3 * cc:])


def _in_proj(x, norm_g, shift, scale, w_bf16, b_in, tm):
    bsz, n, d = x.shape
    n4 = w_bf16.shape[1]
    cc = n4 // 4
    tok = pl.BlockSpec((1, tm, cc), lambda b, i: (b, i, 0))
    return pl.pallas_call(
        _inproj_body,
        grid=(bsz, n // tm),
        in_specs=[pl.BlockSpec((1, tm, d), lambda b, i: (b, i, 0)),
                  pl.BlockSpec((1, d), lambda b, i: (0, 0)),
                  pl.BlockSpec((1, 1, d), lambda b, i: (b, 0, 0)),
                  pl.BlockSpec((1, 1, d), lambda b, i: (b, 0, 0)),
                  pl.BlockSpec((d, n4), lambda b, i: (0, 0)),
                  pl.BlockSpec((1, n4), lambda b, i: (0, 0))],
        out_specs=[tok, tok, tok],
        out_shape=[jax.ShapeDtypeStruct((bsz, n, cc), F32)] * 3,
        compiler_params=_params("arbitrary", "arbitrary"),
        name="in_proj",
    )(x, norm_g.reshape(1, d), shift, scale, w_bf16, b_in.reshape(1, n4))


def _lru_body(*refs, reverse, tl, merge):
    x_ref, h0_ref, cw_ref, cb_ref, wg_ref, bg_ref, lam_ref = refs[:7]
    if merge:
        hother_ref, glg_ref = refs[7:9]
        refs = refs[2:]
    h_ref, hl_ref, ext_ref, a_ref, u_ref, carry_ref, halo_ref = refs[7:]
    i = pl.program_id(1)
    ch = x_ref.shape[2]

    @pl.when(i == 0)
    def _():
        carry_ref[...] = jnp.broadcast_to(h0_ref[0], carry_ref.shape)
        halo_ref[...] = jnp.zeros(halo_ref.shape, F32)

    x = x_ref[0]
    if not reverse:
        ext_ref[0:SUBLANES, :] = halo_ref[...]
        ext_ref[SUBLANES:SUBLANES + tl, :] = x
        taps = [ext_ref[SUBLANES - (LRU_CONV_WIDTH - 1) + j:SUBLANES - (LRU_CONV_WIDTH - 1) + j + tl, :]
                for j in range(LRU_CONV_WIDTH)]
        halo_ref[...] = x[tl - SUBLANES:tl, :]
    else:
        ext_ref[0:tl, :] = x
        ext_ref[tl:tl + SUBLANES, :] = halo_ref[...]
        taps = [ext_ref[(LRU_CONV_WIDTH - 1) - j:(LRU_CONV_WIDTH - 1) - j + tl, :]
                for j in range(LRU_CONV_WIDTH)]
        halo_ref[...] = x[0:SUBLANES, :]
    xc = cb_ref[...]
    for j in range(LRU_CONV_WIDTH):
        xc = xc + cw_ref[j:j + 1, :] * taps[j]

    z = jnp.dot(xc.astype(BF16), wg_ref[...], preferred_element_type=F32) + bg_ref[...]
    r = jax.nn.sigmoid(z[:, :ch])
    ig = jax.nn.sigmoid(z[:, ch:])
    log_a = (-LRU_C) * r * jax.nn.softplus(-lam_ref[...])
    th = jnp.tanh(log_a)
    a_ref[...] = jnp.exp(log_a)
    m2 = -2.0 * th / (1.0 - th)
    u_ref[...] = jnp.where(m2 > 0.0, m2 * lax.rsqrt(m2), 0.0) * (ig * xc)

    ng = tl // SUBLANES
    rowid = lax.broadcasted_iota(jnp.int32, (SUBLANES, ch), 0)

    def group(gi, h):
        g = (ng - 1 - gi) if reverse else gi
        off = pl.multiple_of(g * SUBLANES, SUBLANES)
        a = a_ref[pl.ds(off, SUBLANES), :]
        u = u_ref[pl.ds(off, SUBLANES), :]
        for s in (1, 2, 4):
            if reverse:
                m = rowid < SUBLANES - s
                sh = SUBLANES - s
            else:
                m = rowid >= s
                sh = s
            ap = jnp.where(m, pltpu.roll(a, sh, 0), 1.0)
            up = jnp.where(m, pltpu.roll(u, sh, 0), 0.0)
            u = a * up + u
            a = a * ap
        hh = a * h + u
        if merge:
            u_ref[pl.ds(off, SUBLANES), :] = hh
        else:
            h_ref[0, pl.ds(off, SUBLANES), :] = hh
        edge = hh[0:1, :] if reverse else hh[SUBLANES - 1:SUBLANES, :]
        return jnp.broadcast_to(edge, (SUBLANES, ch))

    h = lax.fori_loop(0, ng, group, carry_ref[...], unroll=8)
    carry_ref[...] = h
    hl_ref[0] = h[0:1, :]
    if merge:
        h_ref[0] = ((u_ref[...] + hother_ref[0]) * glg_ref[0]).astype(BF16)


def _lru_dir(lx, h0, cw, cb, wgate, bgate, lam, reverse, tl, merge_with=None):
    bsz, n, ch = lx.shape
    nc = n // tl
    cmap = (lambda b, i: (b, nc - 1 - i, 0)) if reverse else (lambda b, i: (b, i, 0))
    full = lambda shape: pl.BlockSpec(shape, lambda b, i: (0,) * len(shape))
    merge = merge_with is not None
    extra = list(merge_with) if merge else []
    return pl.pallas_call(
        functools.partial(_lru_body, reverse=reverse, tl=tl, merge=merge),
        grid=(bsz, nc),
        in_specs=[pl.BlockSpec((1, tl, ch), cmap),
                  pl.BlockSpec((1, 1, ch), lambda b, i: (b, 0, 0)),
                  full((LRU_CONV_WIDTH, ch)), full((1, ch)),
                  full((ch, 2 * ch)), full((1, 2 * ch)), full((1, ch))]
                 + [pl.BlockSpec((1, tl, ch), cmap)] * len(extra),
        out_specs=[pl.BlockSpec((1, tl, ch), cmap),
                   pl.BlockSpec((1, 1, ch), lambda b, i: (b, 0, 0))],
        out_shape=[jax.ShapeDtypeStruct((bsz, n, ch), BF16 if merge else F32),
                   jax.ShapeDtypeStruct((bsz, 1, ch), F32)],
        scratch_shapes=[pltpu.VMEM((tl + SUBLANES, ch), F32),
                        pltpu.VMEM((tl, ch), F32),
                        pltpu.VMEM((tl, ch), F32),
                        pltpu.VMEM((SUBLANES, ch), F32),
                        pltpu.VMEM((SUBLANES, ch), F32)],
        compiler_params=_params("arbitrary", "arbitrary"),
        name="lru_rev" if reverse else "lru_fwd",
    )(lx, h0, cw, cb.reshape(1, ch), wgate, bgate.reshape(1, 2 * ch), lam.reshape(1, ch), *extra)


ROW_STRIDE = GRID_W + 2 * BF16_ROWS


def _conv_row_body(x_ref, w_ref, b_ref, o_ref, pad_ref):
    n = x_ref.shape[1]
    nrows = n // GRID_W
    gap = jnp.zeros((BF16_ROWS, LANES), F32)

    def fill(r, c):
        base = pl.multiple_of(r * ROW_STRIDE, SUBLANES)
        src = pl.multiple_of(r * GRID_W, SUBLANES)
        pad_ref[pl.ds(base, BF16_ROWS), :] = gap
        pad_ref[pl.ds(base + BF16_ROWS, GRID_W), :] = x_ref[0, pl.ds(src, GRID_W), :]
        pad_ref[pl.ds(base + BF16_ROWS + GRID_W, BF16_ROWS), :] = gap
        return c

    lax.fori_loop(0, nrows, fill, 0)

    def row(r, c):
        base = pl.multiple_of(r * ROW_STRIDE, SUBLANES)
        acc = jnp.broadcast_to(b_ref[...], (GRID_W, LANES))
        for k in range(CONV_WIDTH):
            acc = acc + w_ref[k:k + 1, :] * pad_ref[pl.ds(base + BF16_ROWS - CONV_PAD + k, GRID_W), :]
        o_ref[0, pl.ds(pl.multiple_of(r * GRID_W, SUBLANES), GRID_W), :] = acc
        return c

    lax.fori_loop(0, nrows, row, 0, unroll=2)


def _conv_col_body(x_ref, w_ref, b_ref, o_ref, pad_ref):
    n = x_ref.shape[1]
    nrows = n // GRID_W
    halo = CONV_PAD * GRID_W
    pad_ref[0:halo, :] = jnp.zeros((halo, LANES), F32)
    pad_ref[halo + n:halo + n + halo, :] = jnp.zeros((halo, LANES), F32)

    def fill(r, c):
        src = pl.multiple_of(r * GRID_W, SUBLANES)
        pad_ref[pl.ds(halo + src, GRID_W), :] = x_ref[0, pl.ds(src, GRID_W), :]
        return c

    lax.fori_loop(0, nrows, fill, 0)

    def row(r, c):
        base = pl.multiple_of(r * GRID_W, SUBLANES)
        acc = jnp.broadcast_to(b_ref[...], (GRID_W, LANES))
        for k in range(CONV_WIDTH):
            acc = acc + w_ref[k:k + 1, :] * pad_ref[pl.ds(base + k * GRID_W, GRID_W), :]
        o_ref[0, pl.ds(base, GRID_W), :] = acc
        return c

    lax.fori_loop(0, nrows, row, 0, unroll=2)


def _conv_grid(glu, w, b):
    bsz, n, ch = glu.shape
    half = ch // 2
    ng = half // LANES
    outs = []
    for body, first, pad_rows in ((_conv_row_body, 0, (n // GRID_W) * ROW_STRIDE),
                                  (_conv_col_body, ng, n + 2 * CONV_PAD * GRID_W)):
        outs.append(pl.pallas_call(
            body,
            grid=(bsz, ng),
            in_specs=[pl.BlockSpec((1, n, LANES), lambda bb, g, first=first: (bb, 0, g + first)),
                      pl.BlockSpec((CONV_WIDTH, LANES), lambda bb, g, first=first: (0, g + first)),
                      pl.BlockSpec((1, LANES), lambda bb, g, first=first: (0, g + first))],
            out_specs=pl.BlockSpec((1, n, LANES), lambda bb, g: (bb, 0, g)),
            out_shape=jax.ShapeDtypeStruct((bsz, n, half), F32),
            scratch_shapes=[pltpu.VMEM((pad_rows, LANES), F32)],
            compiler_params=_params("arbitrary", "arbitrary"),
            name="conv_row" if first == 0 else "conv_col",
        )(glu, w, b.reshape(1, ch)))
    return outs


def _out_body(cr_ref, cc_ref, lng_ref, lnb_ref, yl_ref, wo_ref, bo_ref, x_ref,
              g1_ref, n2g_ref, sh2_ref, sc2_ref, rw_ref, x1_ref, v_ref, aff_ref):
    cv = jnp.concatenate([cr_ref[0], cc_ref[0]], axis=-1)
    mu = jnp.mean(cv, axis=-1, keepdims=True)
    dv = cv - mu
    var = jnp.mean(dv * dv, axis=-1, keepdims=True)
    yn = dv * lax.rsqrt(var + LN_EPS) * lng_ref[...] + lnb_ref[...]
    cy = yn * jax.nn.sigmoid(yn)
    cat = jnp.concatenate([cy.astype(BF16), yl_ref[0]], axis=-1)
    m = jnp.dot(cat, wo_ref[...], preferred_element_type=F32) + bo_ref[...]
    x1 = x_ref[0] + g1_ref[0] * m
    x1_ref[0] = x1
    ms = jnp.mean(x1 * x1, axis=-1, keepdims=True)
    v = x1 * lax.rsqrt(ms + RMS_EPS) * n2g_ref[...]
    v = v * (1.0 + sc2_ref[0]) + sh2_ref[0]
    v_ref[0] = v
    v_hi = v.astype(BF16)
    v_lo = (v - v_hi.astype(F32)).astype(BF16)
    rw = rw_ref[...]
    rw_hi = rw.astype(BF16)
    rw_lo = (rw - rw_hi.astype(F32)).astype(BF16)
    nt = (((1,), (1,)), ((), ()))
    ne = rw.shape[0]
    both = lax.dot_general(jnp.concatenate([rw_hi, rw_lo], axis=0), v_hi, nt, preferred_element_type=F32)
    lg = both[:ne] + both[ne:] + lax.dot_general(rw_hi, v_lo, nt, preferred_element_type=F32)
    ex = jnp.exp(lg - jnp.max(lg, axis=0, keepdims=True))
    aff_ref[0] = ex / jnp.sum(ex, axis=0, keepdims=True)


def _out_proj_route(conv_r, conv_c, ln_g, ln_b, yl, wo_bf16, b_out, x, g1, n2g, sh2, sc2, rw_t, tm):
    bsz, n, d = x.shape
    half = conv_r.shape[2]
    ch = yl.shape[2]
    ne = rw_t.shape[0]
    tok = lambda c: pl.BlockSpec((1, tm, c), lambda b, i: (b, i, 0))
    full = lambda shape: pl.BlockSpec(shape, lambda b, i: (0,) * len(shape))
    per_b = pl.BlockSpec((1, 1, d), lambda b, i: (b, 0, 0))
    return pl.pallas_call(
        _out_body,
        grid=(bsz, n // tm),
        in_specs=[tok(half), tok(half), full((1, 2 * half)), full((1, 2 * half)),
                  tok(ch), full((2 * half + ch, d)), full((1, d)), tok(d),
                  per_b, full((1, d)), per_b, per_b, full((ne, d))],
        out_specs=[tok(d), tok(d), pl.BlockSpec((1, ne, tm), lambda b, i: (b, 0, i))],
        out_shape=[jax.ShapeDtypeStruct((bsz, n, d), F32),
                   jax.ShapeDtypeStruct((bsz, n, d), F32),
                   jax.ShapeDtypeStruct((bsz, ne, n), F32)],
        compiler_params=_params("arbitrary", "arbitrary"),
        name="out_proj_route",
    )(conv_r, conv_c, ln_g.reshape(1, -1), ln_b.reshape(1, -1), yl, wo_bf16,
      b_out.reshape(1, d), x, g1, n2g.reshape(1, d), sh2, sc2, rw_t)


def _token_cumsum(m, rows_per_expert):
    er = m.shape[0]
    li = lax.broadcasted_iota(jnp.int32, (LANES, LANES), 0)
    lj = lax.broadcasted_iota(jnp.int32, (LANES, LANES), 1)
    upper = jnp.where(li <= lj, 1.0, 0.0).astype(BF16)
    cs = jnp.dot(m.astype(BF16), upper, preferred_element_type=F32)
    rt = jnp.broadcast_to(cs[:, LANES - 1:LANES], (er, LANES)).astype(BF16)
    ri = lax.broadcasted_iota(jnp.int32, (er, er), 0)
    ci = lax.broadcasted_iota(jnp.int32, (er, er), 1)
    same = (ri // rows_per_expert) == (ci // rows_per_expert)
    lower = jnp.where(same & (ci < ri), 1.0, 0.0).astype(BF16)
    rstart = jnp.dot(lower, rt, preferred_element_type=F32)
    return cs, rstart


def _topk_body(aff_ref, idx_ref, gate_ref, pos_ref, rs_ref, cs_scr, rs_scr, *, cap):
    aff = aff_ref[0]
    ne, rows, _ = aff.shape

    def bisect(i, thr):
        cand = thr | jnp.left_shift(jnp.int32(1), 30 - i)
        cnt = jnp.sum((aff >= pltpu.bitcast(cand, F32)).astype(jnp.int32), axis=(1, 2), keepdims=True)
        return jnp.where(cnt >= cap, cand, thr)

    thr = lax.fori_loop(0, 31, bisect, jnp.zeros((ne, 1, 1), jnp.int32))
    gt = aff >= pltpu.bitcast(thr + 1, F32)
    eq = (aff >= pltpu.bitcast(thr, F32)) & jnp.logical_not(gt)
    need = (cap - jnp.sum(gt.astype(jnp.int32), axis=(1, 2), keepdims=True)).astype(F32)
    eqf = jnp.where(eq, 1.0, 0.0)
    rank_in_row, rank_row0 = _token_cumsum(eqf.reshape(ne * rows, LANES), rows)
    rank_excl = (rank_in_row + rank_row0).reshape(ne, rows, LANES) - eqf
    sel = gt | (eq & (rank_excl < need))
    self_ = jnp.where(sel, 1.0, 0.0)
    cs, rstart = _token_cumsum(self_.reshape(ne * rows, LANES), rows)
    cs3 = cs.reshape(ne, rows, LANES)
    rstart3 = rstart.reshape(ne, rows, LANES)
    cs_scr[...] = cs3
    rs_scr[...] = rstart3
    pos_ref[0] = jnp.where(sel, cs3 + rstart3 - 1.0, -1.0).astype(jnp.int32)
    rs_ref[0] = rstart3.astype(jnp.int32)

    jrow = lax.broadcasted_iota(jnp.int32, (1, cap), 1).astype(F32)
    sub_r = lax.broadcasted_iota(jnp.int32, (rows, cap), 0).astype(F32)
    sub_l = lax.broadcasted_iota(jnp.int32, (LANES, cap), 0).astype(F32)

    def per_expert(e, c):
        cl = cs_scr[e]
        af = aff_ref[0, e]
        rowtot = cl[:, LANES - 1:LANES]
        before = rowtot + rs_scr[e][:, 0:1] <= jrow
        rj = jnp.sum(jnp.where(before, 1.0, 0.0), axis=0, keepdims=True)
        rowbase = jnp.sum(jnp.where(before, rowtot, 0.0), axis=0, keepdims=True)
        onehot = jnp.where(sub_r == rj, 1.0, 0.0).astype(BF16)
        a1 = af.astype(BF16)
        r1 = af - a1.astype(F32)
        a2 = r1.astype(BF16)
        a3 = (r1 - a2.astype(F32)).astype(BF16)
        lhs = jnp.concatenate([cl.astype(BF16), a1, a2, a3], axis=1)
        gathered = lax.dot_general(lhs, onehot, (((0,), (0,)), ((), ())), preferred_element_type=F32)
        g = gathered[:LANES]
        ga = gathered[LANES:2 * LANES] + gathered[2 * LANES:3 * LANES] + gathered[3 * LANES:]
        lanepos = jnp.sum(jnp.where(g <= jrow - rowbase, 1.0, 0.0), axis=0, keepdims=True)
        gate = jnp.sum(jnp.where(sub_l == lanepos, ga, 0.0), axis=0, keepdims=True)
        idx_ref[0, pl.ds(e, 1), :] = (rj * float(LANES) + lanepos).astype(jnp.int32)
        gate_ref[0, pl.ds(e, 1), :] = gate
        return c

    lax.fori_loop(0, ne, per_expert, 0)


def _topk_route(aff4, cap):
    bsz, ne, rows, _ = aff4.shape
    blk4 = pl.BlockSpec((1, ne, rows, LANES), lambda b: (b, 0, 0, 0))
    lst = pl.BlockSpec((1, ne, cap), lambda b: (b, 0, 0))
    return pl.pallas_call(
        functools.partial(_topk_body, cap=cap),
        grid=(bsz,),
        in_specs=[blk4],
        out_specs=[lst, lst, blk4, blk4],
        out_shape=[jax.ShapeDtypeStruct((bsz, ne, cap), jnp.int32),
                   jax.ShapeDtypeStruct((bsz, ne, cap), F32),
                   jax.ShapeDtypeStruct((bsz, ne, rows, LANES), jnp.int32),
                   jax.ShapeDtypeStruct((bsz, ne, rows, LANES), jnp.int32)],
        scratch_shapes=[pltpu.VMEM((ne, rows, LANES), F32), pltpu.VMEM((ne, rows, LANES), F32)],
        compiler_params=_params("arbitrary"),
        name="topk_route",
    )(aff4)


def _ffn_body(idx_ref, v_hbm, gate_ref, wg_ref, wu_ref, wd_ref, ys_hbm, ring, xb, acc, stage, sems, osems,
              *, mg, nf, chunk, sub, och):
    e = pl.program_id(0)
    g = pl.program_id(1)
    f = pl.program_id(2)
    ngroups = pl.num_programs(1)
    grp = e * ngroups + g
    last_grp = pl.num_programs(0) * ngroups - 1
    cur = grp % 2
    first = jnp.logical_and(grp == 0, f == 0)

    def chunk_start(c):
        return pl.multiple_of(jnp.minimum(c * chunk, mg - chunk), BF16_ROWS)

    def row_copy(src_row, slot, r):
        return pltpu.make_async_copy(v_hbm.at[pl.ds(src_row, 1)], ring.at[slot, pl.ds(r, 1)], sems.at[slot])

    def land(c, slot, xslot):
        pltpu.make_async_copy(v_hbm.at[pl.ds(0, chunk)], ring.at[slot], sems.at[slot]).wait()
        xb[xslot, pl.ds(chunk_start(c), chunk), :] = ring[slot].astype(BF16)

    @pl.when(first)
    def _():
        def one_chunk(c, carry):
            base = chunk_start(c)

            def issue(r, cc):
                row_copy(idx_ref[base + r], 0, r).start()
                return cc

            lax.fori_loop(0, chunk, issue, 0, unroll=8)
            land(c, 0, 0)
            return carry

        lax.fori_loop(0, nf, one_chunk, 0)

    @pl.when(jnp.logical_not(first))
    def _():
        lc = jnp.where(f == 0, nf - 1, f - 1)
        land(lc, lc % 2, jnp.where(f == 0, cur, 1 - cur))

    @pl.when(f == 0)
    def _():
        acc[...] = jnp.zeros(acc.shape, F32)

    nxt_base = jnp.minimum(grp + 1, last_grp) * mg + chunk_start(f)
    for r in range(chunk):
        row_copy(idx_ref[nxt_base + r], f % 2, r).start(priority=r % 2)

    wgb = wg_ref[0].astype(BF16)
    wub = wu_ref[0].astype(BF16)
    wdb = wd_ref[0].astype(BF16)
    for i in range(mg // sub):
        rows = slice(i * sub, (i + 1) * sub)
        x = xb[cur, rows, :]
        gg = jnp.dot(x, wgb, preferred_element_type=F32)
        uu = jnp.dot(x, wub, preferred_element_type=F32)
        h = (gg * jax.nn.sigmoid(gg) * uu).astype(BF16)
        acc[rows, :] += jnp.dot(h, wdb, preferred_element_type=F32)

    @pl.when(f == nf - 1)
    def _():
        gate_t = gate_ref[0].T

        def out_copy(c):
            return pltpu.make_async_copy(stage.at[c % 2], ys_hbm.at[e, pl.ds(g * mg + c * och, och)],
                                         osems.at[c % 2])

        nchunks = mg // och
        for c in range(nchunks):
            if c >= 2:
                out_copy(c - 2).wait()
            for i in range(och // LANES):
                blk = c * (och // LANES) + i
                rows = slice(blk * LANES, (blk + 1) * LANES)
                stage[c % 2, i * LANES:(i + 1) * LANES, :] = (acc[rows, :] * gate_t[:, blk:blk + 1]).astype(BF16)
            out_copy(c).start()
        for c in range(max(nchunks - 2, 0), nchunks):
            out_copy(c).wait()

    @pl.when(jnp.logical_and(grp == last_grp, f == nf - 1))
    def _():
        pltpu.make_async_copy(v_hbm.at[pl.ds(0, chunk)], ring.at[(nf - 1) % 2], sems.at[(nf - 1) % 2]).wait()


def _expert_ffn(idx_flat, v_flat, gate_col, wg, wu, wd, mg, tf):
    ne, d, fdim = wg.shape
    rows = gate_col.shape[1] * LANES
    nf = fdim // tf
    chunk = pl.cdiv(pl.cdiv(mg, nf), BF16_ROWS) * BF16_ROWS
    sub = min(mg, 2048)
    och = min(mg, 512)
    return pl.pallas_call(
        functools.partial(_ffn_body, mg=mg, nf=nf, chunk=chunk, sub=sub, och=och),
        grid_spec=pltpu.PrefetchScalarGridSpec(
            num_scalar_prefetch=1,
            grid=(ne, rows // mg, nf),
            in_specs=[pl.BlockSpec(memory_space=pl.ANY),
                      pl.BlockSpec((1, mg // LANES, LANES), lambda e, g, f, idx: (e, g, 0)),
                      pl.BlockSpec((1, d, tf), lambda e, g, f, idx: (e, 0, f)),
                      pl.BlockSpec((1, d, tf), lambda e, g, f, idx: (e, 0, f)),
                      pl.BlockSpec((1, tf, d), lambda e, g, f, idx: (e, f, 0))],
            out_specs=pl.BlockSpec(memory_space=pl.ANY),
            scratch_shapes=[pltpu.VMEM((2, chunk, d), F32),
                            pltpu.VMEM((2, mg, d), BF16),
                            pltpu.VMEM((mg, d), F32),
                            pltpu.VMEM((2, och, d), BF16),
                            pltpu.SemaphoreType.DMA((2,)),
                            pltpu.SemaphoreType.DMA((2,))]),
        out_shape=jax.ShapeDtypeStruct((ne, rows, d), BF16),
        compiler_params=_params("arbitrary", "arbitrary", "arbitrary"),
        name="expert_ffn",
    )(idx_flat, v_flat, gate_col, wg, wu, wd)


SLAB = 48


def _combine_body(rs_ref, pos_ref, ys_hbm, x1_ref, g5_ref, fng_ref, o_ref, stk, stk_x, sems, sem_x, *, cap, tps):
    b = pl.program_id(0)
    s = pl.program_id(1)
    nb = pl.num_programs(0)
    ns = pl.num_programs(1)
    ne = pos_ref.shape[2]
    nr = ns * tps
    step = b * ns + s
    slot = step % 2
    tile_rows = ne * SLAB

    def first_pos(bb, rr, e):
        return rs_ref[(bb * ne + e) * (nr + 1) + rr]

    def aligned(p0):
        return lax.shift_left(lax.shift_right_logical(p0, 4), 4)

    def slab_start(a0, rnd):
        return pl.multiple_of(jnp.minimum(a0 + rnd * SLAB, cap - SLAB), BF16_ROWS)

    def slab_copy(bb, e, a, dst, row, sem):
        return pltpu.make_async_copy(ys_hbm.at[e, pl.ds(bb * cap + a, SLAB)], dst.at[pl.ds(row, SLAB)], sem)

    def issue(bb, ss, sl):
        for j in range(tps):
            for e in range(ne):
                a = slab_start(aligned(first_pos(bb, ss * tps + j, e)), 0)
                slab_copy(bb, e, a, stk.at[sl], j * tile_rows + e * SLAB, sems.at[sl]).start()

    @pl.when(step == 0)
    def _():
        issue(b, s, slot)

    @pl.when(step + 1 < nb * ns)
    def _():
        nxt = step + 1
        issue(nxt // ns, nxt % ns, 1 - slot)

    kiota = lax.broadcasted_iota(jnp.int32, (SLAB, LANES), 0)
    tn = (((0,), (0,)), ((), ()))
    pltpu.make_async_copy(stk.at[slot], stk.at[slot], sems.at[slot]).wait()

    for j in range(tps):
        r = s * tps + j
        pos = pos_ref[0, j]
        a0 = [aligned(first_pos(b, r, e)) for e in range(ne)]

        def onehot(rnd, pos=pos, a0=a0):
            blocks = []
            for e in range(ne):
                rel = pos[e:e + 1, :] - slab_start(a0[e], rnd)
                fresh = (pos[e:e + 1, :] - a0[e]) >= rnd * SLAB
                blocks.append(jnp.where((kiota == rel) & fresh, 1.0, 0.0))
            return jnp.concatenate(blocks, axis=0).astype(BF16)

        moe = lax.dot_general(onehot(0), stk[slot, j * tile_rows:(j + 1) * tile_rows, :], tn,
                              preferred_element_type=F32)

        span = first_pos(b, r + 1, 0) - a0[0]
        for e in range(1, ne):
            span = jnp.maximum(span, first_pos(b, r + 1, e) - a0[e])
        rounds = jnp.maximum(1, (span + (SLAB - 1)) // SLAB)

        def extra(rnd, m, a0=a0, onehot=onehot):
            for e in range(ne):
                slab_copy(b, e, slab_start(a0[e], rnd), stk_x, e * SLAB, sem_x).start()
            pltpu.make_async_copy(stk_x, stk_x, sem_x).wait()
            return m + lax.dot_general(onehot(rnd), stk_x[...], tn, preferred_element_type=F32)

        moe = lax.fori_loop(1, rounds, extra, moe)

        y = x1_ref[0, j * LANES:(j + 1) * LANES, :] + g5_ref[0] * moe
        ms = jnp.mean(y * y, axis=-1, keepdims=True)
        o_ref[0, j * LANES:(j + 1) * LANES, :] = y * lax.rsqrt(ms + RMS_EPS) * fng_ref[...]


def _combine_norm(rs_ext, pos_r, ys, x1, g5, fng, cap, tps):
    bsz, n, d = x1.shape
    ns = n // (LANES * tps)
    ne = pos_r.shape[2]
    return pl.pallas_call(
        functools.partial(_combine_body, cap=cap, tps=tps),
        grid_spec=pltpu.PrefetchScalarGridSpec(
            num_scalar_prefetch=1,
            grid=(bsz, ns),
            in_specs=[pl.BlockSpec((1, tps, ne, LANES), lambda b, s, rs: (b, s, 0, 0)),
                      pl.BlockSpec(memory_space=pl.ANY),
                      pl.BlockSpec((1, tps * LANES, d), lambda b, s, rs: (b, s, 0)),
                      pl.BlockSpec((1, 1, d), lambda b, s, rs: (b, 0, 0)),
                      pl.BlockSpec((1, d), lambda b, s, rs: (0, 0))],
            out_specs=pl.BlockSpec((1, tps * LANES, d), lambda b, s, rs: (b, s, 0)),
            scratch_shapes=[pltpu.VMEM((2, tps * ne * SLAB, d), BF16),
                            pltpu.VMEM((ne * SLAB, d), BF16),
                            pltpu.SemaphoreType.DMA((2,)),
                            pltpu.SemaphoreType.DMA(())]),
        out_shape=jax.ShapeDtypeStruct((bsz, n, d), F32),
        compiler_params=_params("arbitrary", "arbitrary"),
        name="combine_norm",
    )(rs_ext, pos_r, ys, x1, g5, fng.reshape(1, d))


def _block_diag(w):
    heads, hd, _ = w.shape
    eye = jnp.eye(heads, dtype=w.dtype)
    return (eye[:, None, :, None] * w[:, :, None, :]).reshape(heads * hd, heads * hd)


def _tile(n, pref):
    return pref if n % pref == 0 else n


def kernel(x, c, ctx, c_ctx, norm1_g, norm2_g, ada_w, ada_b, w_in, b_in, conv_dw_w, conv_dw_b, conv_ln_g, conv_ln_b, lru_conv_w, lru_conv_b, lru_wa, lru_ba, lru_wi, lru_bi, lru_lambda, w_out, b_out, router_w, exp_w_gate, exp_w_up, exp_w_down, final_norm_g):
    assert norm1_g.shape[0] == 1
    mod, x1, v, aff = _mixer(x, c, ctx, c_ctx, norm1_g[0], norm2_g[0], ada_w[0], ada_b[0], w_in[0], b_in[0],
                             conv_dw_w[0], conv_dw_b[0], conv_ln_g[0], conv_ln_b[0], lru_conv_w[0],
                             lru_conv_b[0], lru_wa[0], lru_ba[0], lru_wi[0], lru_bi[0], lru_lambda[0],
                             w_out[0], b_out[0], router_w[0])
    return _moe_norm(x1, v, aff, mod[5], exp_w_gate[0], exp_w_up[0], exp_w_down[0], final_norm_g)


def _mixer(x, c, ctx, c_ctx, norm1_g, norm2_g, ada_w, ada_b, w_in, b_in, conv_w, conv_b, ln_g, ln_b,
           lru_cw, lru_cb, lru_wa, lru_ba, lru_wi, lru_bi, lru_lam, w_out, b_out, router_w):
    bsz, n, d = x.shape
    cond8 = jnp.zeros((SUBLANES, d), F32).at[:bsz].set(c).at[bsz].set(c_ctx)
    mods = _ada_mod(cond8, ada_w, ada_b)
    mod = [mods[:bsz, k * d:(k + 1) * d].reshape(bsz, 1, d) for k in range(N_MOD)]
    mod_c = [jnp.broadcast_to(mods[bsz:bsz + 1, k * d:(k + 1) * d].reshape(1, 1, d), (bsz, 1, d))
             for k in range(2)]
    w_in_b = w_in.astype(BF16)
    w_out_b = w_out.astype(BF16)
    ch = lru_cb.shape[1]
    wgate = [jnp.concatenate([_block_diag(lru_wa[dd]), _block_diag(lru_wi[dd])], axis=1).astype(BF16)
             for dd in range(2)]
    bgate = [jnp.concatenate([lru_ba[dd], lru_bi[dd]]) for dd in range(2)]

    def lru(lx, h0, dd, reverse, merge_with=None):
        return _lru_dir(lx, h0, lru_cw[dd], lru_cb[dd], wgate[dd], bgate[dd], lru_lam[dd], reverse,
                        _tile(lx.shape[1], 256), merge_with)

    _, c_lx, _ = _in_proj(ctx, norm1_g, mod_c[0], mod_c[1], w_in_b, b_in, _tile(ctx.shape[1], 256))
    zero_h = jnp.zeros((bsz, 1, ch), F32)
    _, hf0 = lru(c_lx, zero_h, 0, False)
    _, hb0 = lru(c_lx, zero_h, 1, True)

    x_glu, x_lx, x_glg = _in_proj(x, norm1_g, mod[0], mod[1], w_in_b, b_in, _tile(n, 512))
    hb, _ = lru(x_lx, hb0, 1, True)
    yl, _ = lru(x_lx, hf0, 0, False, merge_with=(hb, x_glg))
    conv_r, conv_c = _conv_grid(x_glu, conv_w, conv_b)
    x1, v, aff = _out_proj_route(conv_r, conv_c, ln_g, ln_b, yl, w_out_b, b_out, x, mod[2],
                                 norm2_g, mod[3], mod[4], router_w.T, _tile(n, 512))
    return mod, x1, v, aff


def _moe_norm(x1, v, aff, gate2, wg, wu, wd, final_norm_g):
    bsz, n, d = x1.shape
    ne = aff.shape[1]
    cap = EC_CAPACITY * n // ne
    idx, gate, pos, rstart = _topk_route(aff.reshape(bsz, ne, n // LANES, LANES), cap)
    tok_base = (jnp.arange(bsz, dtype=jnp.int32) * n)[:, None, None]
    idx_flat = jnp.swapaxes(idx + tok_base, 0, 1).reshape(ne * bsz * cap)
    gate_rows = jnp.swapaxes(gate, 0, 1).reshape(ne, bsz * cap // LANES, LANES)
    ys = _expert_ffn(idx_flat, v.reshape(bsz * n, d), gate_rows, wg, wu, wd, bsz * cap, 256)
    rs_ext = jnp.concatenate([rstart[..., 0], jnp.full((bsz, ne, 1), cap, jnp.int32)], axis=-1)
    return _combine_norm(rs_ext.reshape(-1), jnp.swapaxes(pos, 1, 2), ys, x1, gate2, final_norm_g, cap, 4)
```

```python
import functools

import jax
import jax.numpy as jnp
from jax import lax
from jax.experimental import pallas as pl
from jax.experimental.pallas import tpu as pltpu

GRID_W = 64
CONV_WIDTH = 31
CONV_PAD = (CONV_WIDTH - 1) // 2
LRU_CONV_WIDTH = 4
LRU_C = 8.0
N_EXPERTS = 16
EC_CAPACITY = 2
N_MOD = 6
RMS_EPS = 1e-6
LN_EPS = 1e-5

LANES = 128
SUBLANES = 8
BF16_ROWS = 16
VMEM_LIMIT = 56 * 1024 * 1024

F32 = jnp.float32
BF16 = jnp.bfloat16
HIGHEST = lax.Precision.HIGHEST


def _params(*sem):
    return pltpu.CompilerParams(dimension_semantics=sem, vmem_limit_bytes=VMEM_LIMIT)


def _ada_body(c_ref, w_ref, b_ref, o_ref):
    s = c_ref[...]
    s = s * jax.nn.sigmoid(s)
    o_ref[...] = jnp.dot(s, w_ref[...], precision=HIGHEST, preferred_element_type=F32) + b_ref[...]


def _ada_mod(cond8, ada_w, ada_b):
    d, n = ada_w.shape
    tn = n // 4
    return pl.pallas_call(
        _ada_body,
        grid=(n // tn,),
        in_specs=[pl.BlockSpec((SUBLANES, d), lambda j: (0, 0)),
                  pl.BlockSpec((d, tn), lambda j: (0, j)),
                  pl.BlockSpec((1, tn), lambda j: (0, j))],
        out_specs=pl.BlockSpec((SUBLANES, tn), lambda j: (0, j)),
        out_shape=jax.ShapeDtypeStruct((SUBLANES, n), F32),
        compiler_params=_params("arbitrary"),
        name="ada_mod",
    )(cond8, ada_w, ada_b.reshape(1, n))


def _inproj_body(x_ref, g_ref, sh_ref, sc_ref, w_ref, b_ref, glu_ref, lx_ref, glg_ref):
    x = x_ref[0]
    ms = jnp.mean(x * x, axis=-1, keepdims=True)
    y = x * lax.rsqrt(ms + RMS_EPS) * g_ref[...]
    u = y * (1.0 + sc_ref[0]) + sh_ref[0]
    p = jnp.dot(u.astype(BF16), w_ref[...], preferred_element_type=F32) + b_ref[...]
    cc = p.shape[1] // 4
    glu_ref[0] = p[:, :cc] * jax.nn.sigmoid(p[:, cc:2 * cc])
    lx_ref[0] = p[:, 2 * cc:3 * cc]
    glg_ref[0] = jax.nn.gelu(p[:, 3 * cc:])


def _in_proj(x, norm_g, shift, scale, w_bf16, b_in, tm):
    bsz, n, d = x.shape
    n4 = w_bf16.shape[1]
    cc = n4 // 4
    tok = pl.BlockSpec((1, tm, cc), lambda b, i: (b, i, 0))
    return pl.pallas_call(
        _inproj_body,
        grid=(bsz, n // tm),
        in_specs=[pl.BlockSpec((1, tm, d), lambda b, i: (b, i, 0)),
                  pl.BlockSpec((1, d), lambda b, i: (0, 0)),
                  pl.BlockSpec((1, 1, d), lambda b, i: (b, 0, 0)),
                  pl.BlockSpec((1, 1, d), lambda b, i: (b, 0, 0)),
                  pl.BlockSpec((d, n4), lambda b, i: (0, 0)),
                  pl.BlockSpec((1, n4), lambda b, i: (0, 0))],
        out_specs=[tok, tok, tok],
        out_shape=[jax.ShapeDtypeStruct((bsz, n, cc), F32)] * 3,
        compiler_params=_params("arbitrary", "arbitrary"),
        name="in_proj",
    )(x, norm_g.reshape(1, d), shift, scale, w_bf16, b_in.reshape(1, n4))


def _lru_body(*refs, reverse, tl, merge):
    x_ref, h0_ref, cw_ref, cb_ref, wg_ref, bg_ref, lam_ref = refs[:7]
    if merge:
        hother_ref, glg_ref = refs[7:9]
        refs = refs[2:]
    h_ref, hl_ref, ext_ref, a_ref, u_ref, carry_ref, halo_ref = refs[7:]
    i = pl.program_id(1)
    ch = x_ref.shape[2]

    @pl.when(i == 0)
    def _():
        carry_ref[...] = jnp.broadcast_to(h0_ref[0], carry_ref.shape)
        halo_ref[...] = jnp.zeros(halo_ref.shape, F32)

    x = x_ref[0]
    if not reverse:
        ext_ref[0:SUBLANES, :] = halo_ref[...]
        ext_ref[SUBLANES:SUBLANES + tl, :] = x
        taps = [ext_ref[SUBLANES - (LRU_CONV_WIDTH - 1) + j:SUBLANES - (LRU_CONV_WIDTH - 1) + j + tl, :]
                for j in range(LRU_CONV_WIDTH)]
        halo_ref[...] = x[tl - SUBLANES:tl, :]
    else:
        ext_ref[0:tl, :] = x
        ext_ref[tl:tl + SUBLANES, :] = halo_ref[...]
        taps = [ext_ref[(LRU_CONV_WIDTH - 1) - j:(LRU_CONV_WIDTH - 1) - j + tl, :]
                for j in range(LRU_CONV_WIDTH)]
        halo_ref[...] = x[0:SUBLANES, :]
    xc = cb_ref[...]
    for j in range(LRU_CONV_WIDTH):
        xc = xc + cw_ref[j:j + 1, :] * taps[j]

    z = jnp.dot(xc.astype(BF16), wg_ref[...], preferred_element_type=F32) + bg_ref[...]
    r = jax.nn.sigmoid(z[:, :ch])
    ig = jax.nn.sigmoid(z[:, ch:])
    log_a = (-LRU_C) * r * jax.nn.softplus(-lam_ref[...])
    th = jnp.tanh(log_a)
    a_ref[...] = jnp.exp(log_a)
    m2 = -2.0 * th / (1.0 - th)
    u_ref[...] = jnp.where(m2 > 0.0, m2 * lax.rsqrt(m2), 0.0) * (ig * xc)

    ng = tl // SUBLANES
    rowid = lax.broadcasted_iota(jnp.int32, (SUBLANES, ch), 0)

    def group(gi, h):
        g = (ng - 1 - gi) if reverse else gi
        off = pl.multiple_of(g * SUBLANES, SUBLANES)
        a = a_ref[pl.ds(off, SUBLANES), :]
        u = u_ref[pl.ds(off, SUBLANES), :]
        for s in (1, 2, 4):
            if reverse:
                m = rowid < SUBLANES - s
                sh = SUBLANES - s
            else:
                m = rowid >= s
                sh = s
            ap = jnp.where(m, pltpu.roll(a, sh, 0), 1.0)
            up = jnp.where(m, pltpu.roll(u, sh, 0), 0.0)
            u = a * up + u
            a = a * ap
        hh = a * h + u
        if merge:
            u_ref[pl.ds(off, SUBLANES), :] = hh
        else:
            h_ref[0, pl.ds(off, SUBLANES), :] = hh
        edge = hh[0:1, :] if reverse else hh[SUBLANES - 1:SUBLANES, :]
        return jnp.broadcast_to(edge, (SUBLANES, ch))

    h = lax.fori_loop(0, ng, group, carry_ref[...], unroll=8)
    carry_ref[...] = h
    hl_ref[0] = h[0:1, :]
    if merge:
        h_ref[0] = ((u_ref[...] + hother_ref[0]) * glg_ref[0]).astype(BF16)


def _lru_dir(lx, h0, cw, cb, wgate, bgate, lam, reverse, tl, merge_with=None):
    bsz, n, ch = lx.shape
    nc = n // tl
    cmap = (lambda b, i: (b, nc - 1 - i, 0)) if reverse else (lambda b, i: (b, i, 0))
    full = lambda shape: pl.BlockSpec(shape, lambda b, i: (0,) * len(shape))
    merge = merge_with is not None
    extra = list(merge_with) if merge else []
    return pl.pallas_call(
        functools.partial(_lru_body, reverse=reverse, tl=tl, merge=merge),
        grid=(bsz, nc),
        in_specs=[pl.BlockSpec((1, tl, ch), cmap),
                  pl.BlockSpec((1, 1, ch), lambda b, i: (b, 0, 0)),
                  full((LRU_CONV_WIDTH, ch)), full((1, ch)),
                  full((ch, 2 * ch)), full((1, 2 * ch)), full((1, ch))]
                 + [pl.BlockSpec((1, tl, ch), cmap)] * len(extra),
        out_specs=[pl.BlockSpec((1, tl, ch), cmap),
                   pl.BlockSpec((1, 1, ch), lambda b, i: (b, 0, 0))],
        out_shape=[jax.ShapeDtypeStruct((bsz, n, ch), BF16 if merge else F32),
                   jax.ShapeDtypeStruct((bsz, 1, ch), F32)],
        scratch_shapes=[pltpu.VMEM((tl + SUBLANES, ch), F32),
                        pltpu.VMEM((tl, ch), F32),
                        pltpu.VMEM((tl, ch), F32),
                        pltpu.VMEM((SUBLANES, ch), F32),
                        pltpu.VMEM((SUBLANES, ch), F32)],
        compiler_params=_params("arbitrary", "arbitrary"),
        name="lru_rev" if reverse else "lru_fwd",
    )(lx, h0, cw, cb.reshape(1, ch), wgate, bgate.reshape(1, 2 * ch), lam.reshape(1, ch), *extra)


ROW_STRIDE = GRID_W + 2 * BF16_ROWS


def _conv_row_body(x_ref, w_ref, b_ref, o_ref, pad_ref):
    n = x_ref.shape[1]
    nrows = n // GRID_W
    gap = jnp.zeros((BF16_ROWS, LANES), F32)

    def fill(r, c):
        base = pl.multiple_of(r * ROW_STRIDE, SUBLANES)
        src = pl.multiple_of(r * GRID_W, SUBLANES)
        pad_ref[pl.ds(base, BF16_ROWS), :] = gap
        pad_ref[pl.ds(base + BF16_ROWS, GRID_W), :] = x_ref[0, pl.ds(src, GRID_W), :]
        pad_ref[pl.ds(base + BF16_ROWS + GRID_W, BF16_ROWS), :] = gap
        return c

    lax.fori_loop(0, nrows, fill, 0)

    def row(r, c):
        base = pl.multiple_of(r * ROW_STRIDE, SUBLANES)
        acc = jnp.broadcast_to(b_ref[...], (GRID_W, LANES))
        for k in range(CONV_WIDTH):
            acc = acc + w_ref[k:k + 1, :] * pad_ref[pl.ds(base + BF16_ROWS - CONV_PAD + k, GRID_W), :]
        o_ref[0, pl.ds(pl.multiple_of(r * GRID_W, SUBLANES), GRID_W), :] = acc
        return c

    lax.fori_loop(0, nrows, row, 0, unroll=2)


def _conv_col_body(x_ref, w_ref, b_ref, o_ref, pad_ref):
    n = x_ref.shape[1]
    nrows = n // GRID_W
    halo = CONV_PAD * GRID_W
    pad_ref[0:halo, :] = jnp.zeros((halo, LANES), F32)
    pad_ref[halo + n:halo + n + halo, :] = jnp.zeros((halo, LANES), F32)

    def fill(r, c):
        src = pl.multiple_of(r * GRID_W, SUBLANES)
        pad_ref[pl.ds(halo + src, GRID_W), :] = x_ref[0, pl.ds(src, GRID_W), :]
        return c

    lax.fori_loop(0, nrows, fill, 0)

    def row(r, c):
        base = pl.multiple_of(r * GRID_W, SUBLANES)
        acc = jnp.broadcast_to(b_ref[...], (GRID_W, LANES))
        for k in range(CONV_WIDTH):
            acc = acc + w_ref[k:k + 1, :] * pad_ref[pl.ds(base + k * GRID_W, GRID_W), :]
        o_ref[0, pl.ds(base, GRID_W), :] = acc
        return c

    lax.fori_loop(0, nrows, row, 0, unroll=2)


def _conv_grid(glu, w, b):
    bsz, n, ch = glu.shape
    half = ch // 2
    ng = half // LANES
    outs = []
    for body, first, pad_rows in ((_conv_row_body, 0, (n // GRID_W) * ROW_STRIDE),
                                  (_conv_col_body, ng, n + 2 * CONV_PAD * GRID_W)):
        outs.append(pl.pallas_call(
            body,
            grid=(bsz, ng),
            in_specs=[pl.BlockSpec((1, n, LANES), lambda bb, g, first=first: (bb, 0, g + first)),
                      pl.BlockSpec((CONV_WIDTH, LANES), lambda bb, g, first=first: (0, g + first)),
                      pl.BlockSpec((1, LANES), lambda bb, g, first=first: (0, g + first))],
            out_specs=pl.BlockSpec((1, n, LANES), lambda bb, g: (bb, 0, g)),
            out_shape=jax.ShapeDtypeStruct((bsz, n, half), F32),
            scratch_shapes=[pltpu.VMEM((pad_rows, LANES), F32)],
            compiler_params=_params("arbitrary", "arbitrary"),
            name="conv_row" if first == 0 else "conv_col",
        )(glu, w, b.reshape(1, ch)))
    return outs


def _out_body(cr_ref, cc_ref, lng_ref, lnb_ref, yl_ref, wo_ref, bo_ref, x_ref,
              g1_ref, n2g_ref, sh2_ref, sc2_ref, rw_ref, x1_ref, v_ref, aff_ref):
    cv = jnp.concatenate([cr_ref[0], cc_ref[0]], axis=-1)
    mu = jnp.mean(cv, axis=-1, keepdims=True)
    dv = cv - mu
    var = jnp.mean(dv * dv, axis=-1, keepdims=True)
    yn = dv * lax.rsqrt(var + LN_EPS) * lng_ref[...] + lnb_ref[...]
    cy = yn * jax.nn.sigmoid(yn)
    cat = jnp.concatenate([cy.astype(BF16), yl_ref[0]], axis=-1)
    m = jnp.dot(cat, wo_ref[...], preferred_element_type=F32) + bo_ref[...]
    x1 = x_ref[0] + g1_ref[0] * m
    x1_ref[0] = x1
    ms = jnp.mean(x1 * x1, axis=-1, keepdims=True)
    v = x1 * lax.rsqrt(ms + RMS_EPS) * n2g_ref[...]
    v = v * (1.0 + sc2_ref[0]) + sh2_ref[0]
    v_ref[0] = v.astype(BF16)
    v_hi = v.astype(BF16)
    v_lo = (v - v_hi.astype(F32)).astype(BF16)
    rw = rw_ref[...]
    rw_hi = rw.astype(BF16)
    rw_lo = (rw - rw_hi.astype(F32)).astype(BF16)
    nt = (((1,), (1,)), ((), ()))
    ne = rw.shape[0]
    both = lax.dot_general(jnp.concatenate([rw_hi, rw_lo], axis=0), v_hi, nt, preferred_element_type=F32)
    lg = both[:ne] + both[ne:] + lax.dot_general(rw_hi, v_lo, nt, preferred_element_type=F32)
    ex = jnp.exp(lg - jnp.max(lg, axis=0, keepdims=True))
    aff_ref[0] = ex / jnp.sum(ex, axis=0, keepdims=True)


def _out_proj_route(conv_r, conv_c, ln_g, ln_b, yl, wo_bf16, b_out, x, g1, n2g, sh2, sc2, rw_t, tm):
    bsz, n, d = x.shape
    half = conv_r.shape[2]
    ch = yl.shape[2]
    ne = rw_t.shape[0]
    tok = lambda c: pl.BlockSpec((1, tm, c), lambda b, i: (b, i, 0))
    full = lambda shape: pl.BlockSpec(shape, lambda b, i: (0,) * len(shape))
    per_b = pl.BlockSpec((1, 1, d), lambda b, i: (b, 0, 0))
    return pl.pallas_call(
        _out_body,
        grid=(bsz, n // tm),
        in_specs=[tok(half), tok(half), full((1, 2 * half)), full((1, 2 * half)),
                  tok(ch), full((2 * half + ch, d)), full((1, d)), tok(d),
                  per_b, full((1, d)), per_b, per_b, full((ne, d))],
        out_specs=[tok(d), tok(d), pl.BlockSpec((1, ne, tm), lambda b, i: (b, 0, i))],
        out_shape=[jax.ShapeDtypeStruct((bsz, n, d), F32),
                   jax.ShapeDtypeStruct((bsz, n, d), BF16),
                   jax.ShapeDtypeStruct((bsz, ne, n), F32)],
        compiler_params=_params("arbitrary", "arbitrary"),
        name="out_proj_route",
    )(conv_r, conv_c, ln_g.reshape(1, -1), ln_b.reshape(1, -1), yl, wo_bf16,
      b_out.reshape(1, d), x, g1, n2g.reshape(1, d), sh2, sc2, rw_t)


def _token_cumsum(m, rows_per_expert):
    er = m.shape[0]
    li = lax.broadcasted_iota(jnp.int32, (LANES, LANES), 0)
    lj = lax.broadcasted_iota(jnp.int32, (LANES, LANES), 1)
    upper = jnp.where(li <= lj, 1.0, 0.0).astype(BF16)
    cs = jnp.dot(m.astype(BF16), upper, preferred_element_type=F32)
    rt = jnp.broadcast_to(cs[:, LANES - 1:LANES], (er, LANES)).astype(BF16)
    ri = lax.broadcasted_iota(jnp.int32, (er, er), 0)
    ci = lax.broadcasted_iota(jnp.int32, (er, er), 1)
    same = (ri // rows_per_expert) == (ci // rows_per_expert)
    lower = jnp.where(same & (ci < ri), 1.0, 0.0).astype(BF16)
    rstart = jnp.dot(lower, rt, preferred_element_type=F32)
    return cs, rstart


def _topk_body(aff_ref, gate_ref, pos_ref, rs_ref, cs_scr, rs_scr, *, cap):
    aff = aff_ref[0]
    ne, rows, _ = aff.shape

    def bisect(i, thr):
        cand = thr | jnp.left_shift(jnp.int32(1), 30 - i)
        cnt = jnp.sum((aff >= pltpu.bitcast(cand, F32)).astype(jnp.int32), axis=(1, 2), keepdims=True)
        return jnp.where(cnt >= cap, cand, thr)

    thr = lax.fori_loop(0, 31, bisect, jnp.zeros((ne, 1, 1), jnp.int32))
    gt = aff >= pltpu.bitcast(thr + 1, F32)
    eq = (aff >= pltpu.bitcast(thr, F32)) & jnp.logical_not(gt)
    need = (cap - jnp.sum(gt.astype(jnp.int32), axis=(1, 2), keepdims=True)).astype(F32)
    eqf = jnp.where(eq, 1.0, 0.0)
    rank_in_row, rank_row0 = _token_cumsum(eqf.reshape(ne * rows, LANES), rows)
    rank_excl = (rank_in_row + rank_row0).reshape(ne, rows, LANES) - eqf
    sel = gt | (eq & (rank_excl < need))
    self_ = jnp.where(sel, 1.0, 0.0)
    cs, rstart = _token_cumsum(self_.reshape(ne * rows, LANES), rows)
    cs3 = cs.reshape(ne, rows, LANES)
    rstart3 = rstart.reshape(ne, rows, LANES)
    cs_scr[...] = cs3
    rs_scr[...] = rstart3
    pos_ref[0] = jnp.where(sel, cs3 + rstart3 - 1.0, -1.0).astype(jnp.int32)
    rs_ref[0] = rstart3.astype(jnp.int32)

    jrow = lax.broadcasted_iota(jnp.int32, (1, cap), 1).astype(F32)
    sub_r = lax.broadcasted_iota(jnp.int32, (rows, cap), 0).astype(F32)
    sub_l = lax.broadcasted_iota(jnp.int32, (LANES, cap), 0).astype(F32)

    def per_expert(e, c):
        cl = cs_scr[e]
        af = aff_ref[0, e]
        rowtot = cl[:, LANES - 1:LANES]
        before = rowtot + rs_scr[e][:, 0:1] <= jrow
        rj = jnp.sum(jnp.where(before, 1.0, 0.0), axis=0, keepdims=True)
        rowbase = jnp.sum(jnp.where(before, rowtot, 0.0), axis=0, keepdims=True)
        onehot = jnp.where(sub_r == rj, 1.0, 0.0).astype(BF16)
        a1 = af.astype(BF16)
        r1 = af - a1.astype(F32)
        a2 = r1.astype(BF16)
        a3 = (r1 - a2.astype(F32)).astype(BF16)
        lhs = jnp.concatenate([cl.astype(BF16), a1, a2, a3], axis=1)
        gathered = lax.dot_general(lhs, onehot, (((0,), (0,)), ((), ())), preferred_element_type=F32)
        g = gathered[:LANES]
        ga = gathered[LANES:2 * LANES] + gathered[2 * LANES:3 * LANES] + gathered[3 * LANES:]
        lanepos = jnp.sum(jnp.where(g <= jrow - rowbase, 1.0, 0.0), axis=0, keepdims=True)
        gate = jnp.sum(jnp.where(sub_l == lanepos, ga, 0.0), axis=0, keepdims=True)
        gate_ref[0, pl.ds(e, 1), :] = gate
        return c

    lax.fori_loop(0, ne, per_expert, 0)


def _topk_route(aff4, cap):
    bsz, ne, rows, _ = aff4.shape
    blk4 = pl.BlockSpec((1, ne, rows, LANES), lambda b: (b, 0, 0, 0))
    lst = pl.BlockSpec((1, ne, cap), lambda b: (b, 0, 0))
    return pl.pallas_call(
        functools.partial(_topk_body, cap=cap),
        grid=(bsz,),
        in_specs=[blk4],
        out_specs=[lst, blk4, blk4],
        out_shape=[jax.ShapeDtypeStruct((bsz, ne, cap), F32),
                   jax.ShapeDtypeStruct((bsz, ne, rows, LANES), jnp.int32),
                   jax.ShapeDtypeStruct((bsz, ne, rows, LANES), jnp.int32)],
        scratch_shapes=[pltpu.VMEM((ne, rows, LANES), F32), pltpu.VMEM((ne, rows, LANES), F32)],
        compiler_params=_params("arbitrary"),
        name="topk_route",
    )(aff4)


PACK_TOK = 256
PACK_SLAB = 64
PACK_SHIFT = 6


def _pack_body(rs_ref, pos_ref, v_ref, xs_hbm, stage, stage_x, pend, sems, sem_x, *, cap):
    b = pl.program_id(0)
    s = pl.program_id(1)
    nb = pl.num_programs(0)
    ns = pl.num_programs(1)
    ne = pos_ref.shape[2]
    tpr = PACK_TOK // LANES
    nr = ns * tpr
    step = b * ns + s
    slot = step % 2
    slab = PACK_SLAB

    @pl.when(s == 0)
    def _():
        pend[...] = jnp.zeros(pend.shape, BF16)

    def first_pos(rr, e):
        return rs_ref[(b * ne + e) * (nr + 1) + rr]

    def floor16(p):
        return lax.shift_left(lax.shift_right_logical(p, 4), 4)

    p0 = [first_pos(s * tpr, e) for e in range(ne)]
    p1 = [first_pos(s * tpr + tpr, e) for e in range(ne)]
    a0 = [floor16(p0[e]) for e in range(ne)]
    a1 = [floor16(p1[e]) for e in range(ne)]
    has = [a1[e] < p1[e] for e in range(ne)]
    prnd = [lax.shift_right_logical(a1[e] - a0[e], PACK_SHIFT) for e in range(ne)]
    poff = [pl.multiple_of(jnp.bitwise_and(a1[e] - a0[e], slab - 1), BF16_ROWS) for e in range(ne)]

    vb = v_ref[0]
    pos = [jnp.concatenate([pos_ref[0, j, e:e + 1, :] for j in range(tpr)], axis=1) for e in range(ne)]
    kiota = lax.broadcasted_iota(jnp.int32, (slab, PACK_TOK), 0)

    def onehot(rnd):
        blocks = [jnp.where(kiota == pos[e] - (a0[e] + rnd * slab), 1.0, 0.0) for e in range(ne)]
        return jnp.concatenate(blocks, axis=0).astype(BF16)

    def keep_partial(buf, e, rnd, old):
        grp = buf[pl.ds(e * slab + poff[e], BF16_ROWS), :]
        return jnp.where(jnp.logical_and(has[e], prnd[e] == rnd), grp, old)

    cur = stage.at[slot]
    cur[...] = jnp.dot(onehot(0), vb, preferred_element_type=F32).astype(BF16)
    for e in range(ne):
        cur[e * slab:e * slab + BF16_ROWS, :] += pend[e]
    for e in range(ne):
        pend[e] = keep_partial(cur, e, 0, jnp.zeros((BF16_ROWS, vb.shape[1]), BF16))

    def slab_bytes_wait(buf, sem):
        pltpu.make_async_copy(buf, buf, sem).wait()

    @pl.when(step > 0)
    def _():
        slab_bytes_wait(stage.at[1 - slot], sems.at[1 - slot])

    @pl.when(s == 0)
    def _():
        stage_x[0:slab, :] = jnp.zeros((slab, stage_x.shape[1]), BF16)
        for e in range(ne):
            cp = pltpu.make_async_copy(stage_x.at[0:slab], xs_hbm.at[e, b, pl.ds(cap, slab)], sem_x)
            cp.start()
            cp.wait()

    for e in range(ne):
        pltpu.make_async_copy(stage.at[slot, e * slab:(e + 1) * slab],
                              xs_hbm.at[e, b, pl.ds(pl.multiple_of(a0[e], BF16_ROWS), slab)], sems.at[slot]).start()

    span = p1[0] - a0[0]
    for e in range(1, ne):
        span = jnp.maximum(span, p1[e] - a0[e])
    rounds = jnp.maximum(1, lax.shift_right_logical(span + (slab - 1), PACK_SHIFT))

    def extra(rnd, carry):
        stage_x[...] = jnp.dot(onehot(rnd), vb, preferred_element_type=F32).astype(BF16)
        for e in range(ne):
            pend[e] = keep_partial(stage_x, e, rnd, pend[e])
        for e in range(ne):
            @pl.when(p1[e] - a0[e] > rnd * slab)
            def _(e=e):
                cp = pltpu.make_async_copy(
                    stage_x.at[e * slab:(e + 1) * slab],
                    xs_hbm.at[e, b, pl.ds(pl.multiple_of(a0[e] + rnd * slab, BF16_ROWS), slab)], sem_x)
                cp.start()
                cp.wait()
        return carry

    lax.fori_loop(1, rounds, extra, 0)

    @pl.when(step == nb * ns - 1)
    def _():
        slab_bytes_wait(stage.at[slot], sems.at[slot])


def _pack_tokens(rs_ext, pos_r, v, cap):
    bsz, n, d = v.shape
    ne = pos_r.shape[2]
    tpr = PACK_TOK // LANES
    return pl.pallas_call(
        functools.partial(_pack_body, cap=cap),
        grid_spec=pltpu.PrefetchScalarGridSpec(
            num_scalar_prefetch=1,
            grid=(bsz, n // PACK_TOK),
            in_specs=[pl.BlockSpec((1, tpr, ne, LANES), lambda b, s, rs: (b, s, 0, 0)),
                      pl.BlockSpec((1, PACK_TOK, d), lambda b, s, rs: (b, s, 0))],
            out_specs=pl.BlockSpec(memory_space=pl.ANY),
            scratch_shapes=[pltpu.VMEM((2, ne * PACK_SLAB, d), BF16),
                            pltpu.VMEM((ne * PACK_SLAB, d), BF16),
                            pltpu.VMEM((ne, BF16_ROWS, d), BF16),
                            pltpu.SemaphoreType.DMA((2,)),
                            pltpu.SemaphoreType.DMA(())]),
        out_shape=jax.ShapeDtypeStruct((ne, bsz, cap + PACK_SLAB, d), BF16),
        compiler_params=_params("arbitrary", "arbitrary"),
        name="pack_tokens",
    )(rs_ext, pos_r, v)


def _ffn_body(x_ref, gate_ref, wg_ref, wu_ref, wd_ref, ys_hbm, acc, stage, osems, *, mg, nf, sub, och):
    e = pl.program_id(0)
    f = pl.program_id(1)
    spg = x_ref.shape[2]

    @pl.when(f == 0)
    def _():
        acc[...] = jnp.zeros(acc.shape, F32)

    wgb = wg_ref[0].astype(BF16)
    wub = wu_ref[0].astype(BF16)
    wdb = wd_ref[0].astype(BF16)
    for i in range(mg // sub):
        rows = slice(i * sub, (i + 1) * sub)
        x = x_ref[0, i * (sub // spg):(i + 1) * (sub // spg)].reshape(sub, x_ref.shape[3])
        gg = jnp.dot(x, wgb, preferred_element_type=F32)
        uu = jnp.dot(x, wub, preferred_element_type=F32)
        h = (gg * jax.nn.sigmoid(gg) * uu).astype(BF16)
        acc[rows, :] += jnp.dot(h, wdb, preferred_element_type=F32)

    @pl.when(f == nf - 1)
    def _():
        gate_t = gate_ref[0].T

        def out_copy(c):
            return pltpu.make_async_copy(stage.at[c % 2], ys_hbm.at[e, pl.ds(c * och, och)], osems.at[c % 2])

        nchunks = mg // och
        for c in range(nchunks):
            if c >= 2:
                out_copy(c - 2).wait()
            for i in range(och // LANES):
                blk = c * (och // LANES) + i
                rows = slice(blk * LANES, (blk + 1) * LANES)
                stage[c % 2, i * LANES:(i + 1) * LANES, :] = (acc[rows, :] * gate_t[:, blk:blk + 1]).astype(BF16)
            out_copy(c).start()
        for c in range(max(nchunks - 2, 0), nchunks):
            out_copy(c).wait()


def _expert_ffn(xs, gate_rows, wg, wu, wd, tf):
    ne, d, fdim = wg.shape
    bsz, cap = xs.shape[1], gate_rows.shape[1] * LANES // xs.shape[1]
    mg = bsz * cap
    nf = fdim // tf
    sub = min(mg, 2 * cap)
    och = min(mg, 512)
    return pl.pallas_call(
        functools.partial(_ffn_body, mg=mg, nf=nf, sub=sub, och=och),
        grid_spec=pltpu.PrefetchScalarGridSpec(
            num_scalar_prefetch=0,
            grid=(ne, nf),
            in_specs=[pl.BlockSpec((1, bsz, cap, d), lambda e, f: (e, 0, 0, 0)),
                      pl.BlockSpec((1, mg // LANES, LANES), lambda e, f: (e, 0, 0)),
                      pl.BlockSpec((1, d, tf), lambda e, f: (e, 0, f)),
                      pl.BlockSpec((1, d, tf), lambda e, f: (e, 0, f)),
                      pl.BlockSpec((1, tf, d), lambda e, f: (e, f, 0))],
            out_specs=pl.BlockSpec(memory_space=pl.ANY),
            scratch_shapes=[pltpu.VMEM((mg, d), F32),
                            pltpu.VMEM((2, och, d), BF16),
                            pltpu.SemaphoreType.DMA((2,))]),
        out_shape=jax.ShapeDtypeStruct((ne, mg, d), BF16),
        compiler_params=_params("arbitrary", "arbitrary"),
        name="expert_ffn",
    )(xs, gate_rows, wg, wu, wd)


SLAB = 48


def _combine_body(rs_ref, pos_ref, ys_hbm, x1_ref, g5_ref, fng_ref, o_ref, stk, stk_x, sems, sem_x, *, cap, tps):
    b = pl.program_id(0)
    s = pl.program_id(1)
    nb = pl.num_programs(0)
    ns = pl.num_programs(1)
    ne = pos_ref.shape[2]
    nr = ns * tps
    step = b * ns + s
    slot = step % 2
    tile_rows = ne * SLAB

    def first_pos(bb, rr, e):
        return rs_ref[(bb * ne + e) * (nr + 1) + rr]

    def aligned(p0):
        return lax.shift_left(lax.shift_right_logical(p0, 4), 4)

    def slab_start(a0, rnd):
        return pl.multiple_of(jnp.minimum(a0 + rnd * SLAB, cap - SLAB), BF16_ROWS)

    def slab_copy(bb, e, a, dst, row, sem):
        return pltpu.make_async_copy(ys_hbm.at[e, pl.ds(bb * cap + a, SLAB)], dst.at[pl.ds(row, SLAB)], sem)

    def issue(bb, ss, sl):
        for j in range(tps):
            for e in range(ne):
                a = slab_start(aligned(first_pos(bb, ss * tps + j, e)), 0)
                slab_copy(bb, e, a, stk.at[sl], j * tile_rows + e * SLAB, sems.at[sl]).start()

    @pl.when(step == 0)
    def _():
        issue(b, s, slot)

    @pl.when(step + 1 < nb * ns)
    def _():
        nxt = step + 1
        issue(nxt // ns, nxt % ns, 1 - slot)

    kiota = lax.broadcasted_iota(jnp.int32, (SLAB, LANES), 0)
    tn = (((0,), (0,)), ((), ()))
    pltpu.make_async_copy(stk.at[slot], stk.at[slot], sems.at[slot]).wait()

    for j in range(tps):
        r = s * tps + j
        pos = pos_ref[0, j]
        a0 = [aligned(first_pos(b, r, e)) for e in range(ne)]

        def onehot(rnd, pos=pos, a0=a0):
            blocks = []
            for e in range(ne):
                rel = pos[e:e + 1, :] - slab_start(a0[e], rnd)
                fresh = (pos[e:e + 1, :] - a0[e]) >= rnd * SLAB
                blocks.append(jnp.where((kiota == rel) & fresh, 1.0, 0.0))
            return jnp.concatenate(blocks, axis=0).astype(BF16)

        moe = lax.dot_general(onehot(0), stk[slot, j * tile_rows:(j + 1) * tile_rows, :], tn,
                              preferred_element_type=F32)

        span = first_pos(b, r + 1, 0) - a0[0]
        for e in range(1, ne):
            span = jnp.maximum(span, first_pos(b, r + 1, e) - a0[e])
        rounds = jnp.maximum(1, (span + (SLAB - 1)) // SLAB)

        def extra(rnd, m, a0=a0, onehot=onehot):
            for e in range(ne):
                slab_copy(b, e, slab_start(a0[e], rnd), stk_x, e * SLAB, sem_x).start()
            pltpu.make_async_copy(stk_x, stk_x, sem_x).wait()
            return m + lax.dot_general(onehot(rnd), stk_x[...], tn, preferred_element_type=F32)

        moe = lax.fori_loop(1, rounds, extra, moe)

        y = x1_ref[0, j * LANES:(j + 1) * LANES, :] + g5_ref[0] * moe
        ms = jnp.mean(y * y, axis=-1, keepdims=True)
        o_ref[0, j * LANES:(j + 1) * LANES, :] = y * lax.rsqrt(ms + RMS_EPS) * fng_ref[...]


def _combine_norm(rs_ext, pos_r, ys, x1, g5, fng, cap, tps):
    bsz, n, d = x1.shape
    ns = n // (LANES * tps)
    ne = pos_r.shape[2]
    return pl.pallas_call(
        functools.partial(_combine_body, cap=cap, tps=tps),
        grid_spec=pltpu.PrefetchScalarGridSpec(
            num_scalar_prefetch=1,
            grid=(bsz, ns),
            in_specs=[pl.BlockSpec((1, tps, ne, LANES), lambda b, s, rs: (b, s, 0, 0)),
                      pl.BlockSpec(memory_space=pl.ANY),
                      pl.BlockSpec((1, tps * LANES, d), lambda b, s, rs: (b, s, 0)),
                      pl.BlockSpec((1, 1, d), lambda b, s, rs: (b, 0, 0)),
                      pl.BlockSpec((1, d), lambda b, s, rs: (0, 0))],
            out_specs=pl.BlockSpec((1, tps * LANES, d), lambda b, s, rs: (b, s, 0)),
            scratch_shapes=[pltpu.VMEM((2, tps * ne * SLAB, d), BF16),
                            pltpu.VMEM((ne * SLAB, d), BF16),
                            pltpu.SemaphoreType.DMA((2,)),
                            pltpu.SemaphoreType.DMA(())]),
        out_shape=jax.ShapeDtypeStruct((bsz, n, d), F32),
        compiler_params=_params("arbitrary", "arbitrary"),
        name="combine_norm",
    )(rs_ext, pos_r, ys, x1, g5, fng.reshape(1, d))


def _block_diag(w):
    heads, hd, _ = w.shape
    eye = jnp.eye(heads, dtype=w.dtype)
    return (eye[:, None, :, None] * w[:, :, None, :]).reshape(heads * hd, heads * hd)


def _tile(n, pref):
    return pref if n % pref == 0 else n


def kernel(x, c, ctx, c_ctx, norm1_g, norm2_g, ada_w, ada_b, w_in, b_in, conv_dw_w, conv_dw_b, conv_ln_g, conv_ln_b, lru_conv_w, lru_conv_b, lru_wa, lru_ba, lru_wi, lru_bi, lru_lambda, w_out, b_out, router_w, exp_w_gate, exp_w_up, exp_w_down, final_norm_g):
    assert norm1_g.shape[0] == 1
    mod, x1, v, aff = _mixer(x, c, ctx, c_ctx, norm1_g[0], norm2_g[0], ada_w[0], ada_b[0], w_in[0], b_in[0],
                             conv_dw_w[0], conv_dw_b[0], conv_ln_g[0], conv_ln_b[0], lru_conv_w[0],
                             lru_conv_b[0], lru_wa[0], lru_ba[0], lru_wi[0], lru_bi[0], lru_lambda[0],
                             w_out[0], b_out[0], router_w[0])
    return _moe_norm(x1, v, aff, mod[5], exp_w_gate[0], exp_w_up[0], exp_w_down[0], final_norm_g)


def _mixer(x, c, ctx, c_ctx, norm1_g, norm2_g, ada_w, ada_b, w_in, b_in, conv_w, conv_b, ln_g, ln_b,
           lru_cw, lru_cb, lru_wa, lru_ba, lru_wi, lru_bi, lru_lam, w_out, b_out, router_w):
    bsz, n, d = x.shape
    cond8 = jnp.zeros((SUBLANES, d), F32).at[:bsz].set(c).at[bsz].set(c_ctx)
    mods = _ada_mod(cond8, ada_w, ada_b)
    mod = [mods[:bsz, k * d:(k + 1) * d].reshape(bsz, 1, d) for k in range(N_MOD)]
    mod_c = [jnp.broadcast_to(mods[bsz:bsz + 1, k * d:(k + 1) * d].reshape(1, 1, d), (bsz, 1, d))
             for k in range(2)]
    w_in_b = w_in.astype(BF16)
    w_out_b = w_out.astype(BF16)
    ch = lru_cb.shape[1]
    wgate = [jnp.concatenate([_block_diag(lru_wa[dd]), _block_diag(lru_wi[dd])], axis=1).astype(BF16)
             for dd in range(2)]
    bgate = [jnp.concatenate([lru_ba[dd], lru_bi[dd]]) for dd in range(2)]

    def lru(lx, h0, dd, reverse, merge_with=None):
        return _lru_dir(lx, h0, lru_cw[dd], lru_cb[dd], wgate[dd], bgate[dd], lru_lam[dd], reverse,
                        _tile(lx.shape[1], 256), merge_with)

    _, c_lx, _ = _in_proj(ctx, norm1_g, mod_c[0], mod_c[1], w_in_b, b_in, _tile(ctx.shape[1], 256))
    zero_h = jnp.zeros((bsz, 1, ch), F32)
    _, hf0 = lru(c_lx, zero_h, 0, False)
    _, hb0 = lru(c_lx, zero_h, 1, True)

    x_glu, x_lx, x_glg = _in_proj(x, norm1_g, mod[0], mod[1], w_in_b, b_in, _tile(n, 512))
    hb, _ = lru(x_lx, hb0, 1, True)
    yl, _ = lru(x_lx, hf0, 0, False, merge_with=(hb, x_glg))
    conv_r, conv_c = _conv_grid(x_glu, conv_w, conv_b)
    x1, v, aff = _out_proj_route(conv_r, conv_c, ln_g, ln_b, yl, w_out_b, b_out, x, mod[2],
                                 norm2_g, mod[3], mod[4], router_w.T, _tile(n, 512))
    return mod, x1, v, aff


def _moe_norm(x1, v, aff, gate2, wg, wu, wd, final_norm_g):
    bsz, n, d = x1.shape
    ne = aff.shape[1]
    cap = EC_CAPACITY * n // ne
    gate, pos, rstart = _topk_route(aff.reshape(bsz, ne, n // LANES, LANES), cap)
    gate_rows = jnp.swapaxes(gate, 0, 1).reshape(ne, bsz * cap // LANES, LANES)
    rs_ext = jnp.concatenate([rstart[..., 0], jnp.full((bsz, ne, 1), cap, jnp.int32)], axis=-1).reshape(-1)
    pos_r = jnp.swapaxes(pos, 1, 2)
    xs = _pack_tokens(rs_ext, pos_r, v, cap)
    ys = _expert_ffn(xs, gate_rows, wg, wu, wd, 256)
    return _combine_norm(rs_ext, pos_r, ys, x1, gate2, final_norm_g, cap, 4)
```

```python
import functools

import jax
import jax.numpy as jnp
from jax import lax
from jax.experimental import pallas as pl
from jax.experimental.pallas import tpu as pltpu

GRID_W = 64
CONV_WIDTH = 31
CONV_PAD = (CONV_WIDTH - 1) // 2
LRU_CONV_WIDTH = 4
LRU_C = 8.0
N_EXPERTS = 16
EC_CAPACITY = 2
N_MOD = 6
RMS_EPS = 1e-6
LN_EPS = 1e-5

LANES = 128
SUBLANES = 8
BF16_ROWS = 16
VMEM_LIMIT = 56 * 1024 * 1024

F32 = jnp.float32
BF16 = jnp.bfloat16
HIGHEST = lax.Precision.HIGHEST


def _params(*sem):
    return pltpu.CompilerParams(dimension_semantics=sem, vmem_limit_bytes=VMEM_LIMIT)


def _ada_body(c_ref, w_ref, b_ref, o_ref):
    s = c_ref[...]
    s = s * jax.nn.sigmoid(s)
    o_ref[...] = jnp.dot(s, w_ref[...], precision=HIGHEST, preferred_element_type=F32) + b_ref[...]


def _ada_mod(cond8, ada_w, ada_b):
    d, n = ada_w.shape
    tn = n // 4
    return pl.pallas_call(
        _ada_body,
        grid=(n // tn,),
        in_specs=[pl.BlockSpec((SUBLANES, d), lambda j: (0, 0)),
                  pl.BlockSpec((d, tn), lambda j: (0, j)),
                  pl.BlockSpec((1, tn), lambda j: (0, j))],
        out_specs=pl.BlockSpec((SUBLANES, tn), lambda j: (0, j)),
        out_shape=jax.ShapeDtypeStruct((SUBLANES, n), F32),
        compiler_params=_params("arbitrary"),
        name="ada_mod",
    )(cond8, ada_w, ada_b.reshape(1, n))


def _inproj_body(x_ref, g_ref, sh_ref, sc_ref, w_ref, b_ref, glu_ref, lx_ref, glg_ref):
    x = x_ref[0]
    ms = jnp.mean(x * x, axis=-1, keepdims=True)
    y = x * lax.rsqrt(ms + RMS_EPS) * g_ref[...]
    u = y * (1.0 + sc_ref[0]) + sh_ref[0]
    p = jnp.dot(u.astype(BF16), w_ref[...], preferred_element_type=F32) + b_ref[...]
    cc = p.shape[1] // 4
    glu_ref[0] = p[:, :cc] * jax.nn.sigmoid(p[:, cc:2 * cc])
    lx_ref[0] = p[:, 2 * cc:3 * cc]
    glg_ref[0] = jax.nn.gelu(p[:, 3 * cc:])


def _in_proj(x, norm_g, shift, scale, w_bf16, b_in, tm):
    bsz, n, d = x.shape
    n4 = w_bf16.shape[1]
    cc = n4 // 4
    tok = pl.BlockSpec((1, tm, cc), lambda b, i: (b, i, 0))
    return pl.pallas_call(
        _inproj_body,
        grid=(bsz, n // tm),
        in_specs=[pl.BlockSpec((1, tm, d), lambda b, i: (b, i, 0)),
                  pl.BlockSpec((1, d), lambda b, i: (0, 0)),
                  pl.BlockSpec((1, 1, d), lambda b, i: (b, 0, 0)),
                  pl.BlockSpec((1, 1, d), lambda b, i: (b, 0, 0)),
                  pl.BlockSpec((d, n4), lambda b, i: (0, 0)),
                  pl.BlockSpec((1, n4), lambda b, i: (0, 0))],
        out_specs=[tok, tok, tok],
        out_shape=[jax.ShapeDtypeStruct((bsz, n, cc), F32)] * 3,
        compiler_params=_params("arbitrary", "arbitrary"),
        name="in_proj",
    )(x, norm_g.reshape(1, d), shift, scale, w_bf16, b_in.reshape(1, n4))


def _lru_body(*refs, reverse, tl, merge):
    x_ref, h0_ref, cw_ref, cb_ref, wg_ref, bg_ref, lam_ref = refs[:7]
    if merge:
        hother_ref, glg_ref = refs[7:9]
        refs = refs[2:]
    h_ref, hl_ref, a_ref, u_ref, carry_ref, halo_ref = refs[7:]
    i = pl.program_id(1)
    ch = x_ref.shape[2]

    @pl.when(i == 0)
    def _():
        carry_ref[...] = jnp.broadcast_to(h0_ref[0], carry_ref.shape)
        halo_ref[...] = jnp.zeros(halo_ref.shape, F32)

    x = x_ref[0]
    halo = halo_ref[...]
    row8 = lax.broadcasted_iota(jnp.int32, (SUBLANES, ch), 0)
    xc = cb_ref[...] + cw_ref[LRU_CONV_WIDTH - 1:LRU_CONV_WIDTH, :] * x
    for j in range(LRU_CONV_WIDTH - 1):
        s = LRU_CONV_WIDTH - 1 - j
        if not reverse:
            rolled = pltpu.roll(x, s, 0)
            edge = jnp.where(row8 < s, pltpu.roll(halo, s, 0), rolled[0:SUBLANES, :])
            tap = jnp.concatenate([edge, rolled[SUBLANES:, :]], axis=0)
        else:
            rolled = pltpu.roll(x, tl - s, 0)
            edge = jnp.where(row8 >= SUBLANES - s, pltpu.roll(halo, SUBLANES - s, 0), rolled[tl - SUBLANES:, :])
            tap = jnp.concatenate([rolled[:tl - SUBLANES, :], edge], axis=0)
        xc = xc + cw_ref[j:j + 1, :] * tap
    halo_ref[...] = x[0:SUBLANES, :] if reverse else x[tl - SUBLANES:tl, :]

    z = jnp.dot(xc.astype(BF16), wg_ref[...], preferred_element_type=F32) + bg_ref[...]
    r = jax.nn.sigmoid(z[:, :ch])
    ig = jax.nn.sigmoid(z[:, ch:])
    log_a = (-LRU_C) * r * jax.nn.softplus(-lam_ref[...])
    th = jnp.tanh(log_a)
    a_ref[...] = jnp.exp(log_a)
    m2 = -2.0 * th / (1.0 - th)
    u_ref[...] = jnp.where(m2 > 0.0, m2 * lax.rsqrt(m2), 0.0) * (ig * xc)

    ng = tl // SUBLANES
    rowid = lax.broadcasted_iota(jnp.int32, (SUBLANES, ch), 0)

    def group(gi, h):
        g = (ng - 1 - gi) if reverse else gi
        off = pl.multiple_of(g * SUBLANES, SUBLANES)
        a = a_ref[pl.ds(off, SUBLANES), :]
        u = u_ref[pl.ds(off, SUBLANES), :]
        for s in (1, 2, 4):
            if reverse:
                m = rowid < SUBLANES - s
                sh = SUBLANES - s
            else:
                m = rowid >= s
                sh = s
            ap = jnp.where(m, pltpu.roll(a, sh, 0), 1.0)
            up = jnp.where(m, pltpu.roll(u, sh, 0), 0.0)
            u = a * up + u
            a = a * ap
        hh = a * h + u
        if merge:
            u_ref[pl.ds(off, SUBLANES), :] = hh
        else:
            h_ref[0, pl.ds(off, SUBLANES), :] = hh
        edge = hh[0:1, :] if reverse else hh[SUBLANES - 1:SUBLANES, :]
        return jnp.broadcast_to(edge, (SUBLANES, ch))

    h = lax.fori_loop(0, ng, group, carry_ref[...], unroll=8)
    carry_ref[...] = h
    hl_ref[0] = h[0:1, :]
    if merge:
        h_ref[0] = ((u_ref[...] + hother_ref[0]) * glg_ref[0]).astype(BF16)


def _lru_dir(lx, h0, cw, cb, wgate, bgate, lam, reverse, tl, merge_with=None):
    bsz, n, ch = lx.shape
    nc = n // tl
    cmap = (lambda b, i: (b, nc - 1 - i, 0)) if reverse else (lambda b, i: (b, i, 0))
    full = lambda shape: pl.BlockSpec(shape, lambda b, i: (0,) * len(shape))
    merge = merge_with is not None
    extra = list(merge_with) if merge else []
    return pl.pallas_call(
        functools.partial(_lru_body, reverse=reverse, tl=tl, merge=merge),
        grid=(bsz, nc),
        in_specs=[pl.BlockSpec((1, tl, ch), cmap),
                  pl.BlockSpec((1, 1, ch), lambda b, i: (b, 0, 0)),
                  full((LRU_CONV_WIDTH, ch)), full((1, ch)),
                  full((ch, 2 * ch)), full((1, 2 * ch)), full((1, ch))]
                 + [pl.BlockSpec((1, tl, ch), cmap)] * len(extra),
        out_specs=[pl.BlockSpec((1, tl, ch), cmap),
                   pl.BlockSpec((1, 1, ch), lambda b, i: (b, 0, 0))],
        out_shape=[jax.ShapeDtypeStruct((bsz, n, ch), BF16 if merge else F32),
                   jax.ShapeDtypeStruct((bsz, 1, ch), F32)],
        scratch_shapes=[pltpu.VMEM((tl, ch), F32),
                        pltpu.VMEM((tl, ch), F32),
                        pltpu.VMEM((SUBLANES, ch), F32),
                        pltpu.VMEM((SUBLANES, ch), F32)],
        compiler_params=_params("arbitrary", "arbitrary"),
        name="lru_rev" if reverse else "lru_fwd",
    )(lx, h0, cw, cb.reshape(1, ch), wgate, bgate.reshape(1, 2 * ch), lam.reshape(1, ch), *extra)


ROW_STRIDE = GRID_W + 2 * BF16_ROWS


def _conv_row_body(x_ref, w_ref, b_ref, o_ref, pad_ref):
    n = x_ref.shape[1]
    nrows = n // GRID_W
    gap = jnp.zeros((BF16_ROWS, LANES), F32)

    def fill(r, c):
        base = pl.multiple_of(r * ROW_STRIDE, SUBLANES)
        src = pl.multiple_of(r * GRID_W, SUBLANES)
        pad_ref[pl.ds(base, BF16_ROWS), :] = gap
        pad_ref[pl.ds(base + BF16_ROWS, GRID_W), :] = x_ref[0, pl.ds(src, GRID_W), :]
        pad_ref[pl.ds(base + BF16_ROWS + GRID_W, BF16_ROWS), :] = gap
        return c

    lax.fori_loop(0, nrows, fill, 0)

    def row(r, c):
        base = pl.multiple_of(r * ROW_STRIDE, SUBLANES)
        acc = jnp.broadcast_to(b_ref[...], (GRID_W, LANES))
        for k in range(CONV_WIDTH):
            acc = acc + w_ref[k:k + 1, :] * pad_ref[pl.ds(base + BF16_ROWS - CONV_PAD + k, GRID_W), :]
        o_ref[0, pl.ds(pl.multiple_of(r * GRID_W, SUBLANES), GRID_W), :] = acc
        return c

    lax.fori_loop(0, nrows, row, 0, unroll=2)


def _conv_col_body(x_ref, w_ref, b_ref, o_ref, pad_ref):
    n = x_ref.shape[1]
    nrows = n // GRID_W
    halo = CONV_PAD * GRID_W
    pad_ref[0:halo, :] = jnp.zeros((halo, LANES), F32)
    pad_ref[halo + n:halo + n + halo, :] = jnp.zeros((halo, LANES), F32)

    def fill(r, c):
        src = pl.multiple_of(r * GRID_W, SUBLANES)
        pad_ref[pl.ds(halo + src, GRID_W), :] = x_ref[0, pl.ds(src, GRID_W), :]
        return c

    lax.fori_loop(0, nrows, fill, 0)

    def row(r, c):
        base = pl.multiple_of(r * GRID_W, SUBLANES)
        acc = jnp.broadcast_to(b_ref[...], (GRID_W, LANES))
        for k in range(CONV_WIDTH):
            acc = acc + w_ref[k:k + 1, :] * pad_ref[pl.ds(base + k * GRID_W, GRID_W), :]
        o_ref[0, pl.ds(base, GRID_W), :] = acc
        return c

    lax.fori_loop(0, nrows, row, 0, unroll=2)


def _conv_grid(glu, w, b):
    bsz, n, ch = glu.shape
    half = ch // 2
    ng = half // LANES
    outs = []
    for body, first, pad_rows in ((_conv_row_body, 0, (n // GRID_W) * ROW_STRIDE),
                                  (_conv_col_body, ng, n + 2 * CONV_PAD * GRID_W)):
        outs.append(pl.pallas_call(
            body,
            grid=(bsz, ng),
            in_specs=[pl.BlockSpec((1, n, LANES), lambda bb, g, first=first: (bb, 0, g + first)),
                      pl.BlockSpec((CONV_WIDTH, LANES), lambda bb, g, first=first: (0, g + first)),
                      pl.BlockSpec((1, LANES), lambda bb, g, first=first: (0, g + first))],
            out_specs=pl.BlockSpec((1, n, LANES), lambda bb, g: (bb, 0, g)),
            out_shape=jax.ShapeDtypeStruct((bsz, n, half), F32),
            scratch_shapes=[pltpu.VMEM((pad_rows, LANES), F32)],
            compiler_params=_params("arbitrary", "arbitrary"),
            name="conv_row" if first == 0 else "conv_col",
        )(glu, w, b.reshape(1, ch)))
    return outs


def _out_body(cr_ref, cc_ref, lng_ref, lnb_ref, yl_ref, wo_ref, bo_ref, x_ref,
              g1_ref, n2g_ref, sh2_ref, sc2_ref, rw_ref, x1_ref, v_ref, aff_ref):
    cv = jnp.concatenate([cr_ref[0], cc_ref[0]], axis=-1)
    mu = jnp.mean(cv, axis=-1, keepdims=True)
    dv = cv - mu
    var = jnp.mean(dv * dv, axis=-1, keepdims=True)
    yn = dv * lax.rsqrt(var + LN_EPS) * lng_ref[...] + lnb_ref[...]
    cy = yn * jax.nn.sigmoid(yn)
    cat = jnp.concatenate([cy.astype(BF16), yl_ref[0]], axis=-1)
    m = jnp.dot(cat, wo_ref[...], preferred_element_type=F32) + bo_ref[...]
    x1 = x_ref[0] + g1_ref[0] * m
    x1_ref[0] = x1
    ms = jnp.mean(x1 * x1, axis=-1, keepdims=True)
    v = x1 * lax.rsqrt(ms + RMS_EPS) * n2g_ref[...]
    v = v * (1.0 + sc2_ref[0]) + sh2_ref[0]
    v_ref[0] = v.astype(BF16)
    v_hi = v.astype(BF16)
    v_lo = (v - v_hi.astype(F32)).astype(BF16)
    rw = rw_ref[...]
    rw_hi = rw.astype(BF16)
    rw_lo = (rw - rw_hi.astype(F32)).astype(BF16)
    nt = (((1,), (1,)), ((), ()))
    ne = rw.shape[0]
    both = lax.dot_general(jnp.concatenate([rw_hi, rw_lo], axis=0), v_hi, nt, preferred_element_type=F32)
    lg = both[:ne] + both[ne:] + lax.dot_general(rw_hi, v_lo, nt, preferred_element_type=F32)
    ex = jnp.exp(lg - jnp.max(lg, axis=0, keepdims=True))
    aff_ref[0] = ex / jnp.sum(ex, axis=0, keepdims=True)


def _out_proj_route(conv_r, conv_c, ln_g, ln_b, yl, wo_bf16, b_out, x, g1, n2g, sh2, sc2, rw_t, tm):
    bsz, n, d = x.shape
    half = conv_r.shape[2]
    ch = yl.shape[2]
    ne = rw_t.shape[0]
    tok = lambda c: pl.BlockSpec((1, tm, c), lambda b, i: (b, i, 0))
    full = lambda shape: pl.BlockSpec(shape, lambda b, i: (0,) * len(shape))
    per_b = pl.BlockSpec((1, 1, d), lambda b, i: (b, 0, 0))
    return pl.pallas_call(
        _out_body,
        grid=(bsz, n // tm),
        in_specs=[tok(half), tok(half), full((1, 2 * half)), full((1, 2 * half)),
                  tok(ch), full((2 * half + ch, d)), full((1, d)), tok(d),
                  per_b, full((1, d)), per_b, per_b, full((ne, d))],
        out_specs=[tok(d), tok(d), pl.BlockSpec((1, ne, tm), lambda b, i: (b, 0, i))],
        out_shape=[jax.ShapeDtypeStruct((bsz, n, d), F32),
                   jax.ShapeDtypeStruct((bsz, n, d), BF16),
                   jax.ShapeDtypeStruct((bsz, ne, n), F32)],
        compiler_params=_params("arbitrary", "arbitrary"),
        name="out_proj_route",
    )(conv_r, conv_c, ln_g.reshape(1, -1), ln_b.reshape(1, -1), yl, wo_bf16,
      b_out.reshape(1, d), x, g1, n2g.reshape(1, d), sh2, sc2, rw_t)


def _token_cumsum(m, rows_per_expert):
    er = m.shape[0]
    li = lax.broadcasted_iota(jnp.int32, (LANES, LANES), 0)
    lj = lax.broadcasted_iota(jnp.int32, (LANES, LANES), 1)
    upper = jnp.where(li <= lj, 1.0, 0.0).astype(BF16)
    cs = jnp.dot(m.astype(BF16), upper, preferred_element_type=F32)
    rt = jnp.broadcast_to(cs[:, LANES - 1:LANES], (er, LANES)).astype(BF16)
    ri = lax.broadcasted_iota(jnp.int32, (er, er), 0)
    ci = lax.broadcasted_iota(jnp.int32, (er, er), 1)
    same = (ri // rows_per_expert) == (ci // rows_per_expert)
    lower = jnp.where(same & (ci < ri), 1.0, 0.0).astype(BF16)
    rstart = jnp.dot(lower, rt, preferred_element_type=F32)
    return cs, rstart


def _topk_body(aff_ref, gate_ref, pos_ref, rs_ref, cs_scr, rs_scr, *, cap):
    aff = aff_ref[0]
    ne, rows, _ = aff.shape

    def bisect(i, thr):
        cand = thr | jnp.left_shift(jnp.int32(1), 30 - i)
        cnt = jnp.sum((aff >= pltpu.bitcast(cand, F32)).astype(jnp.int32), axis=(1, 2), keepdims=True)
        return jnp.where(cnt >= cap, cand, thr)

    thr = lax.fori_loop(0, 31, bisect, jnp.zeros((ne, 1, 1), jnp.int32))
    gt = aff >= pltpu.bitcast(thr + 1, F32)
    eq = (aff >= pltpu.bitcast(thr, F32)) & jnp.logical_not(gt)
    need = (cap - jnp.sum(gt.astype(jnp.int32), axis=(1, 2), keepdims=True)).astype(F32)
    eqf = jnp.where(eq, 1.0, 0.0)
    rank_in_row, rank_row0 = _token_cumsum(eqf.reshape(ne * rows, LANES), rows)
    rank_excl = (rank_in_row + rank_row0).reshape(ne, rows, LANES) - eqf
    sel = gt | (eq & (rank_excl < need))
    self_ = jnp.where(sel, 1.0, 0.0)
    cs, rstart = _token_cumsum(self_.reshape(ne * rows, LANES), rows)
    cs3 = cs.reshape(ne, rows, LANES)
    rstart3 = rstart.reshape(ne, rows, LANES)
    cs_scr[...] = cs3
    rs_scr[...] = rstart3
    pos_ref[0] = jnp.where(sel, cs3 + rstart3 - 1.0, -1.0).astype(jnp.int32)
    rs_ref[0] = rstart3.astype(jnp.int32)

    jrow = lax.broadcasted_iota(jnp.int32, (1, cap), 1).astype(F32)
    sub_r = lax.broadcasted_iota(jnp.int32, (rows, cap), 0).astype(F32)
    sub_l = lax.broadcasted_iota(jnp.int32, (LANES, cap), 0).astype(F32)

    def per_expert(e, c):
        cl = cs_scr[e]
        af = aff_ref[0, e]
        rowtot = cl[:, LANES - 1:LANES]
        before = rowtot + rs_scr[e][:, 0:1] <= jrow
        rj = jnp.sum(jnp.where(before, 1.0, 0.0), axis=0, keepdims=True)
        rowbase = jnp.sum(jnp.where(before, rowtot, 0.0), axis=0, keepdims=True)
        onehot = jnp.where(sub_r == rj, 1.0, 0.0).astype(BF16)
        a1 = af.astype(BF16)
        r1 = af - a1.astype(F32)
        a2 = r1.astype(BF16)
        a3 = (r1 - a2.astype(F32)).astype(BF16)
        lhs = jnp.concatenate([cl.astype(BF16), a1, a2, a3], axis=1)
        gathered = lax.dot_general(lhs, onehot, (((0,), (0,)), ((), ())), preferred_element_type=F32)
        g = gathered[:LANES]
        ga = gathered[LANES:2 * LANES] + gathered[2 * LANES:3 * LANES] + gathered[3 * LANES:]
        lanepos = jnp.sum(jnp.where(g <= jrow - rowbase, 1.0, 0.0), axis=0, keepdims=True)
        gate = jnp.sum(jnp.where(sub_l == lanepos, ga, 0.0), axis=0, keepdims=True)
        gate_ref[0, pl.ds(e, 1), :] = gate
        return c

    lax.fori_loop(0, ne, per_expert, 0)


def _topk_route(aff4, cap):
    bsz, ne, rows, _ = aff4.shape
    blk4 = pl.BlockSpec((1, ne, rows, LANES), lambda b: (b, 0, 0, 0))
    lst = pl.BlockSpec((1, ne, cap), lambda b: (b, 0, 0))
    return pl.pallas_call(
        functools.partial(_topk_body, cap=cap),
        grid=(bsz,),
        in_specs=[blk4],
        out_specs=[lst, blk4, blk4],
        out_shape=[jax.ShapeDtypeStruct((bsz, ne, cap), F32),
                   jax.ShapeDtypeStruct((bsz, ne, rows, LANES), jnp.int32),
                   jax.ShapeDtypeStruct((bsz, ne, rows, LANES), jnp.int32)],
        scratch_shapes=[pltpu.VMEM((ne, rows, LANES), F32), pltpu.VMEM((ne, rows, LANES), F32)],
        compiler_params=_params("arbitrary"),
        name="topk_route",
    )(aff4)


PACK_TOK = 256
PACK_SLAB = 64
PACK_SHIFT = 6


def _pack_body(rs_ref, pos_ref, v_ref, xs_hbm, stage, stage_x, pend, sems, sem_x, *, cap):
    b = pl.program_id(0)
    s = pl.program_id(1)
    nb = pl.num_programs(0)
    ns = pl.num_programs(1)
    ne = pos_ref.shape[2]
    tpr = PACK_TOK // LANES
    nr = ns * tpr
    step = b * ns + s
    slot = step % 2
    slab = PACK_SLAB

    @pl.when(s == 0)
    def _():
        pend[...] = jnp.zeros(pend.shape, BF16)

    def first_pos(rr, e):
        return rs_ref[(b * ne + e) * (nr + 1) + rr]

    def floor16(p):
        return lax.shift_left(lax.shift_right_logical(p, 4), 4)

    p0 = [first_pos(s * tpr, e) for e in range(ne)]
    p1 = [first_pos(s * tpr + tpr, e) for e in range(ne)]
    a0 = [floor16(p0[e]) for e in range(ne)]
    a1 = [floor16(p1[e]) for e in range(ne)]
    has = [a1[e] < p1[e] for e in range(ne)]
    prnd = [lax.shift_right_logical(a1[e] - a0[e], PACK_SHIFT) for e in range(ne)]
    poff = [pl.multiple_of(jnp.bitwise_and(a1[e] - a0[e], slab - 1), BF16_ROWS) for e in range(ne)]

    vb = v_ref[0]
    pos = [jnp.concatenate([pos_ref[0, j, e:e + 1, :] for j in range(tpr)], axis=1) for e in range(ne)]
    kiota = lax.broadcasted_iota(jnp.int32, (slab, PACK_TOK), 0)

    def onehot(rnd):
        blocks = [jnp.where(kiota == pos[e] - (a0[e] + rnd * slab), 1.0, 0.0) for e in range(ne)]
        return jnp.concatenate(blocks, axis=0).astype(BF16)

    def keep_partial(buf, e, rnd, old):
        grp = buf[pl.ds(e * slab + poff[e], BF16_ROWS), :]
        return jnp.where(jnp.logical_and(has[e], prnd[e] == rnd), grp, old)

    cur = stage.at[slot]
    cur[...] = jnp.dot(onehot(0), vb, preferred_element_type=F32).astype(BF16)
    for e in range(ne):
        cur[e * slab:e * slab + BF16_ROWS, :] += pend[e]
    for e in range(ne):
        pend[e] = keep_partial(cur, e, 0, jnp.zeros((BF16_ROWS, vb.shape[1]), BF16))

    def slab_bytes_wait(buf, sem):
        pltpu.make_async_copy(buf, buf, sem).wait()

    @pl.when(step > 0)
    def _():
        slab_bytes_wait(stage.at[1 - slot], sems.at[1 - slot])

    @pl.when(s == 0)
    def _():
        stage_x[0:slab, :] = jnp.zeros((slab, stage_x.shape[1]), BF16)
        for e in range(ne):
            pltpu.make_async_copy(stage_x.at[0:slab], xs_hbm.at[e, b, pl.ds(cap, slab)], sem_x).start()
        slab_bytes_wait(stage_x, sem_x)

    for e in range(ne):
        pltpu.make_async_copy(stage.at[slot, e * slab:(e + 1) * slab],
                              xs_hbm.at[e, b, pl.ds(pl.multiple_of(a0[e], BF16_ROWS), slab)], sems.at[slot]).start()

    span = p1[0] - a0[0]
    for e in range(1, ne):
        span = jnp.maximum(span, p1[e] - a0[e])
    rounds = jnp.maximum(1, lax.shift_right_logical(span + (slab - 1), PACK_SHIFT))

    def extra(rnd, carry):
        stage_x[...] = jnp.dot(onehot(rnd), vb, preferred_element_type=F32).astype(BF16)
        for e in range(ne):
            pend[e] = keep_partial(stage_x, e, rnd, pend[e])
        for e in range(ne):
            @pl.when(p1[e] - a0[e] > rnd * slab)
            def _(e=e):
                cp = pltpu.make_async_copy(
                    stage_x.at[e * slab:(e + 1) * slab],
                    xs_hbm.at[e, b, pl.ds(pl.multiple_of(a0[e] + rnd * slab, BF16_ROWS), slab)], sem_x)
                cp.start()
                cp.wait()
        return carry

    lax.fori_loop(1, rounds, extra, 0)

    @pl.when(step == nb * ns - 1)
    def _():
        slab_bytes_wait(stage.at[slot], sems.at[slot])


def _pack_tokens(rs_ext, pos_r, v, cap):
    bsz, n, d = v.shape
    ne = pos_r.shape[2]
    tpr = PACK_TOK // LANES
    return pl.pallas_call(
        functools.partial(_pack_body, cap=cap),
        grid_spec=pltpu.PrefetchScalarGridSpec(
            num_scalar_prefetch=1,
            grid=(bsz, n // PACK_TOK),
            in_specs=[pl.BlockSpec((1, tpr, ne, LANES), lambda b, s, rs: (b, s, 0, 0)),
                      pl.BlockSpec((1, PACK_TOK, d), lambda b, s, rs: (b, s, 0))],
            out_specs=pl.BlockSpec(memory_space=pl.ANY),
            scratch_shapes=[pltpu.VMEM((2, ne * PACK_SLAB, d), BF16),
                            pltpu.VMEM((ne * PACK_SLAB, d), BF16),
                            pltpu.VMEM((ne, BF16_ROWS, d), BF16),
                            pltpu.SemaphoreType.DMA((2,)),
                            pltpu.SemaphoreType.DMA(())]),
        out_shape=jax.ShapeDtypeStruct((ne, bsz, cap + PACK_SLAB, d), BF16),
        compiler_params=_params("arbitrary", "arbitrary"),
        name="pack_tokens",
    )(rs_ext, pos_r, v)


def _ffn_body(x_ref, gate_ref, wg_ref, wu_ref, wd_ref, ys_hbm, acc, stage, osems, *, mg, nf, sub, och):
    e = pl.program_id(0)
    f = pl.program_id(1)
    spg = x_ref.shape[2]

    @pl.when(f == 0)
    def _():
        acc[...] = jnp.zeros(acc.shape, F32)

    wgb = wg_ref[0].astype(BF16)
    wub = wu_ref[0].astype(BF16)
    wdb = wd_ref[0].astype(BF16)
    for i in range(mg // sub):
        rows = slice(i * sub, (i + 1) * sub)
        x = x_ref[0, i * (sub // spg):(i + 1) * (sub // spg)].reshape(sub, x_ref.shape[3])
        gg = jnp.dot(x, wgb, preferred_element_type=F32)
        uu = jnp.dot(x, wub, preferred_element_type=F32)
        h = (gg * jax.nn.sigmoid(gg) * uu).astype(BF16)
        acc[rows, :] += jnp.dot(h, wdb, preferred_element_type=F32)

    @pl.when(f == nf - 1)
    def _():
        gate_t = gate_ref[0].T

        def out_copy(c):
            return pltpu.make_async_copy(stage.at[c % 2], ys_hbm.at[e, pl.ds(c * och, och)], osems.at[c % 2])

        nchunks = mg // och
        for c in range(nchunks):
            if c >= 2:
                out_copy(c - 2).wait()
            for i in range(och // LANES):
                blk = c * (och // LANES) + i
                rows = slice(blk * LANES, (blk + 1) * LANES)
                stage[c % 2, i * LANES:(i + 1) * LANES, :] = (acc[rows, :] * gate_t[:, blk:blk + 1]).astype(BF16)
            out_copy(c).start()
        for c in range(max(nchunks - 2, 0), nchunks):
            out_copy(c).wait()


def _expert_ffn(xs, gate_rows, wg, wu, wd, tf):
    ne, d, fdim = wg.shape
    bsz, cap = xs.shape[1], gate_rows.shape[1] * LANES // xs.shape[1]
    mg = bsz * cap
    nf = fdim // tf
    sub = min(mg, 2 * cap)
    och = min(mg, 512)
    return pl.pallas_call(
        functools.partial(_ffn_body, mg=mg, nf=nf, sub=sub, och=och),
        grid_spec=pltpu.PrefetchScalarGridSpec(
            num_scalar_prefetch=0,
            grid=(ne, nf),
            in_specs=[pl.BlockSpec((1, bsz, cap, d), lambda e, f: (e, 0, 0, 0)),
                      pl.BlockSpec((1, mg // LANES, LANES), lambda e, f: (e, 0, 0)),
                      pl.BlockSpec((1, d, tf), lambda e, f: (e, 0, f)),
                      pl.BlockSpec((1, d, tf), lambda e, f: (e, 0, f)),
                      pl.BlockSpec((1, tf, d), lambda e, f: (e, f, 0))],
            out_specs=pl.BlockSpec(memory_space=pl.ANY),
            scratch_shapes=[pltpu.VMEM((mg, d), F32),
                            pltpu.VMEM((2, och, d), BF16),
                            pltpu.SemaphoreType.DMA((2,))]),
        out_shape=jax.ShapeDtypeStruct((ne, mg, d), BF16),
        compiler_params=_params("arbitrary", "arbitrary"),
        name="expert_ffn",
    )(xs, gate_rows, wg, wu, wd)


SLAB = PACK_SLAB
TILE_ROWS = PACK_TOK // LANES


def _combine_body(rs_ref, pos_ref, ys_hbm, x1_ref, g5_ref, fng_ref, o_ref, stk, stk_x, sems, sem_x, *, cap, tps):
    b = pl.program_id(0)
    s = pl.program_id(1)
    nb = pl.num_programs(0)
    ns = pl.num_programs(1)
    ne = pos_ref.shape[2]
    nr = ns * tps * TILE_ROWS
    tok = TILE_ROWS * LANES
    step = b * ns + s
    slot = step % 2
    tile_rows = ne * SLAB

    def first_pos(bb, rr, e):
        return rs_ref[(bb * ne + e) * (nr + 1) + rr]

    def aligned(p0):
        return lax.shift_left(lax.shift_right_logical(p0, 4), 4)

    def slab_start(a0, rnd):
        return pl.multiple_of(jnp.minimum(a0 + rnd * SLAB, cap - SLAB), BF16_ROWS)

    def slab_copy(bb, e, a, dst, row, sem):
        return pltpu.make_async_copy(ys_hbm.at[e, pl.ds(bb * cap + a, SLAB)], dst.at[pl.ds(row, SLAB)], sem)

    def issue(bb, ss, sl):
        for j in range(tps):
            for e in range(ne):
                a = slab_start(aligned(first_pos(bb, (ss * tps + j) * TILE_ROWS, e)), 0)
                slab_copy(bb, e, a, stk.at[sl], j * tile_rows + e * SLAB, sems.at[sl]).start()

    @pl.when(step == 0)
    def _():
        issue(b, s, slot)

    @pl.when(step + 1 < nb * ns)
    def _():
        nxt = step + 1
        issue(nxt // ns, nxt % ns, 1 - slot)

    kiota = lax.broadcasted_iota(jnp.int32, (SLAB, tok), 0)
    tn = (((0,), (0,)), ((), ()))
    pltpu.make_async_copy(stk.at[slot], stk.at[slot], sems.at[slot]).wait()

    for j in range(tps):
        r = (s * tps + j) * TILE_ROWS
        pos = jnp.concatenate([pos_ref[0, j * TILE_ROWS + q] for q in range(TILE_ROWS)], axis=1)
        a0 = [aligned(first_pos(b, r, e)) for e in range(ne)]

        def onehot(rnd, pos=pos, a0=a0):
            blocks = []
            for e in range(ne):
                rel = pos[e:e + 1, :] - slab_start(a0[e], rnd)
                fresh = (pos[e:e + 1, :] - a0[e]) >= rnd * SLAB
                blocks.append(jnp.where((kiota == rel) & fresh, 1.0, 0.0))
            return jnp.concatenate(blocks, axis=0).astype(BF16)

        moe = lax.dot_general(onehot(0), stk[slot, j * tile_rows:(j + 1) * tile_rows, :], tn,
                              preferred_element_type=F32)

        span = first_pos(b, r + TILE_ROWS, 0) - a0[0]
        for e in range(1, ne):
            span = jnp.maximum(span, first_pos(b, r + TILE_ROWS, e) - a0[e])
        rounds = jnp.maximum(1, lax.shift_right_logical(span + (SLAB - 1), PACK_SHIFT))

        def extra(rnd, m, a0=a0, onehot=onehot):
            for e in range(ne):
                slab_copy(b, e, slab_start(a0[e], rnd), stk_x, e * SLAB, sem_x).start()
            pltpu.make_async_copy(stk_x, stk_x, sem_x).wait()
            return m + lax.dot_general(onehot(rnd), stk_x[...], tn, preferred_element_type=F32)

        moe = lax.fori_loop(1, rounds, extra, moe)

        y = x1_ref[0, j * tok:(j + 1) * tok, :] + g5_ref[0] * moe
        ms = jnp.mean(y * y, axis=-1, keepdims=True)
        o_ref[0, j * tok:(j + 1) * tok, :] = y * lax.rsqrt(ms + RMS_EPS) * fng_ref[...]


def _combine_norm(rs_ext, pos_r, ys, x1, g5, fng, cap, tps):
    bsz, n, d = x1.shape
    ns = n // (PACK_TOK * tps)
    ne = pos_r.shape[2]
    return pl.pallas_call(
        functools.partial(_combine_body, cap=cap, tps=tps),
        grid_spec=pltpu.PrefetchScalarGridSpec(
            num_scalar_prefetch=1,
            grid=(bsz, ns),
            in_specs=[pl.BlockSpec((1, tps * TILE_ROWS, ne, LANES), lambda b, s, rs: (b, s, 0, 0)),
                      pl.BlockSpec(memory_space=pl.ANY),
                      pl.BlockSpec((1, tps * PACK_TOK, d), lambda b, s, rs: (b, s, 0)),
                      pl.BlockSpec((1, 1, d), lambda b, s, rs: (b, 0, 0)),
                      pl.BlockSpec((1, d), lambda b, s, rs: (0, 0))],
            out_specs=pl.BlockSpec((1, tps * PACK_TOK, d), lambda b, s, rs: (b, s, 0)),
            scratch_shapes=[pltpu.VMEM((2, tps * ne * SLAB, d), BF16),
                            pltpu.VMEM((ne * SLAB, d), BF16),
                            pltpu.SemaphoreType.DMA((2,)),
                            pltpu.SemaphoreType.DMA(())]),
        out_shape=jax.ShapeDtypeStruct((bsz, n, d), F32),
        compiler_params=_params("arbitrary", "arbitrary"),
        name="combine_norm",
    )(rs_ext, pos_r, ys, x1, g5, fng.reshape(1, d))


def _block_diag(w):
    heads, hd, _ = w.shape
    eye = jnp.eye(heads, dtype=w.dtype)
    return (eye[:, None, :, None] * w[:, :, None, :]).reshape(heads * hd, heads * hd)


def _tile(n, pref):
    return pref if n % pref == 0 else n


def kernel(x, c, ctx, c_ctx, norm1_g, norm2_g, ada_w, ada_b, w_in, b_in, conv_dw_w, conv_dw_b, conv_ln_g, conv_ln_b, lru_conv_w, lru_conv_b, lru_wa, lru_ba, lru_wi, lru_bi, lru_lambda, w_out, b_out, router_w, exp_w_gate, exp_w_up, exp_w_down, final_norm_g):
    assert norm1_g.shape[0] == 1
    mod, x1, v, aff = _mixer(x, c, ctx, c_ctx, norm1_g[0], norm2_g[0], ada_w[0], ada_b[0], w_in[0], b_in[0],
                             conv_dw_w[0], conv_dw_b[0], conv_ln_g[0], conv_ln_b[0], lru_conv_w[0],
                             lru_conv_b[0], lru_wa[0], lru_ba[0], lru_wi[0], lru_bi[0], lru_lambda[0],
                             w_out[0], b_out[0], router_w[0])
    return _moe_norm(x1, v, aff, mod[5], exp_w_gate[0], exp_w_up[0], exp_w_down[0], final_norm_g)


def _mixer(x, c, ctx, c_ctx, norm1_g, norm2_g, ada_w, ada_b, w_in, b_in, conv_w, conv_b, ln_g, ln_b,
           lru_cw, lru_cb, lru_wa, lru_ba, lru_wi, lru_bi, lru_lam, w_out, b_out, router_w):
    bsz, n, d = x.shape
    cond8 = jnp.zeros((SUBLANES, d), F32).at[:bsz].set(c).at[bsz].set(c_ctx)
    mods = _ada_mod(cond8, ada_w, ada_b)
    mod = [mods[:bsz, k * d:(k + 1) * d].reshape(bsz, 1, d) for k in range(N_MOD)]
    mod_c = [jnp.broadcast_to(mods[bsz:bsz + 1, k * d:(k + 1) * d].reshape(1, 1, d), (bsz, 1, d))
             for k in range(2)]
    w_in_b = w_in.astype(BF16)
    w_out_b = w_out.astype(BF16)
    ch = lru_cb.shape[1]
    wgate = [jnp.concatenate([_block_diag(lru_wa[dd]), _block_diag(lru_wi[dd])], axis=1).astype(BF16)
             for dd in range(2)]
    bgate = [jnp.concatenate([lru_ba[dd], lru_bi[dd]]) for dd in range(2)]

    def lru(lx, h0, dd, reverse, merge_with=None):
        return _lru_dir(lx, h0, lru_cw[dd], lru_cb[dd], wgate[dd], bgate[dd], lru_lam[dd], reverse,
                        _tile(lx.shape[1], 256), merge_with)

    _, c_lx, _ = _in_proj(ctx, norm1_g, mod_c[0], mod_c[1], w_in_b, b_in, _tile(ctx.shape[1], 256))
    zero_h = jnp.zeros((bsz, 1, ch), F32)
    _, hf0 = lru(c_lx, zero_h, 0, False)
    _, hb0 = lru(c_lx, zero_h, 1, True)

    x_glu, x_lx, x_glg = _in_proj(x, norm1_g, mod[0], mod[1], w_in_b, b_in, _tile(n, 512))
    hb, _ = lru(x_lx, hb0, 1, True)
    yl, _ = lru(x_lx, hf0, 0, False, merge_with=(hb, x_glg))
    conv_r, conv_c = _conv_grid(x_glu, conv_w, conv_b)
    x1, v, aff = _out_proj_route(conv_r, conv_c, ln_g, ln_b, yl, w_out_b, b_out, x, mod[2],
                                 norm2_g, mod[3], mod[4], router_w.T, _tile(n, 512))
    return mod, x1, v, aff


def _moe_norm(x1, v, aff, gate2, wg, wu, wd, final_norm_g):
    bsz, n, d = x1.shape
    ne = aff.shape[1]
    cap = EC_CAPACITY * n // ne
    gate, pos, rstart = _topk_route(aff.reshape(bsz, ne, n // LANES, LANES), cap)
    gate_rows = jnp.swapaxes(gate, 0, 1).reshape(ne, bsz * cap // LANES, LANES)
    rs_ext = jnp.concatenate([rstart[..., 0], jnp.full((bsz, ne, 1), cap, jnp.int32)], axis=-1).reshape(-1)
    pos_r = jnp.swapaxes(pos, 1, 2)
    xs = _pack_tokens(rs_ext, pos_r, v, cap)
    ys = _expert_ffn(xs, gate_rows, wg, wu, wd, 256)
    return _combine_norm(rs_ext, pos_r, ys, x1, gate2, final_norm_g, cap, 2)
```

```python
import functools

import jax
import jax.numpy as jnp
from jax import lax
from jax.experimental import pallas as pl
from jax.experimental.pallas import tpu as pltpu

GRID_W = 64
CONV_WIDTH = 31
CONV_PAD = (CONV_WIDTH - 1) // 2
LRU_CONV_WIDTH = 4
LRU_C = 8.0
N_EXPERTS = 16
EC_CAPACITY = 2
N_MOD = 6
RMS_EPS = 1e-6
LN_EPS = 1e-5

LANES = 128
SUBLANES = 8
BF16_ROWS = 16
VMEM_LIMIT = 56 * 1024 * 1024

F32 = jnp.float32
BF16 = jnp.bfloat16
HIGHEST = lax.Precision.HIGHEST


def _params(*sem):
    return pltpu.CompilerParams(dimension_semantics=sem, vmem_limit_bytes=VMEM_LIMIT)


def _ada_body(c_ref, w_ref, b_ref, o_ref):
    s = c_ref[...]
    s = s * jax.nn.sigmoid(s)
    o_ref[...] = jnp.dot(s, w_ref[...], precision=HIGHEST, preferred_element_type=F32) + b_ref[...]


def _ada_mod(cond8, ada_w, ada_b):
    d, n = ada_w.shape
    tn = n // 4
    return pl.pallas_call(
        _ada_body,
        grid=(n // tn,),
        in_specs=[pl.BlockSpec((SUBLANES, d), lambda j: (0, 0)),
                  pl.BlockSpec((d, tn), lambda j: (0, j)),
                  pl.BlockSpec((1, tn), lambda j: (0, j))],
        out_specs=pl.BlockSpec((SUBLANES, tn), lambda j: (0, j)),
        out_shape=jax.ShapeDtypeStruct((SUBLANES, n), F32),
        compiler_params=_params("arbitrary"),
        name="ada_mod",
    )(cond8, ada_w, ada_b.reshape(1, n))


def _inproj_body(x_ref, g_ref, sh_ref, sc_ref, w_ref, b_ref, glu_ref, lx_ref, glg_ref, *, rb):
    gain = g_ref[...] * (1.0 + sc_ref[0])
    for i in range(x_ref.shape[1] // rb):
        rows = slice(i * rb, (i + 1) * rb)
        x = x_ref[0, rows, :]
        ms = jnp.mean(x * x, axis=-1, keepdims=True)
        u = x * lax.rsqrt(ms + RMS_EPS) * gain + sh_ref[0]
        p = jnp.dot(u.astype(BF16), w_ref[...], preferred_element_type=F32) + b_ref[...]
        cc = p.shape[1] // 4
        glu_ref[0, rows, :] = p[:, :cc] * jax.nn.sigmoid(p[:, cc:2 * cc])
        lx_ref[0, rows, :] = p[:, 2 * cc:3 * cc]
        glg_ref[0, rows, :] = jax.nn.gelu(p[:, 3 * cc:])


def _in_proj(x, norm_g, shift, scale, w_bf16, b_in, tm):
    bsz, n, d = x.shape
    n4 = w_bf16.shape[1]
    cc = n4 // 4
    tok = pl.BlockSpec((1, tm, cc), lambda b, i: (b, i, 0))
    return pl.pallas_call(
        functools.partial(_inproj_body, rb=min(tm, 256)),
        grid=(bsz, n // tm),
        in_specs=[pl.BlockSpec((1, tm, d), lambda b, i: (b, i, 0)),
                  pl.BlockSpec((1, d), lambda b, i: (0, 0)),
                  pl.BlockSpec((1, 1, d), lambda b, i: (b, 0, 0)),
                  pl.BlockSpec((1, 1, d), lambda b, i: (b, 0, 0)),
                  pl.BlockSpec((d, n4), lambda b, i: (0, 0)),
                  pl.BlockSpec((1, n4), lambda b, i: (0, 0))],
        out_specs=[tok, tok, tok],
        out_shape=[jax.ShapeDtypeStruct((bsz, n, cc), F32)] * 3,
        compiler_params=_params("arbitrary", "arbitrary"),
        name="in_proj",
    )(x, norm_g.reshape(1, d), shift, scale, w_bf16, b_in.reshape(1, n4))


def _lru_body(*refs, reverse, tl, merge):
    x_ref, h0_ref, cw_ref, cb_ref, wg_ref, bg_ref, lam_ref = refs[:7]
    if merge:
        hother_ref, glg_ref = refs[7:9]
        refs = refs[2:]
    h_ref, hl_ref, a_ref, u_ref, carry_ref, halo_ref = refs[7:]
    i = pl.program_id(1)
    ch = x_ref.shape[2]

    @pl.when(i == 0)
    def _():
        carry_ref[...] = jnp.broadcast_to(h0_ref[0], carry_ref.shape)
        halo_ref[...] = jnp.zeros(halo_ref.shape, F32)

    x = x_ref[0]
    halo = halo_ref[...]
    row8 = lax.broadcasted_iota(jnp.int32, (SUBLANES, ch), 0)
    xc = cb_ref[...] + cw_ref[LRU_CONV_WIDTH - 1:LRU_CONV_WIDTH, :] * x
    for j in range(LRU_CONV_WIDTH - 1):
        s = LRU_CONV_WIDTH - 1 - j
        if not reverse:
            rolled = pltpu.roll(x, s, 0)
            edge = jnp.where(row8 < s, pltpu.roll(halo, s, 0), rolled[0:SUBLANES, :])
            tap = jnp.concatenate([edge, rolled[SUBLANES:, :]], axis=0)
        else:
            rolled = pltpu.roll(x, tl - s, 0)
            edge = jnp.where(row8 >= SUBLANES - s, pltpu.roll(halo, SUBLANES - s, 0), rolled[tl - SUBLANES:, :])
            tap = jnp.concatenate([rolled[:tl - SUBLANES, :], edge], axis=0)
        xc = xc + cw_ref[j:j + 1, :] * tap
    halo_ref[...] = x[0:SUBLANES, :] if reverse else x[tl - SUBLANES:tl, :]

    z = jnp.dot(xc.astype(BF16), wg_ref[...], preferred_element_type=F32) + bg_ref[...]
    half_c = (-0.5 * LRU_C) * jax.nn.softplus(-lam_ref[...])
    log_a = half_c * jnp.tanh(0.5 * z[:, :ch]) + half_c
    ig = 0.5 * jnp.tanh(0.5 * z[:, ch:]) + 0.5
    th = jnp.tanh(log_a)
    a_ref[...] = jnp.exp(log_a)
    m2 = -2.0 * th / (1.0 - th)
    u_ref[...] = jnp.where(m2 > 0.0, m2 * lax.rsqrt(m2), 0.0) * (ig * xc)

    ng = tl // SUBLANES
    rowid = lax.broadcasted_iota(jnp.int32, (SUBLANES, ch), 0)

    def group(gi, h):
        g = (ng - 1 - gi) if reverse else gi
        off = pl.multiple_of(g * SUBLANES, SUBLANES)
        a = a_ref[pl.ds(off, SUBLANES), :]
        u = u_ref[pl.ds(off, SUBLANES), :]
        for s in (1, 2, 4):
            if reverse:
                m = rowid < SUBLANES - s
                sh = SUBLANES - s
            else:
                m = rowid >= s
                sh = s
            ap = jnp.where(m, pltpu.roll(a, sh, 0), 1.0)
            up = jnp.where(m, pltpu.roll(u, sh, 0), 0.0)
            u = a * up + u
            a = a * ap
        hh = a * h + u
        if merge:
            u_ref[pl.ds(off, SUBLANES), :] = hh
        else:
            h_ref[0, pl.ds(off, SUBLANES), :] = hh
        edge = hh[0:1, :] if reverse else hh[SUBLANES - 1:SUBLANES, :]
        return jnp.broadcast_to(edge, (SUBLANES, ch))

    h = lax.fori_loop(0, ng, group, carry_ref[...], unroll=8)
    carry_ref[...] = h
    hl_ref[0] = h[0:1, :]
    if merge:
        h_ref[0] = ((u_ref[...] + hother_ref[0]) * glg_ref[0]).astype(BF16)


def _lru_dir(lx, h0, cw, cb, wgate, bgate, lam, reverse, tl, merge_with=None):
    bsz, n, ch = lx.shape
    nc = n // tl
    cmap = (lambda b, i: (b, nc - 1 - i, 0)) if reverse else (lambda b, i: (b, i, 0))
    full = lambda shape: pl.BlockSpec(shape, lambda b, i: (0,) * len(shape))
    merge = merge_with is not None
    extra = list(merge_with) if merge else []
    return pl.pallas_call(
        functools.partial(_lru_body, reverse=reverse, tl=tl, merge=merge),
        grid=(bsz, nc),
        in_specs=[pl.BlockSpec((1, tl, ch), cmap),
                  pl.BlockSpec((1, 1, ch), lambda b, i: (b, 0, 0)),
                  full((LRU_CONV_WIDTH, ch)), full((1, ch)),
                  full((ch, 2 * ch)), full((1, 2 * ch)), full((1, ch))]
                 + [pl.BlockSpec((1, tl, ch), cmap)] * len(extra),
        out_specs=[pl.BlockSpec((1, tl, ch), cmap),
                   pl.BlockSpec((1, 1, ch), lambda b, i: (b, 0, 0))],
        out_shape=[jax.ShapeDtypeStruct((bsz, n, ch), BF16 if merge else F32),
                   jax.ShapeDtypeStruct((bsz, 1, ch), F32)],
        scratch_shapes=[pltpu.VMEM((tl, ch), F32),
                        pltpu.VMEM((tl, ch), F32),
                        pltpu.VMEM((SUBLANES, ch), F32),
                        pltpu.VMEM((SUBLANES, ch), F32)],
        compiler_params=_params("arbitrary", "arbitrary"),
        name="lru_rev" if reverse else "lru_fwd",
    )(lx, h0, cw, cb.reshape(1, ch), wgate, bgate.reshape(1, 2 * ch), lam.reshape(1, ch), *extra)


ROW_STRIDE = GRID_W + 2 * BF16_ROWS


def _conv_row_body(x_ref, w_ref, b_ref, o_ref, pad_ref):
    n = x_ref.shape[1]
    nrows = n // GRID_W
    gap = jnp.zeros((BF16_ROWS, LANES), F32)

    def fill(r, c):
        base = pl.multiple_of(r * ROW_STRIDE, SUBLANES)
        src = pl.multiple_of(r * GRID_W, SUBLANES)
        pad_ref[pl.ds(base, BF16_ROWS), :] = gap
        pad_ref[pl.ds(base + BF16_ROWS, GRID_W), :] = x_ref[0, pl.ds(src, GRID_W), :]
        pad_ref[pl.ds(base + BF16_ROWS + GRID_W, BF16_ROWS), :] = gap
        return c

    lax.fori_loop(0, nrows, fill, 0)

    def row(r, c):
        base = pl.multiple_of(r * ROW_STRIDE, SUBLANES)
        acc = jnp.broadcast_to(b_ref[...], (GRID_W, LANES))
        for k in range(CONV_WIDTH):
            acc = acc + w_ref[k:k + 1, :] * pad_ref[pl.ds(base + BF16_ROWS - CONV_PAD + k, GRID_W), :]
        o_ref[0, pl.ds(pl.multiple_of(r * GRID_W, SUBLANES), GRID_W), :] = acc
        return c

    lax.fori_loop(0, nrows, row, 0, unroll=2)


def _conv_col_body(x_ref, w_ref, b_ref, o_ref, pad_ref):
    n = x_ref.shape[1]
    nrows = n // GRID_W
    halo = CONV_PAD * GRID_W
    pad_ref[0:halo, :] = jnp.zeros((halo, LANES), F32)
    pad_ref[halo + n:halo + n + halo, :] = jnp.zeros((halo, LANES), F32)

    def fill(r, c):
        src = pl.multiple_of(r * GRID_W, SUBLANES)
        pad_ref[pl.ds(halo + src, GRID_W), :] = x_ref[0, pl.ds(src, GRID_W), :]
        return c

    lax.fori_loop(0, nrows, fill, 0)

    def row(r, c):
        base = pl.multiple_of(r * GRID_W, SUBLANES)
        acc = jnp.broadcast_to(b_ref[...], (GRID_W, LANES))
        for k in range(CONV_WIDTH):
            acc = acc + w_ref[k:k + 1, :] * pad_ref[pl.ds(base + k * GRID_W, GRID_W), :]
        o_ref[0, pl.ds(base, GRID_W), :] = acc
        return c

    lax.fori_loop(0, nrows, row, 0, unroll=2)


def _conv_grid(glu, w, b):
    bsz, n, ch = glu.shape
    half = ch // 2
    ng = half // LANES
    outs = []
    for body, first, pad_rows in ((_conv_row_body, 0, (n // GRID_W) * ROW_STRIDE),
                                  (_conv_col_body, ng, n + 2 * CONV_PAD * GRID_W)):
        outs.append(pl.pallas_call(
            body,
            grid=(bsz, ng),
            in_specs=[pl.BlockSpec((1, n, LANES), lambda bb, g, first=first: (bb, 0, g + first)),
                      pl.BlockSpec((CONV_WIDTH, LANES), lambda bb, g, first=first: (0, g + first)),
                      pl.BlockSpec((1, LANES), lambda bb, g, first=first: (0, g + first))],
            out_specs=pl.BlockSpec((1, n, LANES), lambda bb, g: (bb, 0, g)),
            out_shape=jax.ShapeDtypeStruct((bsz, n, half), F32),
            scratch_shapes=[pltpu.VMEM((pad_rows, LANES), F32)],
            compiler_params=_params("arbitrary", "arbitrary"),
            name="conv_row" if first == 0 else "conv_col",
        )(glu, w, b.reshape(1, ch)))
    return outs


def _out_body(cr_ref, cc_ref, lng_ref, lnb_ref, yl_ref, wo_ref, bo_ref, x_ref,
              g1_ref, n2g_ref, sh2_ref, sc2_ref, rw_ref, x1_ref, v_ref, aff_ref):
    cv = jnp.concatenate([cr_ref[0], cc_ref[0]], axis=-1)
    mu = jnp.mean(cv, axis=-1, keepdims=True)
    dv = cv - mu
    var = jnp.mean(dv * dv, axis=-1, keepdims=True)
    yn = dv * lax.rsqrt(var + LN_EPS) * lng_ref[...] + lnb_ref[...]
    cy = yn * jax.nn.sigmoid(yn)
    cat = jnp.concatenate([cy.astype(BF16), yl_ref[0]], axis=-1)
    m = jnp.dot(cat, wo_ref[...], preferred_element_type=F32) + bo_ref[...]
    x1 = x_ref[0] + g1_ref[0] * m
    x1_ref[0] = x1
    ms = jnp.mean(x1 * x1, axis=-1, keepdims=True)
    v = x1 * lax.rsqrt(ms + RMS_EPS) * n2g_ref[...]
    v = v * (1.0 + sc2_ref[0]) + sh2_ref[0]
    v_ref[0] = v.astype(BF16)
    v_hi = v.astype(BF16)
    v_lo = (v - v_hi.astype(F32)).astype(BF16)
    rw = rw_ref[...]
    rw_hi = rw.astype(BF16)
    rw_lo = (rw - rw_hi.astype(F32)).astype(BF16)
    nt = (((1,), (1,)), ((), ()))
    ne = rw.shape[0]
    both = lax.dot_general(jnp.concatenate([rw_hi, rw_lo], axis=0), v_hi, nt, preferred_element_type=F32)
    lg = both[:ne] + both[ne:] + lax.dot_general(rw_hi, v_lo, nt, preferred_element_type=F32)
    ex = jnp.exp(lg - jnp.max(lg, axis=0, keepdims=True))
    aff_ref[0] = ex / jnp.sum(ex, axis=0, keepdims=True)


def _out_proj_route(conv_r, conv_c, ln_g, ln_b, yl, wo_bf16, b_out, x, g1, n2g, sh2, sc2, rw_t, tm):
    bsz, n, d = x.shape
    half = conv_r.shape[2]
    ch = yl.shape[2]
    ne = rw_t.shape[0]
    tok = lambda c: pl.BlockSpec((1, tm, c), lambda b, i: (b, i, 0))
    full = lambda shape: pl.BlockSpec(shape, lambda b, i: (0,) * len(shape))
    per_b = pl.BlockSpec((1, 1, d), lambda b, i: (b, 0, 0))
    return pl.pallas_call(
        _out_body,
        grid=(bsz, n // tm),
        in_specs=[tok(half), tok(half), full((1, 2 * half)), full((1, 2 * half)),
                  tok(ch), full((2 * half + ch, d)), full((1, d)), tok(d),
                  per_b, full((1, d)), per_b, per_b, full((ne, d))],
        out_specs=[tok(d), tok(d), pl.BlockSpec((1, ne, tm), lambda b, i: (b, 0, i))],
        out_shape=[jax.ShapeDtypeStruct((bsz, n, d), F32),
                   jax.ShapeDtypeStruct((bsz, n, d), BF16),
                   jax.ShapeDtypeStruct((bsz, ne, n), F32)],
        compiler_params=_params("arbitrary", "arbitrary"),
        name="out_proj_route",
    )(conv_r, conv_c, ln_g.reshape(1, -1), ln_b.reshape(1, -1), yl, wo_bf16,
      b_out.reshape(1, d), x, g1, n2g.reshape(1, d), sh2, sc2, rw_t)


def _token_cumsum(m, rows_per_expert):
    er = m.shape[0]
    li = lax.broadcasted_iota(jnp.int32, (LANES, LANES), 0)
    lj = lax.broadcasted_iota(jnp.int32, (LANES, LANES), 1)
    upper = jnp.where(li <= lj, 1.0, 0.0).astype(BF16)
    cs = jnp.dot(m.astype(BF16), upper, preferred_element_type=F32)
    rt = jnp.broadcast_to(cs[:, LANES - 1:LANES], (er, LANES)).astype(BF16)
    ri = lax.broadcasted_iota(jnp.int32, (er, er), 0)
    ci = lax.broadcasted_iota(jnp.int32, (er, er), 1)
    same = (ri // rows_per_expert) == (ci // rows_per_expert)
    lower = jnp.where(same & (ci < ri), 1.0, 0.0).astype(BF16)
    rstart = jnp.dot(lower, rt, preferred_element_type=F32)
    return cs, rstart


def _topk_body(aff_ref, gate_ref, pos_ref, rs_ref, cs_scr, rs_scr, *, cap):
    aff = aff_ref[0]
    ne, rows, _ = aff.shape

    def bisect(i, thr):
        cand = thr | jnp.left_shift(jnp.int32(1), 30 - i)
        cnt = jnp.sum((aff >= pltpu.bitcast(cand, F32)).astype(jnp.int32), axis=(1, 2), keepdims=True)
        return jnp.where(cnt >= cap, cand, thr)

    thr = lax.fori_loop(0, 31, bisect, jnp.zeros((ne, 1, 1), jnp.int32))
    gt = aff >= pltpu.bitcast(thr + 1, F32)
    eq = (aff >= pltpu.bitcast(thr, F32)) & jnp.logical_not(gt)
    need = (cap - jnp.sum(gt.astype(jnp.int32), axis=(1, 2), keepdims=True)).astype(F32)
    eqf = jnp.where(eq, 1.0, 0.0)
    rank_in_row, rank_row0 = _token_cumsum(eqf.reshape(ne * rows, LANES), rows)
    rank_excl = (rank_in_row + rank_row0).reshape(ne, rows, LANES) - eqf
    sel = gt | (eq & (rank_excl < need))
    self_ = jnp.where(sel, 1.0, 0.0)
    cs, rstart = _token_cumsum(self_.reshape(ne * rows, LANES), rows)
    cs3 = cs.reshape(ne, rows, LANES)
    rstart3 = rstart.reshape(ne, rows, LANES)
    cs_scr[...] = cs3
    rs_scr[...] = rstart3
    pos_ref[0] = jnp.where(sel, cs3 + rstart3 - 1.0, -1.0).astype(jnp.int32)
    rs_ref[0] = rstart3.astype(jnp.int32)

    jrow = lax.broadcasted_iota(jnp.int32, (1, cap), 1).astype(F32)
    sub_r = lax.broadcasted_iota(jnp.int32, (rows, cap), 0).astype(F32)
    sub_l = lax.broadcasted_iota(jnp.int32, (LANES, cap), 0).astype(F32)

    def per_expert(e, c):
        cl = cs_scr[e]
        af = aff_ref[0, e]
        rowtot = cl[:, LANES - 1:LANES]
        before = rowtot + rs_scr[e][:, 0:1] <= jrow
        rj = jnp.sum(jnp.where(before, 1.0, 0.0), axis=0, keepdims=True)
        rowbase = jnp.sum(jnp.where(before, rowtot, 0.0), axis=0, keepdims=True)
        onehot = jnp.where(sub_r == rj, 1.0, 0.0).astype(BF16)
        a1 = af.astype(BF16)
        r1 = af - a1.astype(F32)
        a2 = r1.astype(BF16)
        a3 = (r1 - a2.astype(F32)).astype(BF16)
        lhs = jnp.concatenate([cl.astype(BF16), a1, a2, a3], axis=1)
        gathered = lax.dot_general(lhs, onehot, (((0,), (0,)), ((), ())), preferred_element_type=F32)
        g = gathered[:LANES]
        ga = gathered[LANES:2 * LANES] + gathered[2 * LANES:3 * LANES] + gathered[3 * LANES:]
        lanepos = jnp.sum(jnp.where(g <= jrow - rowbase, 1.0, 0.0), axis=0, keepdims=True)
        gate = jnp.sum(jnp.where(sub_l == lanepos, ga, 0.0), axis=0, keepdims=True)
        gate_ref[0, pl.ds(e, 1), :] = gate
        return c

    lax.fori_loop(0, ne, per_expert, 0)


def _topk_route(aff4, cap):
    bsz, ne, rows, _ = aff4.shape
    blk4 = pl.BlockSpec((1, ne, rows, LANES), lambda b: (b, 0, 0, 0))
    lst = pl.BlockSpec((1, ne, cap), lambda b: (b, 0, 0))
    return pl.pallas_call(
        functools.partial(_topk_body, cap=cap),
        grid=(bsz,),
        in_specs=[blk4],
        out_specs=[lst, blk4, blk4],
        out_shape=[jax.ShapeDtypeStruct((bsz, ne, cap), F32),
                   jax.ShapeDtypeStruct((bsz, ne, rows, LANES), jnp.int32),
                   jax.ShapeDtypeStruct((bsz, ne, rows, LANES), jnp.int32)],
        scratch_shapes=[pltpu.VMEM((ne, rows, LANES), F32), pltpu.VMEM((ne, rows, LANES), F32)],
        compiler_params=_params("arbitrary"),
        name="topk_route",
    )(aff4)


PACK_TOK = 256
PACK_SLAB = 64
PACK_SHIFT = 6


def _pack_body(rs_ref, pos_ref, v_ref, xs_hbm, stage, stage_x, pend, sems, sem_x, *, cap):
    b = pl.program_id(0)
    s = pl.program_id(1)
    nb = pl.num_programs(0)
    ns = pl.num_programs(1)
    ne = pos_ref.shape[2]
    tpr = PACK_TOK // LANES
    nr = ns * tpr
    step = b * ns + s
    slot = step % 2
    slab = PACK_SLAB

    @pl.when(s == 0)
    def _():
        pend[...] = jnp.zeros(pend.shape, BF16)

    def first_pos(rr, e):
        return rs_ref[(b * ne + e) * (nr + 1) + rr]

    def floor16(p):
        return lax.shift_left(lax.shift_right_logical(p, 4), 4)

    p0 = [first_pos(s * tpr, e) for e in range(ne)]
    p1 = [first_pos(s * tpr + tpr, e) for e in range(ne)]
    a0 = [floor16(p0[e]) for e in range(ne)]
    a1 = [floor16(p1[e]) for e in range(ne)]
    has = [a1[e] < p1[e] for e in range(ne)]
    prnd = [lax.shift_right_logical(a1[e] - a0[e], PACK_SHIFT) for e in range(ne)]
    poff = [pl.multiple_of(jnp.bitwise_and(a1[e] - a0[e], slab - 1), BF16_ROWS) for e in range(ne)]

    vb = v_ref[0]
    pos = [jnp.concatenate([pos_ref[0, j, e:e + 1, :] for j in range(tpr)], axis=1) for e in range(ne)]
    kiota = lax.broadcasted_iota(jnp.int32, (slab, PACK_TOK), 0)

    def onehot(rnd):
        blocks = [jnp.where(kiota == pos[e] - (a0[e] + rnd * slab), 1.0, 0.0) for e in range(ne)]
        return jnp.concatenate(blocks, axis=0).astype(BF16)

    def keep_partial(buf, e, rnd, old):
        grp = buf[pl.ds(e * slab + poff[e], BF16_ROWS), :]
        return jnp.where(jnp.logical_and(has[e], prnd[e] == rnd), grp, old)

    cur = stage.at[slot]
    cur[...] = jnp.dot(onehot(0), vb, preferred_element_type=F32).astype(BF16)
    for e in range(ne):
        cur[e * slab:e * slab + BF16_ROWS, :] += pend[e]
    for e in range(ne):
        pend[e] = keep_partial(cur, e, 0, jnp.zeros((BF16_ROWS, vb.shape[1]), BF16))

    def slab_bytes_wait(buf, sem):
        pltpu.make_async_copy(buf, buf, sem).wait()

    @pl.when(step > 0)
    def _():
        slab_bytes_wait(stage.at[1 - slot], sems.at[1 - slot])

    @pl.when(s == 0)
    def _():
        stage_x[0:slab, :] = jnp.zeros((slab, stage_x.shape[1]), BF16)
        for e in range(ne):
            pltpu.make_async_copy(stage_x.at[0:slab], xs_hbm.at[e, b, pl.ds(cap, slab)], sem_x).start()
        slab_bytes_wait(stage_x, sem_x)

    for e in range(ne):
        pltpu.make_async_copy(stage.at[slot, e * slab:(e + 1) * slab],
                              xs_hbm.at[e, b, pl.ds(pl.multiple_of(a0[e], BF16_ROWS), slab)], sems.at[slot]).start()

    span = p1[0] - a0[0]
    for e in range(1, ne):
        span = jnp.maximum(span, p1[e] - a0[e])
    rounds = jnp.maximum(1, lax.shift_right_logical(span + (slab - 1), PACK_SHIFT))

    def extra(rnd, carry):
        stage_x[...] = jnp.dot(onehot(rnd), vb, preferred_element_type=F32).astype(BF16)
        for e in range(ne):
            pend[e] = keep_partial(stage_x, e, rnd, pend[e])
        for e in range(ne):
            @pl.when(p1[e] - a0[e] > rnd * slab)
            def _(e=e):
                cp = pltpu.make_async_copy(
                    stage_x.at[e * slab:(e + 1) * slab],
                    xs_hbm.at[e, b, pl.ds(pl.multiple_of(a0[e] + rnd * slab, BF16_ROWS), slab)], sem_x)
                cp.start()
                cp.wait()
        return carry

    lax.fori_loop(1, rounds, extra, 0)

    @pl.when(step == nb * ns - 1)
    def _():
        slab_bytes_wait(stage.at[slot], sems.at[slot])


def _pack_tokens(rs_ext, pos_r, v, cap):
    bsz, n, d = v.shape
    ne = pos_r.shape[2]
    tpr = PACK_TOK // LANES
    return pl.pallas_call(
        functools.partial(_pack_body, cap=cap),
        grid_spec=pltpu.PrefetchScalarGridSpec(
            num_scalar_prefetch=1,
            grid=(bsz, n // PACK_TOK),
            in_specs=[pl.BlockSpec((1, tpr, ne, LANES), lambda b, s, rs: (b, s, 0, 0)),
                      pl.BlockSpec((1, PACK_TOK, d), lambda b, s, rs: (b, s, 0))],
            out_specs=pl.BlockSpec(memory_space=pl.ANY),
            scratch_shapes=[pltpu.VMEM((2, ne * PACK_SLAB, d), BF16),
                            pltpu.VMEM((ne * PACK_SLAB, d), BF16),
                            pltpu.VMEM((ne, BF16_ROWS, d), BF16),
                            pltpu.SemaphoreType.DMA((2,)),
                            pltpu.SemaphoreType.DMA(())]),
        out_shape=jax.ShapeDtypeStruct((ne, bsz, cap + PACK_SLAB, d), BF16),
        compiler_params=_params("arbitrary", "arbitrary"),
        name="pack_tokens",
    )(rs_ext, pos_r, v)


def _ffn_body(x_ref, gate_ref, wg_ref, wu_ref, wd_ref, ys_hbm, acc, stage, osems, *, mg, nf, sub, och):
    e = pl.program_id(0)
    f = pl.program_id(1)
    spg = x_ref.shape[2]

    @pl.when(f == 0)
    def _():
        acc[...] = jnp.zeros(acc.shape, F32)

    wgb = wg_ref[0].astype(BF16)
    wub = wu_ref[0].astype(BF16)
    wdb = wd_ref[0].astype(BF16)
    for i in range(mg // sub):
        rows = slice(i * sub, (i + 1) * sub)
        x = x_ref[0, i * (sub // spg):(i + 1) * (sub // spg)].reshape(sub, x_ref.shape[3])
        gg = jnp.dot(x, wgb, preferred_element_type=F32)
        uu = jnp.dot(x, wub, preferred_element_type=F32)
        h = (gg * jax.nn.sigmoid(gg) * uu).astype(BF16)
        acc[rows, :] += jnp.dot(h, wdb, preferred_element_type=F32)

    @pl.when(f == nf - 1)
    def _():
        gate_t = gate_ref[0].T

        def out_copy(c):
            return pltpu.make_async_copy(stage.at[c % 2], ys_hbm.at[e, pl.ds(c * och, och)], osems.at[c % 2])

        nchunks = mg // och
        for c in range(nchunks):
            if c >= 2:
                out_copy(c - 2).wait()
            for i in range(och // LANES):
                blk = c * (och // LANES) + i
                rows = slice(blk * LANES, (blk + 1) * LANES)
                stage[c % 2, i * LANES:(i + 1) * LANES, :] = (acc[rows, :] * gate_t[:, blk:blk + 1]).astype(BF16)
            out_copy(c).start()
        for c in range(max(nchunks - 2, 0), nchunks):
            out_copy(c).wait()


def _expert_ffn(xs, gate_rows, wg, wu, wd, tf):
    ne, d, fdim = wg.shape
    bsz, cap = xs.shape[1], gate_rows.shape[1] * LANES // xs.shape[1]
    mg = bsz * cap
    nf = fdim // tf
    sub = min(mg, cap)
    och = min(mg, 512)
    return pl.pallas_call(
        functools.partial(_ffn_body, mg=mg, nf=nf, sub=sub, och=och),
        grid_spec=pltpu.PrefetchScalarGridSpec(
            num_scalar_prefetch=0,
            grid=(ne, nf),
            in_specs=[pl.BlockSpec((1, bsz, cap, d), lambda e, f: (e, 0, 0, 0)),
                      pl.BlockSpec((1, mg // LANES, LANES), lambda e, f: (e, 0, 0)),
                      pl.BlockSpec((1, d, tf), lambda e, f: (e, 0, f)),
                      pl.BlockSpec((1, d, tf), lambda e, f: (e, 0, f)),
                      pl.BlockSpec((1, tf, d), lambda e, f: (e, f, 0))],
            out_specs=pl.BlockSpec(memory_space=pl.ANY),
            scratch_shapes=[pltpu.VMEM((mg, d), F32),
                            pltpu.VMEM((2, och, d), BF16),
                            pltpu.SemaphoreType.DMA((2,))]),
        out_shape=jax.ShapeDtypeStruct((ne, mg, d), BF16),
        compiler_params=_params("arbitrary", "arbitrary"),
        name="expert_ffn",
    )(xs, gate_rows, wg, wu, wd)


SLAB = PACK_SLAB
TILE_ROWS = PACK_TOK // LANES


def _combine_body(rs_ref, pos_ref, ys_hbm, x1_ref, g5_ref, fng_ref, o_ref, stk, stk_x, sems, sem_x, *, cap, tps):
    b = pl.program_id(0)
    s = pl.program_id(1)
    nb = pl.num_programs(0)
    ns = pl.num_programs(1)
    ne = pos_ref.shape[2]
    nr = ns * tps * TILE_ROWS
    tok = TILE_ROWS * LANES
    step = b * ns + s
    slot = step % 2
    tile_rows = ne * SLAB

    def first_pos(bb, rr, e):
        return rs_ref[(bb * ne + e) * (nr + 1) + rr]

    def aligned(p0):
        return lax.shift_left(lax.shift_right_logical(p0, 4), 4)

    def slab_start(a0, rnd):
        return pl.multiple_of(jnp.minimum(a0 + rnd * SLAB, cap - SLAB), BF16_ROWS)

    def slab_copy(bb, e, a, dst, row, sem):
        return pltpu.make_async_copy(ys_hbm.at[e, pl.ds(bb * cap + a, SLAB)], dst.at[pl.ds(row, SLAB)], sem)

    def issue(bb, ss, sl):
        for j in range(tps):
            for e in range(ne):
                a = slab_start(aligned(first_pos(bb, (ss * tps + j) * TILE_ROWS, e)), 0)
                slab_copy(bb, e, a, stk.at[sl], j * tile_rows + e * SLAB, sems.at[sl]).start()

    @pl.when(step == 0)
    def _():
        issue(b, s, slot)

    @pl.when(step + 1 < nb * ns)
    def _():
        nxt = step + 1
        issue(nxt // ns, nxt % ns, 1 - slot)

    kiota = lax.broadcasted_iota(jnp.int32, (SLAB, tok), 0)
    tn = (((0,), (0,)), ((), ()))
    pltpu.make_async_copy(stk.at[slot], stk.at[slot], sems.at[slot]).wait()

    for j in range(tps):
        r = (s * tps + j) * TILE_ROWS
        pos = jnp.concatenate([pos_ref[0, j * TILE_ROWS + q] for q in range(TILE_ROWS)], axis=1)
        a0 = [aligned(first_pos(b, r, e)) for e in range(ne)]

        def onehot(rnd, pos=pos, a0=a0):
            blocks = []
            for e in range(ne):
                rel = pos[e:e + 1, :] - slab_start(a0[e], rnd)
                fresh = (pos[e:e + 1, :] - a0[e]) >= rnd * SLAB
                blocks.append(jnp.where((kiota == rel) & fresh, 1.0, 0.0))
            return jnp.concatenate(blocks, axis=0).astype(BF16)

        moe = lax.dot_general(onehot(0), stk[slot, j * tile_rows:(j + 1) * tile_rows, :], tn,
                              preferred_element_type=F32)

        span = first_pos(b, r + TILE_ROWS, 0) - a0[0]
        for e in range(1, ne):
            span = jnp.maximum(span, first_pos(b, r + TILE_ROWS, e) - a0[e])
        rounds = jnp.maximum(1, lax.shift_right_logical(span + (SLAB - 1), PACK_SHIFT))

        def extra(rnd, m, a0=a0, onehot=onehot):
            for e in range(ne):
                slab_copy(b, e, slab_start(a0[e], rnd), stk_x, e * SLAB, sem_x).start()
            pltpu.make_async_copy(stk_x, stk_x, sem_x).wait()
            return m + lax.dot_general(onehot(rnd), stk_x[...], tn, preferred_element_type=F32)

        moe = lax.fori_loop(1, rounds, extra, moe)

        y = x1_ref[0, j * tok:(j + 1) * tok, :] + g5_ref[0] * moe
        ms = jnp.mean(y * y, axis=-1, keepdims=True)
        o_ref[0, j * tok:(j + 1) * tok, :] = y * lax.rsqrt(ms + RMS_EPS) * fng_ref[...]


def _combine_norm(rs_ext, pos_r, ys, x1, g5, fng, cap, tps):
    bsz, n, d = x1.shape
    ns = n // (PACK_TOK * tps)
    ne = pos_r.shape[2]
    return pl.pallas_call(
        functools.partial(_combine_body, cap=cap, tps=tps),
        grid_spec=pltpu.PrefetchScalarGridSpec(
            num_scalar_prefetch=1,
            grid=(bsz, ns),
            in_specs=[pl.BlockSpec((1, tps * TILE_ROWS, ne, LANES), lambda b, s, rs: (b, s, 0, 0)),
                      pl.BlockSpec(memory_space=pl.ANY),
                      pl.BlockSpec((1, tps * PACK_TOK, d), lambda b, s, rs: (b, s, 0)),
                      pl.BlockSpec((1, 1, d), lambda b, s, rs: (b, 0, 0)),
                      pl.BlockSpec((1, d), lambda b, s, rs: (0, 0))],
            out_specs=pl.BlockSpec((1, tps * PACK_TOK, d), lambda b, s, rs: (b, s, 0)),
            scratch_shapes=[pltpu.VMEM((2, tps * ne * SLAB, d), BF16),
                            pltpu.VMEM((ne * SLAB, d), BF16),
                            pltpu.SemaphoreType.DMA((2,)),
                            pltpu.SemaphoreType.DMA(())]),
        out_shape=jax.ShapeDtypeStruct((bsz, n, d), F32),
        compiler_params=_params("arbitrary", "arbitrary"),
        name="combine_norm",
    )(rs_ext, pos_r, ys, x1, g5, fng.reshape(1, d))


def _block_diag(w):
    heads, hd, _ = w.shape
    eye = jnp.eye(heads, dtype=w.dtype)
    return (eye[:, None, :, None] * w[:, :, None, :]).reshape(heads * hd, heads * hd)


def _tile(n, pref):
    return pref if n % pref == 0 else n


def kernel(x, c, ctx, c_ctx, norm1_g, norm2_g, ada_w, ada_b, w_in, b_in, conv_dw_w, conv_dw_b, conv_ln_g, conv_ln_b, lru_conv_w, lru_conv_b, lru_wa, lru_ba, lru_wi, lru_bi, lru_lambda, w_out, b_out, router_w, exp_w_gate, exp_w_up, exp_w_down, final_norm_g):
    assert norm1_g.shape[0] == 1
    mod, x1, v, aff = _mixer(x, c, ctx, c_ctx, norm1_g[0], norm2_g[0], ada_w[0], ada_b[0], w_in[0], b_in[0],
                             conv_dw_w[0], conv_dw_b[0], conv_ln_g[0], conv_ln_b[0], lru_conv_w[0],
                             lru_conv_b[0], lru_wa[0], lru_ba[0], lru_wi[0], lru_bi[0], lru_lambda[0],
                             w_out[0], b_out[0], router_w[0])
    return _moe_norm(x1, v, aff, mod[5], exp_w_gate[0], exp_w_up[0], exp_w_down[0], final_norm_g)


def _mixer(x, c, ctx, c_ctx, norm1_g, norm2_g, ada_w, ada_b, w_in, b_in, conv_w, conv_b, ln_g, ln_b,
           lru_cw, lru_cb, lru_wa, lru_ba, lru_wi, lru_bi, lru_lam, w_out, b_out, router_w):
    bsz, n, d = x.shape
    cond8 = jnp.zeros((SUBLANES, d), F32).at[:bsz].set(c).at[bsz].set(c_ctx)
    mods = _ada_mod(cond8, ada_w, ada_b)
    mod = [mods[:bsz, k * d:(k + 1) * d].reshape(bsz, 1, d) for k in range(N_MOD)]
    mod_c = [jnp.broadcast_to(mods[bsz:bsz + 1, k * d:(k + 1) * d].reshape(1, 1, d), (bsz, 1, d))
             for k in range(2)]
    w_in_b = w_in.astype(BF16)
    w_out_b = w_out.astype(BF16)
    ch = lru_cb.shape[1]
    wgate = [jnp.concatenate([_block_diag(lru_wa[dd]), _block_diag(lru_wi[dd])], axis=1).astype(BF16)
             for dd in range(2)]
    bgate = [jnp.concatenate([lru_ba[dd], lru_bi[dd]]) for dd in range(2)]

    def lru(lx, h0, dd, reverse, merge_with=None):
        return _lru_dir(lx, h0, lru_cw[dd], lru_cb[dd], wgate[dd], bgate[dd], lru_lam[dd], reverse,
                        _tile(lx.shape[1], 256), merge_with)

    _, c_lx, _ = _in_proj(ctx, norm1_g, mod_c[0], mod_c[1], w_in_b, b_in, _tile(ctx.shape[1], 256))
    zero_h = jnp.zeros((bsz, 1, ch), F32)
    _, hf0 = lru(c_lx, zero_h, 0, False)
    _, hb0 = lru(c_lx, zero_h, 1, True)

    x_glu, x_lx, x_glg = _in_proj(x, norm1_g, mod[0], mod[1], w_in_b, b_in, _tile(n, 1024))
    hb, _ = lru(x_lx, hb0, 1, True)
    yl, _ = lru(x_lx, hf0, 0, False, merge_with=(hb, x_glg))
    conv_r, conv_c = _conv_grid(x_glu, conv_w, conv_b)
    x1, v, aff = _out_proj_route(conv_r, conv_c, ln_g, ln_b, yl, w_out_b, b_out, x, mod[2],
                                 norm2_g, mod[3], mod[4], router_w.T, _tile(n, 512))
    return mod, x1, v, aff


def _moe_norm(x1, v, aff, gate2, wg, wu, wd, final_norm_g):
    bsz, n, d = x1.shape
    ne = aff.shape[1]
    cap = EC_CAPACITY * n // ne
    gate, pos, rstart = _topk_route(aff.reshape(bsz, ne, n // LANES, LANES), cap)
    gate_rows = jnp.swapaxes(gate, 0, 1).reshape(ne, bsz * cap // LANES, LANES)
    rs_ext = jnp.concatenate([rstart[..., 0], jnp.full((bsz, ne, 1), cap, jnp.int32)], axis=-1).reshape(-1)
    pos_r = jnp.swapaxes(pos, 1, 2)
    xs = _pack_tokens(rs_ext, pos_r, v, cap)
    ys = _expert_ffn(xs, gate_rows, wg, wu, wd, 256)
    return _combine_norm(rs_ext, pos_r, ys, x1, gate2, final_norm_g, cap, 2)
```

```python
import functools

import jax
import jax.numpy as jnp
from jax import lax
from jax.experimental import pallas as pl
from jax.experimental.pallas import tpu as pltpu

GRID_W = 64
CONV_WIDTH = 31
CONV_PAD = (CONV_WIDTH - 1) // 2
LRU_CONV_WIDTH = 4
LRU_C = 8.0
N_EXPERTS = 16
EC_CAPACITY = 2
N_MOD = 6
RMS_EPS = 1e-6
LN_EPS = 1e-5

LANES = 128
SUBLANES = 8
BF16_ROWS = 16
VMEM_LIMIT = 56 * 1024 * 1024

F32 = jnp.float32
BF16 = jnp.bfloat16
HIGHEST = lax.Precision.HIGHEST


def _params(*sem):
    return pltpu.CompilerParams(dimension_semantics=sem, vmem_limit_bytes=VMEM_LIMIT)


def _ada_body(c_ref, w_ref, b_ref, o_ref):
    s = c_ref[...]
    s = s * jax.nn.sigmoid(s)
    o_ref[...] = jnp.dot(s, w_ref[...], precision=HIGHEST, preferred_element_type=F32) + b_ref[...]


def _ada_mod(cond8, ada_w, ada_b):
    d, n = ada_w.shape
    tn = n // 4
    return pl.pallas_call(
        _ada_body,
        grid=(n // tn,),
        in_specs=[pl.BlockSpec((SUBLANES, d), lambda j: (0, 0)),
                  pl.BlockSpec((d, tn), lambda j: (0, j)),
                  pl.BlockSpec((1, tn), lambda j: (0, j))],
        out_specs=pl.BlockSpec((SUBLANES, tn), lambda j: (0, j)),
        out_shape=jax.ShapeDtypeStruct((SUBLANES, n), F32),
        compiler_params=_params("arbitrary"),
        name="ada_mod",
    )(cond8, ada_w, ada_b.reshape(1, n))


def _inproj_body(x_ref, g_ref, sh_ref, sc_ref, w_ref, b_ref, glu_ref, lx_ref, glg_ref, *, rb):
    gain = g_ref[...] * (1.0 + sc_ref[0])
    for i in range(x_ref.shape[1] // rb):
        rows = slice(i * rb, (i + 1) * rb)
        x = x_ref[0, rows, :]
        ms = jnp.mean(x * x, axis=-1, keepdims=True)
        u = x * lax.rsqrt(ms + RMS_EPS) * gain + sh_ref[0]
        p = jnp.dot(u.astype(BF16), w_ref[...], preferred_element_type=F32) + b_ref[...]
        cc = p.shape[1] // 4
        glu_ref[0, rows, :] = (p[:, :cc] * jax.nn.sigmoid(p[:, cc:2 * cc])).astype(glu_ref.dtype)
        lx_ref[0, rows, :] = p[:, 2 * cc:3 * cc]
        glg_ref[0, rows, :] = jax.nn.gelu(p[:, 3 * cc:]).astype(glg_ref.dtype)


def _in_proj(x, norm_g, shift, scale, w_bf16, b_in, tm):
    bsz, n, d = x.shape
    n4 = w_bf16.shape[1]
    cc = n4 // 4
    tok = pl.BlockSpec((1, tm, cc), lambda b, i: (b, i, 0))
    return pl.pallas_call(
        functools.partial(_inproj_body, rb=min(tm, 256)),
        grid=(bsz, n // tm),
        in_specs=[pl.BlockSpec((1, tm, d), lambda b, i: (b, i, 0)),
                  pl.BlockSpec((1, d), lambda b, i: (0, 0)),
                  pl.BlockSpec((1, 1, d), lambda b, i: (b, 0, 0)),
                  pl.BlockSpec((1, 1, d), lambda b, i: (b, 0, 0)),
                  pl.BlockSpec((d, n4), lambda b, i: (0, 0)),
                  pl.BlockSpec((1, n4), lambda b, i: (0, 0))],
        out_specs=[tok, tok, tok],
        out_shape=[jax.ShapeDtypeStruct((bsz, n, cc), BF16),
                   jax.ShapeDtypeStruct((bsz, n, cc), F32),
                   jax.ShapeDtypeStruct((bsz, n, cc), BF16)],
        compiler_params=_params("arbitrary", "arbitrary"),
        name="in_proj",
    )(x, norm_g.reshape(1, d), shift, scale, w_bf16, b_in.reshape(1, n4))


def _lru_body(*refs, reverse, tl, merge):
    x_ref, h0_ref, cw_ref, cb_ref, wg_ref, bg_ref, lam_ref = refs[:7]
    if merge:
        hother_ref, glg_ref = refs[7:9]
        refs = refs[2:]
    h_ref, hl_ref, a_ref, u_ref, carry_ref, halo_ref = refs[7:]
    i = pl.program_id(1)
    ch = x_ref.shape[2]

    @pl.when(i == 0)
    def _():
        carry_ref[...] = jnp.broadcast_to(h0_ref[0], carry_ref.shape)
        halo_ref[...] = jnp.zeros(halo_ref.shape, F32)

    x = x_ref[0]
    halo = halo_ref[...]
    row8 = lax.broadcasted_iota(jnp.int32, (SUBLANES, ch), 0)
    xc = cb_ref[...] + cw_ref[LRU_CONV_WIDTH - 1:LRU_CONV_WIDTH, :] * x
    for j in range(LRU_CONV_WIDTH - 1):
        s = LRU_CONV_WIDTH - 1 - j
        if not reverse:
            rolled = pltpu.roll(x, s, 0)
            edge = jnp.where(row8 < s, pltpu.roll(halo, s, 0), rolled[0:SUBLANES, :])
            tap = jnp.concatenate([edge, rolled[SUBLANES:, :]], axis=0)
        else:
            rolled = pltpu.roll(x, tl - s, 0)
            edge = jnp.where(row8 >= SUBLANES - s, pltpu.roll(halo, SUBLANES - s, 0), rolled[tl - SUBLANES:, :])
            tap = jnp.concatenate([rolled[:tl - SUBLANES, :], edge], axis=0)
        xc = xc + cw_ref[j:j + 1, :] * tap
    halo_ref[...] = x[0:SUBLANES, :] if reverse else x[tl - SUBLANES:tl, :]

    z = jnp.dot(xc.astype(BF16), wg_ref[...], preferred_element_type=F32) + bg_ref[...]
    half_c = (-0.5 * LRU_C) * jax.nn.softplus(-lam_ref[...])
    log_a = half_c * jnp.tanh(0.5 * z[:, :ch]) + half_c
    ig = 0.5 * jnp.tanh(0.5 * z[:, ch:]) + 0.5
    th = jnp.tanh(log_a)
    a_ref[...] = jnp.exp(log_a)
    m2 = -2.0 * th / (1.0 - th)
    u_ref[...] = jnp.where(m2 > 0.0, m2 * lax.rsqrt(m2), 0.0) * (ig * xc)

    ng = tl // SUBLANES
    rowid = lax.broadcasted_iota(jnp.int32, (SUBLANES, ch), 0)

    def group(gi, h):
        g = (ng - 1 - gi) if reverse else gi
        off = pl.multiple_of(g * SUBLANES, SUBLANES)
        a = a_ref[pl.ds(off, SUBLANES), :]
        u = u_ref[pl.ds(off, SUBLANES), :]
        for s in (1, 2, 4):
            if reverse:
                m = rowid < SUBLANES - s
                sh = SUBLANES - s
            else:
                m = rowid >= s
                sh = s
            ap = jnp.where(m, pltpu.roll(a, sh, 0), 1.0)
            up = jnp.where(m, pltpu.roll(u, sh, 0), 0.0)
            u = a * up + u
            a = a * ap
        hh = a * h + u
        if merge:
            u_ref[pl.ds(off, SUBLANES), :] = hh
        else:
            h_ref[0, pl.ds(off, SUBLANES), :] = hh
        edge = hh[0:1, :] if reverse else hh[SUBLANES - 1:SUBLANES, :]
        return jnp.broadcast_to(edge, (SUBLANES, ch))

    h = lax.fori_loop(0, ng, group, carry_ref[...], unroll=8)
    carry_ref[...] = h
    hl_ref[0] = h[0:1, :]
    if merge:
        h_ref[0] = ((u_ref[...] + hother_ref[0]) * glg_ref[0].astype(F32)).astype(BF16)


def _lru_dir(lx, h0, cw, cb, wgate, bgate, lam, reverse, tl, merge_with=None):
    bsz, n, ch = lx.shape
    nc = n // tl
    cmap = (lambda b, i: (b, nc - 1 - i, 0)) if reverse else (lambda b, i: (b, i, 0))
    full = lambda shape: pl.BlockSpec(shape, lambda b, i: (0,) * len(shape))
    merge = merge_with is not None
    extra = list(merge_with) if merge else []
    return pl.pallas_call(
        functools.partial(_lru_body, reverse=reverse, tl=tl, merge=merge),
        grid=(bsz, nc),
        in_specs=[pl.BlockSpec((1, tl, ch), cmap),
                  pl.BlockSpec((1, 1, ch), lambda b, i: (b, 0, 0)),
                  full((LRU_CONV_WIDTH, ch)), full((1, ch)),
                  full((ch, 2 * ch)), full((1, 2 * ch)), full((1, ch))]
                 + [pl.BlockSpec((1, tl, ch), cmap)] * len(extra),
        out_specs=[pl.BlockSpec((1, tl, ch), cmap),
                   pl.BlockSpec((1, 1, ch), lambda b, i: (b, 0, 0))],
        out_shape=[jax.ShapeDtypeStruct((bsz, n, ch), BF16 if merge else F32),
                   jax.ShapeDtypeStruct((bsz, 1, ch), F32)],
        scratch_shapes=[pltpu.VMEM((tl, ch), F32),
                        pltpu.VMEM((tl, ch), F32),
                        pltpu.VMEM((SUBLANES, ch), F32),
                        pltpu.VMEM((SUBLANES, ch), F32)],
        compiler_params=_params("arbitrary", "arbitrary"),
        name="lru_rev" if reverse else "lru_fwd",
    )(lx, h0, cw, cb.reshape(1, ch), wgate, bgate.reshape(1, 2 * ch), lam.reshape(1, ch), *extra)


ROW_STRIDE = GRID_W + 2 * BF16_ROWS


def _conv_row_body(x_ref, w_ref, b_ref, o_ref, pad_ref):
    n = x_ref.shape[1]
    nrows = n // GRID_W
    gap = jnp.zeros((BF16_ROWS, LANES), F32)

    def fill(r, c):
        base = pl.multiple_of(r * ROW_STRIDE, SUBLANES)
        src = pl.multiple_of(r * GRID_W, BF16_ROWS)
        pad_ref[pl.ds(base, BF16_ROWS), :] = gap
        pad_ref[pl.ds(base + BF16_ROWS, GRID_W), :] = x_ref[0, pl.ds(src, GRID_W), :].astype(F32)
        pad_ref[pl.ds(base + BF16_ROWS + GRID_W, BF16_ROWS), :] = gap
        return c

    lax.fori_loop(0, nrows, fill, 0)

    def row(r, c):
        base = pl.multiple_of(r * ROW_STRIDE, SUBLANES)
        acc = jnp.broadcast_to(b_ref[...], (GRID_W, LANES))
        for k in range(CONV_WIDTH):
            acc = acc + w_ref[k:k + 1, :] * pad_ref[pl.ds(base + BF16_ROWS - CONV_PAD + k, GRID_W), :]
        o_ref[0, pl.ds(pl.multiple_of(r * GRID_W, SUBLANES), GRID_W), :] = acc
        return c

    lax.fori_loop(0, nrows, row, 0, unroll=2)


def _conv_col_body(x_ref, w_ref, b_ref, o_ref, pad_ref):
    n = x_ref.shape[1]
    nrows = n // GRID_W
    halo = CONV_PAD * GRID_W
    pad_ref[0:halo, :] = jnp.zeros((halo, LANES), F32)
    pad_ref[halo + n:halo + n + halo, :] = jnp.zeros((halo, LANES), F32)

    def fill(r, c):
        src = pl.multiple_of(r * GRID_W, BF16_ROWS)
        pad_ref[pl.ds(halo + src, GRID_W), :] = x_ref[0, pl.ds(src, GRID_W), :].astype(F32)
        return c

    lax.fori_loop(0, nrows, fill, 0)

    def row(r, c):
        base = pl.multiple_of(r * GRID_W, SUBLANES)
        acc = jnp.broadcast_to(b_ref[...], (GRID_W, LANES))
        for k in range(CONV_WIDTH):
            acc = acc + w_ref[k:k + 1, :] * pad_ref[pl.ds(base + k * GRID_W, GRID_W), :]
        o_ref[0, pl.ds(base, GRID_W), :] = acc
        return c

    lax.fori_loop(0, nrows, row, 0, unroll=2)


def _conv_grid(glu, w, b):
    bsz, n, ch = glu.shape
    half = ch // 2
    ng = half // LANES
    outs = []
    for body, first, pad_rows in ((_conv_row_body, 0, (n // GRID_W) * ROW_STRIDE),
                                  (_conv_col_body, ng, n + 2 * CONV_PAD * GRID_W)):
        outs.append(pl.pallas_call(
            body,
            grid=(bsz, ng),
            in_specs=[pl.BlockSpec((1, n, LANES), lambda bb, g, first=first: (bb, 0, g + first)),
                      pl.BlockSpec((CONV_WIDTH, LANES), lambda bb, g, first=first: (0, g + first)),
                      pl.BlockSpec((1, LANES), lambda bb, g, first=first: (0, g + first))],
            out_specs=pl.BlockSpec((1, n, LANES), lambda bb, g: (bb, 0, g)),
            out_shape=jax.ShapeDtypeStruct((bsz, n, half), F32),
            scratch_shapes=[pltpu.VMEM((pad_rows, LANES), F32)],
            compiler_params=_params("arbitrary", "arbitrary"),
            name="conv_row" if first == 0 else "conv_col",
        )(glu, w, b.reshape(1, ch)))
    return outs


def _out_body(cr_ref, cc_ref, lng_ref, lnb_ref, yl_ref, wo_ref, bo_ref, x_ref,
              g1_ref, n2g_ref, sh2_ref, sc2_ref, rw_ref, x1_ref, v_ref, aff_ref):
    cv = jnp.concatenate([cr_ref[0], cc_ref[0]], axis=-1)
    mu = jnp.mean(cv, axis=-1, keepdims=True)
    dv = cv - mu
    var = jnp.mean(dv * dv, axis=-1, keepdims=True)
    yn = dv * lax.rsqrt(var + LN_EPS) * lng_ref[...] + lnb_ref[...]
    cy = yn * jax.nn.sigmoid(yn)
    cat = jnp.concatenate([cy.astype(BF16), yl_ref[0]], axis=-1)
    m = jnp.dot(cat, wo_ref[...], preferred_element_type=F32) + bo_ref[...]
    x1 = x_ref[0] + g1_ref[0] * m
    x1_ref[0] = x1
    ms = jnp.mean(x1 * x1, axis=-1, keepdims=True)
    v = x1 * lax.rsqrt(ms + RMS_EPS) * n2g_ref[...]
    v = v * (1.0 + sc2_ref[0]) + sh2_ref[0]
    v_ref[0] = v.astype(BF16)
    v_hi = v.astype(BF16)
    v_lo = (v - v_hi.astype(F32)).astype(BF16)
    rw = rw_ref[...]
    rw_hi = rw.astype(BF16)
    rw_lo = (rw - rw_hi.astype(F32)).astype(BF16)
    nt = (((1,), (1,)), ((), ()))
    ne = rw.shape[0]
    both = lax.dot_general(jnp.concatenate([rw_hi, rw_lo], axis=0), v_hi, nt, preferred_element_type=F32)
    lg = both[:ne] + both[ne:] + lax.dot_general(rw_hi, v_lo, nt, preferred_element_type=F32)
    ex = jnp.exp(lg - jnp.max(lg, axis=0, keepdims=True))
    aff_ref[0] = ex / jnp.sum(ex, axis=0, keepdims=True)


def _out_proj_route(conv_r, conv_c, ln_g, ln_b, yl, wo_bf16, b_out, x, g1, n2g, sh2, sc2, rw_t, tm):
    bsz, n, d = x.shape
    half = conv_r.shape[2]
    ch = yl.shape[2]
    ne = rw_t.shape[0]
    tok = lambda c: pl.BlockSpec((1, tm, c), lambda b, i: (b, i, 0))
    full = lambda shape: pl.BlockSpec(shape, lambda b, i: (0,) * len(shape))
    per_b = pl.BlockSpec((1, 1, d), lambda b, i: (b, 0, 0))
    return pl.pallas_call(
        _out_body,
        grid=(bsz, n // tm),
        in_specs=[tok(half), tok(half), full((1, 2 * half)), full((1, 2 * half)),
                  tok(ch), full((2 * half + ch, d)), full((1, d)), tok(d),
                  per_b, full((1, d)), per_b, per_b, full((ne, d))],
        out_specs=[tok(d), tok(d), pl.BlockSpec((1, ne, tm), lambda b, i: (b, 0, i))],
        out_shape=[jax.ShapeDtypeStruct((bsz, n, d), F32),
                   jax.ShapeDtypeStruct((bsz, n, d), BF16),
                   jax.ShapeDtypeStruct((bsz, ne, n), F32)],
        compiler_params=_params("arbitrary", "arbitrary"),
        name="out_proj_route",
    )(conv_r, conv_c, ln_g.reshape(1, -1), ln_b.reshape(1, -1), yl, wo_bf16,
      b_out.reshape(1, d), x, g1, n2g.reshape(1, d), sh2, sc2, rw_t)


def _token_cumsum(m, rows_per_expert):
    er = m.shape[0]
    li = lax.broadcasted_iota(jnp.int32, (LANES, LANES), 0)
    lj = lax.broadcasted_iota(jnp.int32, (LANES, LANES), 1)
    upper = jnp.where(li <= lj, 1.0, 0.0).astype(BF16)
    cs = jnp.dot(m.astype(BF16), upper, preferred_element_type=F32)
    rt = jnp.broadcast_to(cs[:, LANES - 1:LANES], (er, LANES)).astype(BF16)
    ri = lax.broadcasted_iota(jnp.int32, (er, er), 0)
    ci = lax.broadcasted_iota(jnp.int32, (er, er), 1)
    same = (ri // rows_per_expert) == (ci // rows_per_expert)
    lower = jnp.where(same & (ci < ri), 1.0, 0.0).astype(BF16)
    rstart = jnp.dot(lower, rt, preferred_element_type=F32)
    return cs, rstart


def _topk_body(aff_ref, gate_ref, pos_ref, rs_ref, cs_scr, rs_scr, *, cap):
    aff = aff_ref[0]
    ne, rows, _ = aff.shape

    def bisect(i, thr):
        cand = thr | jnp.left_shift(jnp.int32(1), 30 - i)
        cnt = jnp.sum((aff >= pltpu.bitcast(cand, F32)).astype(jnp.int32), axis=(1, 2), keepdims=True)
        return jnp.where(cnt >= cap, cand, thr)

    thr = lax.fori_loop(0, 31, bisect, jnp.zeros((ne, 1, 1), jnp.int32))
    gt = aff >= pltpu.bitcast(thr + 1, F32)
    eq = (aff >= pltpu.bitcast(thr, F32)) & jnp.logical_not(gt)
    need = (cap - jnp.sum(gt.astype(jnp.int32), axis=(1, 2), keepdims=True)).astype(F32)
    eqf = jnp.where(eq, 1.0, 0.0)
    rank_in_row, rank_row0 = _token_cumsum(eqf.reshape(ne * rows, LANES), rows)
    rank_excl = (rank_in_row + rank_row0).reshape(ne, rows, LANES) - eqf
    sel = gt | (eq & (rank_excl < need))
    self_ = jnp.where(sel, 1.0, 0.0)
    cs, rstart = _token_cumsum(self_.reshape(ne * rows, LANES), rows)
    cs3 = cs.reshape(ne, rows, LANES)
    rstart3 = rstart.reshape(ne, rows, LANES)
    cs_scr[...] = cs3
    rs_scr[...] = rstart3
    pos_ref[0] = jnp.where(sel, cs3 + rstart3 - 1.0, -1.0).astype(jnp.int32)
    rs_ref[0] = rstart3.astype(jnp.int32)

    jrow = lax.broadcasted_iota(jnp.int32, (1, cap), 1).astype(F32)
    sub_r = lax.broadcasted_iota(jnp.int32, (rows, cap), 0).astype(F32)
    sub_l = lax.broadcasted_iota(jnp.int32, (LANES, cap), 0).astype(F32)

    def per_expert(e, c):
        cl = cs_scr[e]
        af = aff_ref[0, e]
        rowtot = cl[:, LANES - 1:LANES]
        before = rowtot + rs_scr[e][:, 0:1] <= jrow
        rj = jnp.sum(jnp.where(before, 1.0, 0.0), axis=0, keepdims=True)
        rowbase = jnp.sum(jnp.where(before, rowtot, 0.0), axis=0, keepdims=True)
        onehot = jnp.where(sub_r == rj, 1.0, 0.0).astype(BF16)
        a1 = af.astype(BF16)
        r1 = af - a1.astype(F32)
        a2 = r1.astype(BF16)
        a3 = (r1 - a2.astype(F32)).astype(BF16)
        lhs = jnp.concatenate([cl.astype(BF16), a1, a2, a3], axis=1)
        gathered = lax.dot_general(lhs, onehot, (((0,), (0,)), ((), ())), preferred_element_type=F32)
        g = gathered[:LANES]
        ga = gathered[LANES:2 * LANES] + gathered[2 * LANES:3 * LANES] + gathered[3 * LANES:]
        lanepos = jnp.sum(jnp.where(g <= jrow - rowbase, 1.0, 0.0), axis=0, keepdims=True)
        gate = jnp.sum(jnp.where(sub_l == lanepos, ga, 0.0), axis=0, keepdims=True)
        gate_ref[0, pl.ds(e, 1), :] = gate
        return c

    lax.fori_loop(0, ne, per_expert, 0)


def _topk_route(aff4, cap):
    bsz, ne, rows, _ = aff4.shape
    blk4 = pl.BlockSpec((1, ne, rows, LANES), lambda b: (b, 0, 0, 0))
    lst = pl.BlockSpec((1, ne, cap), lambda b: (b, 0, 0))
    return pl.pallas_call(
        functools.partial(_topk_body, cap=cap),
        grid=(bsz,),
        in_specs=[blk4],
        out_specs=[lst, blk4, blk4],
        out_shape=[jax.ShapeDtypeStruct((bsz, ne, cap), F32),
                   jax.ShapeDtypeStruct((bsz, ne, rows, LANES), jnp.int32),
                   jax.ShapeDtypeStruct((bsz, ne, rows, LANES), jnp.int32)],
        scratch_shapes=[pltpu.VMEM((ne, rows, LANES), F32), pltpu.VMEM((ne, rows, LANES), F32)],
        compiler_params=_params("arbitrary"),
        name="topk_route",
    )(aff4)


PACK_TOK = 256
PACK_SLAB = 64
PACK_SHIFT = 6


def _pack_body(rs_ref, pos_ref, v_ref, xs_hbm, stage, stage_x, pend, sems, sem_x, *, cap):
    b = pl.program_id(0)
    s = pl.program_id(1)
    nb = pl.num_programs(0)
    ns = pl.num_programs(1)
    ne = pos_ref.shape[2]
    tpr = PACK_TOK // LANES
    nr = ns * tpr
    step = b * ns + s
    slot = step % 2
    slab = PACK_SLAB

    @pl.when(s == 0)
    def _():
        pend[...] = jnp.zeros(pend.shape, BF16)

    def first_pos(rr, e):
        return rs_ref[(b * ne + e) * (nr + 1) + rr]

    def floor16(p):
        return lax.shift_left(lax.shift_right_logical(p, 4), 4)

    p0 = [first_pos(s * tpr, e) for e in range(ne)]
    p1 = [first_pos(s * tpr + tpr, e) for e in range(ne)]
    a0 = [floor16(p0[e]) for e in range(ne)]
    a1 = [floor16(p1[e]) for e in range(ne)]
    has = [a1[e] < p1[e] for e in range(ne)]
    prnd = [lax.shift_right_logical(a1[e] - a0[e], PACK_SHIFT) for e in range(ne)]
    poff = [pl.multiple_of(jnp.bitwise_and(a1[e] - a0[e], slab - 1), BF16_ROWS) for e in range(ne)]

    vb = v_ref[0]
    pos = [jnp.concatenate([pos_ref[0, j, e:e + 1, :] for j in range(tpr)], axis=1) for e in range(ne)]
    kiota = lax.broadcasted_iota(jnp.int32, (slab, PACK_TOK), 0)

    def onehot(rnd):
        blocks = [jnp.where(kiota == pos[e] - (a0[e] + rnd * slab), 1.0, 0.0) for e in range(ne)]
        return jnp.concatenate(blocks, axis=0).astype(BF16)

    def keep_partial(buf, e, rnd, old):
        grp = buf[pl.ds(e * slab + poff[e], BF16_ROWS), :]
        return jnp.where(jnp.logical_and(has[e], prnd[e] == rnd), grp, old)

    cur = stage.at[slot]
    cur[...] = jnp.dot(onehot(0), vb, preferred_element_type=F32).astype(BF16)
    for e in range(ne):
        cur[e * slab:e * slab + BF16_ROWS, :] += pend[e]
    for e in range(ne):
        pend[e] = keep_partial(cur, e, 0, jnp.zeros((BF16_ROWS, vb.shape[1]), BF16))

    def slab_bytes_wait(buf, sem):
        pltpu.make_async_copy(buf, buf, sem).wait()

    @pl.when(step > 0)
    def _():
        slab_bytes_wait(stage.at[1 - slot], sems.at[1 - slot])

    @pl.when(s == 0)
    def _():
        stage_x[0:slab, :] = jnp.zeros((slab, stage_x.shape[1]), BF16)
        for e in range(ne):
            pltpu.make_async_copy(stage_x.at[0:slab], xs_hbm.at[e, b, pl.ds(cap, slab)], sem_x).start()
        slab_bytes_wait(stage_x, sem_x)

    for e in range(ne):
        pltpu.make_async_copy(stage.at[slot, e * slab:(e + 1) * slab],
                              xs_hbm.at[e, b, pl.ds(pl.multiple_of(a0[e], BF16_ROWS), slab)], sems.at[slot]).start()

    span = p1[0] - a0[0]
    for e in range(1, ne):
        span = jnp.maximum(span, p1[e] - a0[e])
    rounds = jnp.maximum(1, lax.shift_right_logical(span + (slab - 1), PACK_SHIFT))

    def extra(rnd, carry):
        stage_x[...] = jnp.dot(onehot(rnd), vb, preferred_element_type=F32).astype(BF16)
        for e in range(ne):
            pend[e] = keep_partial(stage_x, e, rnd, pend[e])
        for e in range(ne):
            @pl.when(p1[e] - a0[e] > rnd * slab)
            def _(e=e):
                cp = pltpu.make_async_copy(
                    stage_x.at[e * slab:(e + 1) * slab],
                    xs_hbm.at[e, b, pl.ds(pl.multiple_of(a0[e] + rnd * slab, BF16_ROWS), slab)], sem_x)
                cp.start()
                cp.wait()
        return carry

    lax.fori_loop(1, rounds, extra, 0)

    @pl.when(step == nb * ns - 1)
    def _():
        slab_bytes_wait(stage.at[slot], sems.at[slot])


def _pack_tokens(rs_ext, pos_r, v, cap):
    bsz, n, d = v.shape
    ne = pos_r.shape[2]
    tpr = PACK_TOK // LANES
    return pl.pallas_call(
        functools.partial(_pack_body, cap=cap),
        grid_spec=pltpu.PrefetchScalarGridSpec(
            num_scalar_prefetch=1,
            grid=(bsz, n // PACK_TOK),
            in_specs=[pl.BlockSpec((1, tpr, ne, LANES), lambda b, s, rs: (b, s, 0, 0)),
                      pl.BlockSpec((1, PACK_TOK, d), lambda b, s, rs: (b, s, 0))],
            out_specs=pl.BlockSpec(memory_space=pl.ANY),
            scratch_shapes=[pltpu.VMEM((2, ne * PACK_SLAB, d), BF16),
                            pltpu.VMEM((ne * PACK_SLAB, d), BF16),
                            pltpu.VMEM((ne, BF16_ROWS, d), BF16),
                            pltpu.SemaphoreType.DMA((2,)),
                            pltpu.SemaphoreType.DMA(())]),
        out_shape=jax.ShapeDtypeStruct((ne, bsz, cap + PACK_SLAB, d), BF16),
        compiler_params=_params("arbitrary", "arbitrary"),
        name="pack_tokens",
    )(rs_ext, pos_r, v)


def _ffn_body(x_ref, gate_ref, wg_ref, wu_ref, wd_ref, ys_hbm, acc, stage, osems, *, mg, nf, sub, och):
    e = pl.program_id(0)
    f = pl.program_id(1)
    spg = x_ref.shape[2]

    @pl.when(f == 0)
    def _():
        acc[...] = jnp.zeros(acc.shape, F32)

    wgb = wg_ref[0].astype(BF16)
    wub = wu_ref[0].astype(BF16)
    wdb = wd_ref[0].astype(BF16)
    for i in range(mg // sub):
        rows = slice(i * sub, (i + 1) * sub)
        x = x_ref[0, (i * sub) // spg, (i * sub) % spg:(i * sub) % spg + sub, :]
        gg = jnp.dot(x, wgb, preferred_element_type=F32)
        uu = jnp.dot(x, wub, preferred_element_type=F32)
        h = (gg * jax.nn.sigmoid(gg) * uu).astype(BF16)
        acc[rows, :] += jnp.dot(h, wdb, preferred_element_type=F32)

    @pl.when(f == nf - 1)
    def _():
        gate_t = gate_ref[0].T

        def out_copy(c):
            return pltpu.make_async_copy(stage.at[c % 2], ys_hbm.at[e, pl.ds(c * och, och)], osems.at[c % 2])

        nchunks = mg // och
        for c in range(nchunks):
            if c >= 2:
                out_copy(c - 2).wait()
            for i in range(och // LANES):
                blk = c * (och // LANES) + i
                rows = slice(blk * LANES, (blk + 1) * LANES)
                stage[c % 2, i * LANES:(i + 1) * LANES, :] = (acc[rows, :] * gate_t[:, blk:blk + 1]).astype(BF16)
            out_copy(c).start()
        for c in range(max(nchunks - 2, 0), nchunks):
            out_copy(c).wait()


def _expert_ffn(xs, gate_rows, wg, wu, wd, tf):
    ne, d, fdim = wg.shape
    bsz, cap = xs.shape[1], gate_rows.shape[1] * LANES // xs.shape[1]
    mg = bsz * cap
    nf = fdim // tf
    sub = cap
    och = min(mg, 512)
    return pl.pallas_call(
        functools.partial(_ffn_body, mg=mg, nf=nf, sub=sub, och=och),
        grid_spec=pltpu.PrefetchScalarGridSpec(
            num_scalar_prefetch=0,
            grid=(ne, nf),
            in_specs=[pl.BlockSpec((1, bsz, cap, d), lambda e, f: (e, 0, 0, 0)),
                      pl.BlockSpec((1, mg // LANES, LANES), lambda e, f: (e, 0, 0)),
                      pl.BlockSpec((1, d, tf), lambda e, f: (e, 0, f)),
                      pl.BlockSpec((1, d, tf), lambda e, f: (e, 0, f)),
                      pl.BlockSpec((1, tf, d), lambda e, f: (e, f, 0))],
            out_specs=pl.BlockSpec(memory_space=pl.ANY),
            scratch_shapes=[pltpu.VMEM((mg, d), F32),
                            pltpu.VMEM((2, och, d), BF16),
                            pltpu.SemaphoreType.DMA((2,))]),
        out_shape=jax.ShapeDtypeStruct((ne, mg, d), BF16),
        compiler_params=_params("arbitrary", "arbitrary"),
        name="expert_ffn",
    )(xs, gate_rows, wg, wu, wd)


SLAB = PACK_SLAB
TILE_ROWS = PACK_TOK // LANES


def _combine_body(rs_ref, pos_ref, ys_hbm, x1_ref, g5_ref, fng_ref, o_ref, stk, stk_x, sems, sem_x, *, cap, tps):
    b = pl.program_id(0)
    s = pl.program_id(1)
    nb = pl.num_programs(0)
    ns = pl.num_programs(1)
    ne = pos_ref.shape[2]
    nr = ns * tps * TILE_ROWS
    tok = TILE_ROWS * LANES
    step = b * ns + s
    slot = step % 2
    tile_rows = ne * SLAB

    def first_pos(bb, rr, e):
        return rs_ref[(bb * ne + e) * (nr + 1) + rr]

    def aligned(p0):
        return lax.shift_left(lax.shift_right_logical(p0, 4), 4)

    def slab_start(a0, rnd):
        return pl.multiple_of(jnp.minimum(a0 + rnd * SLAB, cap - SLAB), BF16_ROWS)

    def slab_copy(bb, e, a, dst, row, sem):
        return pltpu.make_async_copy(ys_hbm.at[e, pl.ds(bb * cap + a, SLAB)], dst.at[pl.ds(row, SLAB)], sem)

    def issue(bb, ss, sl):
        for j in range(tps):
            for e in range(ne):
                a = slab_start(aligned(first_pos(bb, (ss * tps + j) * TILE_ROWS, e)), 0)
                slab_copy(bb, e, a, stk.at[sl], j * tile_rows + e * SLAB, sems.at[sl]).start()

    @pl.when(step == 0)
    def _():
        issue(b, s, slot)

    @pl.when(step + 1 < nb * ns)
    def _():
        nxt = step + 1
        issue(nxt // ns, nxt % ns, 1 - slot)

    kiota = lax.broadcasted_iota(jnp.int32, (SLAB, tok), 0)
    tn = (((0,), (0,)), ((), ()))
    pltpu.make_async_copy(stk.at[slot], stk.at[slot], sems.at[slot]).wait()

    for j in range(tps):
        r = (s * tps + j) * TILE_ROWS
        pos = jnp.concatenate([pos_ref[0, j * TILE_ROWS + q] for q in range(TILE_ROWS)], axis=1)
        a0 = [aligned(first_pos(b, r, e)) for e in range(ne)]

        def onehot(rnd, pos=pos, a0=a0):
            blocks = []
            for e in range(ne):
                rel = pos[e:e + 1, :] - slab_start(a0[e], rnd)
                fresh = (pos[e:e + 1, :] - a0[e]) >= rnd * SLAB
                blocks.append(jnp.where((kiota == rel) & fresh, 1.0, 0.0))
            return jnp.concatenate(blocks, axis=0).astype(BF16)

        moe = lax.dot_general(onehot(0), stk[slot, j * tile_rows:(j + 1) * tile_rows, :], tn,
                              preferred_element_type=F32)

        span = first_pos(b, r + TILE_ROWS, 0) - a0[0]
        for e in range(1, ne):
            span = jnp.maximum(span, first_pos(b, r + TILE_ROWS, e) - a0[e])
        rounds = jnp.maximum(1, lax.shift_right_logical(span + (SLAB - 1), PACK_SHIFT))

        def extra(rnd, m, a0=a0, onehot=onehot):
            for e in range(ne):
                slab_copy(b, e, slab_start(a0[e], rnd), stk_x, e * SLAB, sem_x).start()
            pltpu.make_async_copy(stk_x, stk_x, sem_x).wait()
            return m + lax.dot_general(onehot(rnd), stk_x[...], tn, preferred_element_type=F32)

        moe = lax.fori_loop(1, rounds, extra, moe)

        y = x1_ref[0, j * tok:(j + 1) * tok, :] + g5_ref[0] * moe
        ms = jnp.mean(y * y, axis=-1, keepdims=True)
        o_ref[0, j * tok:(j + 1) * tok, :] = y * lax.rsqrt(ms + RMS_EPS) * fng_ref[...]


def _combine_norm(rs_ext, pos_r, ys, x1, g5, fng, cap, tps):
    bsz, n, d = x1.shape
    ns = n // (PACK_TOK * tps)
    ne = pos_r.shape[2]
    return pl.pallas_call(
        functools.partial(_combine_body, cap=cap, tps=tps),
        grid_spec=pltpu.PrefetchScalarGridSpec(
            num_scalar_prefetch=1,
            grid=(bsz, ns),
            in_specs=[pl.BlockSpec((1, tps * TILE_ROWS, ne, LANES), lambda b, s, rs: (b, s, 0, 0)),
                      pl.BlockSpec(memory_space=pl.ANY),
                      pl.BlockSpec((1, tps * PACK_TOK, d), lambda b, s, rs: (b, s, 0)),
                      pl.BlockSpec((1, 1, d), lambda b, s, rs: (b, 0, 0)),
                      pl.BlockSpec((1, d), lambda b, s, rs: (0, 0))],
            out_specs=pl.BlockSpec((1, tps * PACK_TOK, d), lambda b, s, rs: (b, s, 0)),
            scratch_shapes=[pltpu.VMEM((2, tps * ne * SLAB, d), BF16),
                            pltpu.VMEM((ne * SLAB, d), BF16),
                            pltpu.SemaphoreType.DMA((2,)),
                            pltpu.SemaphoreType.DMA(())]),
        out_shape=jax.ShapeDtypeStruct((bsz, n, d), F32),
        compiler_params=_params("arbitrary", "arbitrary"),
        name="combine_norm",
    )(rs_ext, pos_r, ys, x1, g5, fng.reshape(1, d))


def _block_diag(w):
    heads, hd, _ = w.shape
    eye = jnp.eye(heads, dtype=w.dtype)
    return (eye[:, None, :, None] * w[:, :, None, :]).reshape(heads * hd, heads * hd)


def _tile(n, pref):
    return pref if n % pref == 0 else n


def kernel(x, c, ctx, c_ctx, norm1_g, norm2_g, ada_w, ada_b, w_in, b_in, conv_dw_w, conv_dw_b, conv_ln_g, conv_ln_b, lru_conv_w, lru_conv_b, lru_wa, lru_ba, lru_wi, lru_bi, lru_lambda, w_out, b_out, router_w, exp_w_gate, exp_w_up, exp_w_down, final_norm_g):
    assert norm1_g.shape[0] == 1
    mod, x1, v, aff = _mixer(x, c, ctx, c_ctx, norm1_g[0], norm2_g[0], ada_w[0], ada_b[0], w_in[0], b_in[0],
                             conv_dw_w[0], conv_dw_b[0], conv_ln_g[0], conv_ln_b[0], lru_conv_w[0],
                             lru_conv_b[0], lru_wa[0], lru_ba[0], lru_wi[0], lru_bi[0], lru_lambda[0],
                             w_out[0], b_out[0], router_w[0])
    return _moe_norm(x1, v, aff, mod[5], exp_w_gate[0], exp_w_up[0], exp_w_down[0], final_norm_g)


def _mixer(x, c, ctx, c_ctx, norm1_g, norm2_g, ada_w, ada_b, w_in, b_in, conv_w, conv_b, ln_g, ln_b,
           lru_cw, lru_cb, lru_wa, lru_ba, lru_wi, lru_bi, lru_lam, w_out, b_out, router_w):
    bsz, n, d = x.shape
    cond8 = jnp.zeros((SUBLANES, d), F32).at[:bsz].set(c).at[bsz].set(c_ctx)
    mods = _ada_mod(cond8, ada_w, ada_b)
    mod = [mods[:bsz, k * d:(k + 1) * d].reshape(bsz, 1, d) for k in range(N_MOD)]
    mod_c = [jnp.broadcast_to(mods[bsz:bsz + 1, k * d:(k + 1) * d].reshape(1, 1, d), (bsz, 1, d))
             for k in range(2)]
    w_in_b = w_in.astype(BF16)
    w_out_b = w_out.astype(BF16)
    ch = lru_cb.shape[1]
    wgate = [jnp.concatenate([_block_diag(lru_wa[dd]), _block_diag(lru_wi[dd])], axis=1).astype(BF16)
             for dd in range(2)]
    bgate = [jnp.concatenate([lru_ba[dd], lru_bi[dd]]) for dd in range(2)]

    def lru(lx, h0, dd, reverse, merge_with=None):
        return _lru_dir(lx, h0, lru_cw[dd], lru_cb[dd], wgate[dd], bgate[dd], lru_lam[dd], reverse,
                        _tile(lx.shape[1], 256), merge_with)

    _, c_lx, _ = _in_proj(ctx, norm1_g, mod_c[0], mod_c[1], w_in_b, b_in, _tile(ctx.shape[1], 256))
    zero_h = jnp.zeros((bsz, 1, ch), F32)
    _, hf0 = lru(c_lx, zero_h, 0, False)
    _, hb0 = lru(c_lx, zero_h, 1, True)

    x_glu, x_lx, x_glg = _in_proj(x, norm1_g, mod[0], mod[1], w_in_b, b_in, _tile(n, 1024))
    hb, _ = lru(x_lx, hb0, 1, True)
    yl, _ = lru(x_lx, hf0, 0, False, merge_with=(hb, x_glg))
    conv_r, conv_c = _conv_grid(x_glu, conv_w, conv_b)
    x1, v, aff = _out_proj_route(conv_r, conv_c, ln_g, ln_b, yl, w_out_b, b_out, x, mod[2],
                                 norm2_g, mod[3], mod[4], router_w.T, _tile(n, 512))
    return mod, x1, v, aff


def _moe_norm(x1, v, aff, gate2, wg, wu, wd, final_norm_g):
    bsz, n, d = x1.shape
    ne = aff.shape[1]
    cap = EC_CAPACITY * n // ne
    gate, pos, rstart = _topk_route(aff.reshape(bsz, ne, n // LANES, LANES), cap)
    gate_rows = jnp.swapaxes(gate, 0, 1).reshape(ne, bsz * cap // LANES, LANES)
    rs_ext = jnp.concatenate([rstart[..., 0], jnp.full((bsz, ne, 1), cap, jnp.int32)], axis=-1).reshape(-1)
    pos_r = jnp.swapaxes(pos, 1, 2)
    xs = _pack_tokens(rs_ext, pos_r, v, cap)
    ys = _expert_ffn(xs, gate_rows, wg, wu, wd, 256)
    return _combine_norm(rs_ext, pos_r, ys, x1, gate2, final_norm_g, cap, 2)
```

```python
import functools

import jax
import jax.numpy as jnp
from jax import lax
from jax.experimental import pallas as pl
from jax.experimental.pallas import tpu as pltpu

GRID_W = 64
CONV_WIDTH = 31
CONV_PAD = (CONV_WIDTH - 1) // 2
LRU_CONV_WIDTH = 4
LRU_C = 8.0
N_EXPERTS = 16
EC_CAPACITY = 2
N_MOD = 6
RMS_EPS = 1e-6
LN_EPS = 1e-5

LANES = 128
SUBLANES = 8
BF16_ROWS = 16
VMEM_LIMIT = 56 * 1024 * 1024

F32 = jnp.float32
BF16 = jnp.bfloat16
HIGHEST = lax.Precision.HIGHEST


def _params(*sem):
    return pltpu.CompilerParams(dimension_semantics=sem, vmem_limit_bytes=VMEM_LIMIT)


def _ada_body(c_ref, w_ref, b_ref, o_ref):
    s = c_ref[...]
    s = s * jax.nn.sigmoid(s)
    o_ref[...] = jnp.dot(s, w_ref[...], precision=HIGHEST, preferred_element_type=F32) + b_ref[...]


def _ada_mod(cond8, ada_w, ada_b):
    d, n = ada_w.shape
    tn = n // 4
    return pl.pallas_call(
        _ada_body,
        grid=(n // tn,),
        in_specs=[pl.BlockSpec((SUBLANES, d), lambda j: (0, 0)),
                  pl.BlockSpec((d, tn), lambda j: (0, j)),
                  pl.BlockSpec((1, tn), lambda j: (0, j))],
        out_specs=pl.BlockSpec((SUBLANES, tn), lambda j: (0, j)),
        out_shape=jax.ShapeDtypeStruct((SUBLANES, n), F32),
        compiler_params=_params("arbitrary"),
        name="ada_mod",
    )(cond8, ada_w, ada_b.reshape(1, n))


def _inproj_body(x_ref, g_ref, sh_ref, sc_ref, w_ref, b_ref, glu_ref, lx_ref, glg_ref, *, rb):
    gain = g_ref[...] * (1.0 + sc_ref[0])
    for i in range(x_ref.shape[1] // rb):
        rows = slice(i * rb, (i + 1) * rb)
        x = x_ref[0, rows, :]
        ms = jnp.mean(x * x, axis=-1, keepdims=True)
        u = x * lax.rsqrt(ms + RMS_EPS) * gain + sh_ref[0]
        p = jnp.dot(u.astype(BF16), w_ref[...], preferred_element_type=F32) + b_ref[...]
        cc = p.shape[1] // 4
        glu_ref[0, rows, :] = p[:, :cc] * jax.nn.sigmoid(p[:, cc:2 * cc])
        lx_ref[0, rows, :] = p[:, 2 * cc:3 * cc]
        glg_ref[0, rows, :] = jax.nn.gelu(p[:, 3 * cc:])


def _in_proj(x, norm_g, shift, scale, w_bf16, b_in, tm):
    bsz, n, d = x.shape
    n4 = w_bf16.shape[1]
    cc = n4 // 4
    tok = pl.BlockSpec((1, tm, cc), lambda b, i: (b, i, 0))
    return pl.pallas_call(
        functools.partial(_inproj_body, rb=min(tm, 256)),
        grid=(bsz, n // tm),
        in_specs=[pl.BlockSpec((1, tm, d), lambda b, i: (b, i, 0)),
                  pl.BlockSpec((1, d), lambda b, i: (0, 0)),
                  pl.BlockSpec((1, 1, d), lambda b, i: (b, 0, 0)),
                  pl.BlockSpec((1, 1, d), lambda b, i: (b, 0, 0)),
                  pl.BlockSpec((d, n4), lambda b, i: (0, 0)),
                  pl.BlockSpec((1, n4), lambda b, i: (0, 0))],
        out_specs=[tok, tok, tok],
        out_shape=[jax.ShapeDtypeStruct((bsz, n, cc), F32)] * 3,
        compiler_params=_params("arbitrary", "arbitrary"),
        name="in_proj",
    )(x, norm_g.reshape(1, d), shift, scale, w_bf16, b_in.reshape(1, n4))


def _lru_body(*refs, reverse, tl, merge):
    x_ref, h0_ref, cw_ref, cb_ref, wg_ref, bg_ref, lam_ref = refs[:7]
    if merge:
        hother_ref, glg_ref = refs[7:9]
        refs = refs[2:]
    h_ref, hl_ref, a_ref, u_ref, carry_ref, halo_ref = refs[7:]
    i = pl.program_id(1)
    ch = x_ref.shape[2]

    @pl.when(i == 0)
    def _():
        carry_ref[...] = jnp.broadcast_to(h0_ref[0], carry_ref.shape)
        halo_ref[...] = jnp.zeros(halo_ref.shape, F32)

    x = x_ref[0]
    halo = halo_ref[...]
    row8 = lax.broadcasted_iota(jnp.int32, (SUBLANES, ch), 0)
    xc = cb_ref[...] + cw_ref[LRU_CONV_WIDTH - 1:LRU_CONV_WIDTH, :] * x
    for j in range(LRU_CONV_WIDTH - 1):
        s = LRU_CONV_WIDTH - 1 - j
        if not reverse:
            rolled = pltpu.roll(x, s, 0)
            edge = jnp.where(row8 < s, pltpu.roll(halo, s, 0), rolled[0:SUBLANES, :])
            tap = jnp.concatenate([edge, rolled[SUBLANES:, :]], axis=0)
        else:
            rolled = pltpu.roll(x, tl - s, 0)
            edge = jnp.where(row8 >= SUBLANES - s, pltpu.roll(halo, SUBLANES - s, 0), rolled[tl - SUBLANES:, :])
            tap = jnp.concatenate([rolled[:tl - SUBLANES, :], edge], axis=0)
        xc = xc + cw_ref[j:j + 1, :] * tap
    halo_ref[...] = x[0:SUBLANES, :] if reverse else x[tl - SUBLANES:tl, :]

    z = jnp.dot(xc.astype(BF16), wg_ref[...], preferred_element_type=F32) + bg_ref[...]
    half_c = (-0.5 * LRU_C) * jax.nn.softplus(-lam_ref[...])
    log_a = half_c * jnp.tanh(0.5 * z[:, :ch]) + half_c
    ig = 0.5 * jnp.tanh(0.5 * z[:, ch:]) + 0.5
    th = jnp.tanh(log_a)
    a_ref[...] = jnp.exp(log_a)
    m2 = -2.0 * th / (1.0 - th)
    u_ref[...] = jnp.where(m2 > 0.0, m2 * lax.rsqrt(m2), 0.0) * (ig * xc)

    ng = tl // SUBLANES
    rowid = lax.broadcasted_iota(jnp.int32, (SUBLANES, ch), 0)

    def group(gi, h):
        g = (ng - 1 - gi) if reverse else gi
        off = pl.multiple_of(g * SUBLANES, SUBLANES)
        a = a_ref[pl.ds(off, SUBLANES), :]
        u = u_ref[pl.ds(off, SUBLANES), :]
        for s in (1, 2, 4):
            if reverse:
                m = rowid < SUBLANES - s
                sh = SUBLANES - s
            else:
                m = rowid >= s
                sh = s
            ap = jnp.where(m, pltpu.roll(a, sh, 0), 1.0)
            up = jnp.where(m, pltpu.roll(u, sh, 0), 0.0)
            u = a * up + u
            a = a * ap
        hh = a * h + u
        if merge:
            u_ref[pl.ds(off, SUBLANES), :] = hh
        else:
            h_ref[0, pl.ds(off, SUBLANES), :] = hh
        edge = hh[0:1, :] if reverse else hh[SUBLANES - 1:SUBLANES, :]
        return jnp.broadcast_to(edge, (SUBLANES, ch))

    h = lax.fori_loop(0, ng, group, carry_ref[...], unroll=8)
    carry_ref[...] = h
    hl_ref[0] = h[0:1, :]
    if merge:
        h_ref[0] = ((u_ref[...] + hother_ref[0]) * glg_ref[0]).astype(BF16)


def _lru_dir(lx, h0, cw, cb, wgate, bgate, lam, reverse, tl, merge_with=None):
    bsz, n, ch = lx.shape
    nc = n // tl
    cmap = (lambda b, i: (b, nc - 1 - i, 0)) if reverse else (lambda b, i: (b, i, 0))
    full = lambda shape: pl.BlockSpec(shape, lambda b, i: (0,) * len(shape))
    merge = merge_with is not None
    extra = list(merge_with) if merge else []
    return pl.pallas_call(
        functools.partial(_lru_body, reverse=reverse, tl=tl, merge=merge),
        grid=(bsz, nc),
        in_specs=[pl.BlockSpec((1, tl, ch), cmap),
                  pl.BlockSpec((1, 1, ch), lambda b, i: (b, 0, 0)),
                  full((LRU_CONV_WIDTH, ch)), full((1, ch)),
                  full((ch, 2 * ch)), full((1, 2 * ch)), full((1, ch))]
                 + [pl.BlockSpec((1, tl, ch), cmap)] * len(extra),
        out_specs=[pl.BlockSpec((1, tl, ch), cmap),
                   pl.BlockSpec((1, 1, ch), lambda b, i: (b, 0, 0))],
        out_shape=[jax.ShapeDtypeStruct((bsz, n, ch), BF16 if merge else F32),
                   jax.ShapeDtypeStruct((bsz, 1, ch), F32)],
        scratch_shapes=[pltpu.VMEM((tl, ch), F32),
                        pltpu.VMEM((tl, ch), F32),
                        pltpu.VMEM((SUBLANES, ch), F32),
                        pltpu.VMEM((SUBLANES, ch), F32)],
        compiler_params=_params("arbitrary", "arbitrary"),
        name="lru_rev" if reverse else "lru_fwd",
    )(lx, h0, cw, cb.reshape(1, ch), wgate, bgate.reshape(1, 2 * ch), lam.reshape(1, ch), *extra)


ROW_STRIDE = GRID_W + 2 * BF16_ROWS


def _conv_row_body(x_ref, w_ref, b_ref, o_ref, pad_ref):
    n = x_ref.shape[1]
    nrows = n // GRID_W
    gap = jnp.zeros((BF16_ROWS, LANES), F32)

    def fill(r, c):
        base = pl.multiple_of(r * ROW_STRIDE, SUBLANES)
        src = pl.multiple_of(r * GRID_W, SUBLANES)
        pad_ref[pl.ds(base, BF16_ROWS), :] = gap
        pad_ref[pl.ds(base + BF16_ROWS, GRID_W), :] = x_ref[0, pl.ds(src, GRID_W), :]
        pad_ref[pl.ds(base + BF16_ROWS + GRID_W, BF16_ROWS), :] = gap
        return c

    lax.fori_loop(0, nrows, fill, 0)

    def row(r, c):
        base = pl.multiple_of(r * ROW_STRIDE, SUBLANES)
        acc = jnp.broadcast_to(b_ref[...], (GRID_W, LANES))
        for k in range(CONV_WIDTH):
            acc = acc + w_ref[k:k + 1, :] * pad_ref[pl.ds(base + BF16_ROWS - CONV_PAD + k, GRID_W), :]
        o_ref[0, pl.ds(pl.multiple_of(r * GRID_W, SUBLANES), GRID_W), :] = acc
        return c

    lax.fori_loop(0, nrows, row, 0, unroll=2)


def _conv_col_body(x_ref, w_ref, b_ref, o_ref, pad_ref):
    n = x_ref.shape[1]
    nrows = n // GRID_W
    halo = CONV_PAD * GRID_W
    pad_ref[0:halo, :] = jnp.zeros((halo, LANES), F32)
    pad_ref[halo + n:halo + n + halo, :] = jnp.zeros((halo, LANES), F32)

    def fill(r, c):
        src = pl.multiple_of(r * GRID_W, SUBLANES)
        pad_ref[pl.ds(halo + src, GRID_W), :] = x_ref[0, pl.ds(src, GRID_W), :]
        return c

    lax.fori_loop(0, nrows, fill, 0)

    def row(r, c):
        base = pl.multiple_of(r * GRID_W, SUBLANES)
        acc = jnp.broadcast_to(b_ref[...], (GRID_W, LANES))
        for k in range(CONV_WIDTH):
            acc = acc + w_ref[k:k + 1, :] * pad_ref[pl.ds(base + k * GRID_W, GRID_W), :]
        o_ref[0, pl.ds(base, GRID_W), :] = acc
        return c

    lax.fori_loop(0, nrows, row, 0, unroll=2)


def _conv_grid(glu, w, b):
    bsz, n, ch = glu.shape
    half = ch // 2
    ng = half // LANES
    outs = []
    for body, first, pad_rows in ((_conv_row_body, 0, (n // GRID_W) * ROW_STRIDE),
                                  (_conv_col_body, ng, n + 2 * CONV_PAD * GRID_W)):
        outs.append(pl.pallas_call(
            body,
            grid=(bsz, ng),
            in_specs=[pl.BlockSpec((1, n, LANES), lambda bb, g, first=first: (bb, 0, g + first)),
                      pl.BlockSpec((CONV_WIDTH, LANES), lambda bb, g, first=first: (0, g + first)),
                      pl.BlockSpec((1, LANES), lambda bb, g, first=first: (0, g + first))],
            out_specs=pl.BlockSpec((1, n, LANES), lambda bb, g: (bb, 0, g)),
            out_shape=jax.ShapeDtypeStruct((bsz, n, half), F32),
            scratch_shapes=[pltpu.VMEM((pad_rows, LANES), F32)],
            compiler_params=_params("arbitrary", "arbitrary"),
            name="conv_row" if first == 0 else "conv_col",
        )(glu, w, b.reshape(1, ch)))
    return outs


def _out_body(cr_ref, cc_ref, lng_ref, lnb_ref, yl_ref, wo_ref, bo_ref, x_ref,
              g1_ref, n2g_ref, sh2_ref, sc2_ref, rw_ref, x1_ref, v_ref, aff_ref):
    cv = jnp.concatenate([cr_ref[0], cc_ref[0]], axis=-1)
    mu = jnp.mean(cv, axis=-1, keepdims=True)
    dv = cv - mu
    var = jnp.mean(dv * dv, axis=-1, keepdims=True)
    yn = dv * lax.rsqrt(var + LN_EPS) * lng_ref[...] + lnb_ref[...]
    cy = yn * jax.nn.sigmoid(yn)
    cat = jnp.concatenate([cy.astype(BF16), yl_ref[0]], axis=-1)
    m = jnp.dot(cat, wo_ref[...], preferred_element_type=F32) + bo_ref[...]
    x1 = x_ref[0] + g1_ref[0] * m
    x1_ref[0] = x1
    ms = jnp.mean(x1 * x1, axis=-1, keepdims=True)
    v = x1 * lax.rsqrt(ms + RMS_EPS) * n2g_ref[...]
    v = v * (1.0 + sc2_ref[0]) + sh2_ref[0]
    v_ref[0] = v.astype(BF16)
    v_hi = v.astype(BF16)
    v_lo = (v - v_hi.astype(F32)).astype(BF16)
    rw = rw_ref[...]
    rw_hi = rw.astype(BF16)
    rw_lo = (rw - rw_hi.astype(F32)).astype(BF16)
    nt = (((1,), (1,)), ((), ()))
    ne = rw.shape[0]
    both = lax.dot_general(jnp.concatenate([rw_hi, rw_lo], axis=0), v_hi, nt, preferred_element_type=F32)
    lg = both[:ne] + both[ne:] + lax.dot_general(rw_hi, v_lo, nt, preferred_element_type=F32)
    ex = jnp.exp(lg - jnp.max(lg, axis=0, keepdims=True))
    aff_ref[0] = ex / jnp.sum(ex, axis=0, keepdims=True)


def _out_proj_route(conv_r, conv_c, ln_g, ln_b, yl, wo_bf16, b_out, x, g1, n2g, sh2, sc2, rw_t, tm):
    bsz, n, d = x.shape
    half = conv_r.shape[2]
    ch = yl.shape[2]
    ne = rw_t.shape[0]
    tok = lambda c: pl.BlockSpec((1, tm, c), lambda b, i: (b, i, 0))
    full = lambda shape: pl.BlockSpec(shape, lambda b, i: (0,) * len(shape))
    per_b = pl.BlockSpec((1, 1, d), lambda b, i: (b, 0, 0))
    return pl.pallas_call(
        _out_body,
        grid=(bsz, n // tm),
        in_specs=[tok(half), tok(half), full((1, 2 * half)), full((1, 2 * half)),
                  tok(ch), full((2 * half + ch, d)), full((1, d)), tok(d),
                  per_b, full((1, d)), per_b, per_b, full((ne, d))],
        out_specs=[tok(d), tok(d), pl.BlockSpec((1, ne, tm), lambda b, i: (b, 0, i))],
        out_shape=[jax.ShapeDtypeStruct((bsz, n, d), F32),
                   jax.ShapeDtypeStruct((bsz, n, d), BF16),
                   jax.ShapeDtypeStruct((bsz, ne, n), F32)],
        compiler_params=_params("arbitrary", "arbitrary"),
        name="out_proj_route",
    )(conv_r, conv_c, ln_g.reshape(1, -1), ln_b.reshape(1, -1), yl, wo_bf16,
      b_out.reshape(1, d), x, g1, n2g.reshape(1, d), sh2, sc2, rw_t)


def _token_cumsum(m, rows_per_expert):
    er = m.shape[0]
    li = lax.broadcasted_iota(jnp.int32, (LANES, LANES), 0)
    lj = lax.broadcasted_iota(jnp.int32, (LANES, LANES), 1)
    upper = jnp.where(li <= lj, 1.0, 0.0).astype(BF16)
    cs = jnp.dot(m.astype(BF16), upper, preferred_element_type=F32)
    rt = jnp.broadcast_to(cs[:, LANES - 1:LANES], (er, LANES)).astype(BF16)
    ri = lax.broadcasted_iota(jnp.int32, (er, er), 0)
    ci = lax.broadcasted_iota(jnp.int32, (er, er), 1)
    same = (ri // rows_per_expert) == (ci // rows_per_expert)
    lower = jnp.where(same & (ci < ri), 1.0, 0.0).astype(BF16)
    rstart = jnp.dot(lower, rt, preferred_element_type=F32)
    return cs, rstart


def _topk_body(aff_ref, gate_ref, pos_ref, rs_ref, cs_scr, rs_scr, *, cap):
    aff = aff_ref[0]
    ne, rows, _ = aff.shape

    def bisect(i, thr):
        cand = thr | jnp.left_shift(jnp.int32(1), 30 - i)
        cnt = jnp.sum((aff >= pltpu.bitcast(cand, F32)).astype(jnp.int32), axis=(1, 2), keepdims=True)
        return jnp.where(cnt >= cap, cand, thr)

    thr = lax.fori_loop(0, 31, bisect, jnp.zeros((ne, 1, 1), jnp.int32))
    gt = aff >= pltpu.bitcast(thr + 1, F32)
    eq = (aff >= pltpu.bitcast(thr, F32)) & jnp.logical_not(gt)
    need = (cap - jnp.sum(gt.astype(jnp.int32), axis=(1, 2), keepdims=True)).astype(F32)
    eqf = jnp.where(eq, 1.0, 0.0)
    rank_in_row, rank_row0 = _token_cumsum(eqf.reshape(ne * rows, LANES), rows)
    rank_excl = (rank_in_row + rank_row0).reshape(ne, rows, LANES) - eqf
    sel = gt | (eq & (rank_excl < need))
    self_ = jnp.where(sel, 1.0, 0.0)
    cs, rstart = _token_cumsum(self_.reshape(ne * rows, LANES), rows)
    cs3 = cs.reshape(ne, rows, LANES)
    rstart3 = rstart.reshape(ne, rows, LANES)
    cs_scr[...] = cs3
    rs_scr[...] = rstart3
    pos_ref[0] = jnp.where(sel, cs3 + rstart3 - 1.0, -1.0).astype(jnp.int32)
    rs_ref[0] = rstart3.astype(jnp.int32)

    jrow = lax.broadcasted_iota(jnp.int32, (1, cap), 1).astype(F32)
    sub_r = lax.broadcasted_iota(jnp.int32, (rows, cap), 0).astype(F32)
    sub_l = lax.broadcasted_iota(jnp.int32, (LANES, cap), 0).astype(F32)

    def per_expert(e, c):
        cl = cs_scr[e]
        af = aff_ref[0, e]
        rowtot = cl[:, LANES - 1:LANES]
        before = rowtot + rs_scr[e][:, 0:1] <= jrow
        rj = jnp.sum(jnp.where(before, 1.0, 0.0), axis=0, keepdims=True)
        rowbase = jnp.sum(jnp.where(before, rowtot, 0.0), axis=0, keepdims=True)
        onehot = jnp.where(sub_r == rj, 1.0, 0.0).astype(BF16)
        a1 = af.astype(BF16)
        r1 = af - a1.astype(F32)
        a2 = r1.astype(BF16)
        a3 = (r1 - a2.astype(F32)).astype(BF16)
        lhs = jnp.concatenate([cl.astype(BF16), a1, a2, a3], axis=1)
        gathered = lax.dot_general(lhs, onehot, (((0,), (0,)), ((), ())), preferred_element_type=F32)
        g = gathered[:LANES]
        ga = gathered[LANES:2 * LANES] + gathered[2 * LANES:3 * LANES] + gathered[3 * LANES:]
        lanepos = jnp.sum(jnp.where(g <= jrow - rowbase, 1.0, 0.0), axis=0, keepdims=True)
        gate = jnp.sum(jnp.where(sub_l == lanepos, ga, 0.0), axis=0, keepdims=True)
        gate_ref[0, pl.ds(e, 1), :] = gate
        return c

    lax.fori_loop(0, ne, per_expert, 0)


def _topk_route(aff4, cap):
    bsz, ne, rows, _ = aff4.shape
    blk4 = pl.BlockSpec((1, ne, rows, LANES), lambda b: (b, 0, 0, 0))
    lst = pl.BlockSpec((1, ne, cap), lambda b: (b, 0, 0))
    return pl.pallas_call(
        functools.partial(_topk_body, cap=cap),
        grid=(bsz,),
        in_specs=[blk4],
        out_specs=[lst, blk4, blk4],
        out_shape=[jax.ShapeDtypeStruct((bsz, ne, cap), F32),
                   jax.ShapeDtypeStruct((bsz, ne, rows, LANES), jnp.int32),
                   jax.ShapeDtypeStruct((bsz, ne, rows, LANES), jnp.int32)],
        scratch_shapes=[pltpu.VMEM((ne, rows, LANES), F32), pltpu.VMEM((ne, rows, LANES), F32)],
        compiler_params=_params("arbitrary"),
        name="topk_route",
    )(aff4)


PACK_TOK = 256
PACK_SLAB = 64
PACK_SHIFT = 6
PACK_WAYS = 2


def _pack_body(rs_ref, pos_ref, v_ref, xs_hbm, stage, stage_x, pend, sems, sem_x, *, cap):
    bp = pl.program_id(0)
    s = pl.program_id(1)
    q = pl.program_id(2)
    ns = pl.num_programs(1)
    b = bp * PACK_WAYS + q
    ne = pos_ref.shape[2]
    tpr = PACK_TOK // LANES
    nr = ns * tpr
    step = (bp * ns + s) * PACK_WAYS + q
    nsteps = pl.num_programs(0) * ns * PACK_WAYS
    slot = step % (PACK_WAYS + 1)
    slab = PACK_SLAB
    pq = q * ne

    @pl.when(s == 0)
    def _():
        for e in range(ne):
            pend[pq + e] = jnp.zeros(pend.shape[1:], BF16)

    def first_pos(rr, e):
        return rs_ref[(b * ne + e) * (nr + 1) + rr]

    def floor16(p):
        return lax.shift_left(lax.shift_right_logical(p, 4), 4)

    p0 = [first_pos(s * tpr, e) for e in range(ne)]
    p1 = [first_pos(s * tpr + tpr, e) for e in range(ne)]
    a0 = [floor16(p0[e]) for e in range(ne)]
    a1 = [floor16(p1[e]) for e in range(ne)]
    has = [a1[e] < p1[e] for e in range(ne)]
    prnd = [lax.shift_right_logical(a1[e] - a0[e], PACK_SHIFT) for e in range(ne)]
    poff = [pl.multiple_of(jnp.bitwise_and(a1[e] - a0[e], slab - 1), BF16_ROWS) for e in range(ne)]

    vb = v_ref[0]
    pos = [jnp.concatenate([pos_ref[0, j, e:e + 1, :] for j in range(tpr)], axis=1) for e in range(ne)]
    kiota = lax.broadcasted_iota(jnp.int32, (slab, PACK_TOK), 0)

    def onehot(rnd):
        blocks = [jnp.where(kiota == pos[e] - (a0[e] + rnd * slab), 1.0, 0.0) for e in range(ne)]
        return jnp.concatenate(blocks, axis=0).astype(BF16)

    def keep_partial(buf, e, rnd, old):
        grp = buf[pl.ds(e * slab + poff[e], BF16_ROWS), :]
        return jnp.where(jnp.logical_and(has[e], prnd[e] == rnd), grp, old)

    cur = stage.at[slot]
    cur[...] = jnp.dot(onehot(0), vb, preferred_element_type=F32).astype(BF16)
    for e in range(ne):
        cur[e * slab:e * slab + BF16_ROWS, :] += pend[pq + e]
    for e in range(ne):
        pend[pq + e] = keep_partial(cur, e, 0, jnp.zeros((BF16_ROWS, vb.shape[1]), BF16))

    def slab_bytes_wait(buf, sem):
        pltpu.make_async_copy(buf, buf, sem).wait()

    @pl.when(step >= PACK_WAYS)
    def _():
        old = (step + 1) % (PACK_WAYS + 1)
        slab_bytes_wait(stage.at[old], sems.at[old])

    @pl.when(s == 0)
    def _():
        stage_x[0:slab, :] = jnp.zeros((slab, stage_x.shape[1]), BF16)
        for e in range(ne):
            pltpu.make_async_copy(stage_x.at[0:slab], xs_hbm.at[e, b, pl.ds(cap, slab)], sem_x).start()
        slab_bytes_wait(stage_x, sem_x)

    for e in range(ne):
        pltpu.make_async_copy(stage.at[slot, e * slab:(e + 1) * slab],
                              xs_hbm.at[e, b, pl.ds(pl.multiple_of(a0[e], BF16_ROWS), slab)], sems.at[slot]).start()

    span = p1[0] - a0[0]
    for e in range(1, ne):
        span = jnp.maximum(span, p1[e] - a0[e])
    rounds = jnp.maximum(1, lax.shift_right_logical(span + (slab - 1), PACK_SHIFT))

    def extra(rnd, carry):
        stage_x[...] = jnp.dot(onehot(rnd), vb, preferred_element_type=F32).astype(BF16)
        for e in range(ne):
            pend[pq + e] = keep_partial(stage_x, e, rnd, pend[pq + e])
        for e in range(ne):
            @pl.when(p1[e] - a0[e] > rnd * slab)
            def _(e=e):
                cp = pltpu.make_async_copy(
                    stage_x.at[e * slab:(e + 1) * slab],
                    xs_hbm.at[e, b, pl.ds(pl.multiple_of(a0[e] + rnd * slab, BF16_ROWS), slab)], sem_x)
                cp.start()
                cp.wait()
        return carry

    lax.fori_loop(1, rounds, extra, 0)

    @pl.when(step == nsteps - 1)
    def _():
        for back in range(PACK_WAYS):
            last = (step + (PACK_WAYS + 1) - back) % (PACK_WAYS + 1)
            slab_bytes_wait(stage.at[last], sems.at[last])


def _pack_tokens(rs_ext, pos_r, v, cap):
    bsz, n, d = v.shape
    assert bsz % PACK_WAYS == 0
    ne = pos_r.shape[2]
    tpr = PACK_TOK // LANES
    return pl.pallas_call(
        functools.partial(_pack_body, cap=cap),
        grid_spec=pltpu.PrefetchScalarGridSpec(
            num_scalar_prefetch=1,
            grid=(bsz // PACK_WAYS, n // PACK_TOK, PACK_WAYS),
            in_specs=[pl.BlockSpec((1, tpr, ne, LANES), lambda bp, s, q, rs: (bp * PACK_WAYS + q, s, 0, 0)),
                      pl.BlockSpec((1, PACK_TOK, d), lambda bp, s, q, rs: (bp * PACK_WAYS + q, s, 0))],
            out_specs=pl.BlockSpec(memory_space=pl.ANY),
            scratch_shapes=[pltpu.VMEM((PACK_WAYS + 1, ne * PACK_SLAB, d), BF16),
                            pltpu.VMEM((ne * PACK_SLAB, d), BF16),
                            pltpu.VMEM((PACK_WAYS * ne, BF16_ROWS, d), BF16),
                            pltpu.SemaphoreType.DMA((PACK_WAYS + 1,)),
                            pltpu.SemaphoreType.DMA(())]),
        out_shape=jax.ShapeDtypeStruct((ne, bsz, cap + PACK_SLAB, d), BF16),
        compiler_params=_params("arbitrary", "arbitrary", "arbitrary"),
        name="pack_tokens",
    )(rs_ext, pos_r, v)


def _ffn_body(x_ref, gate_ref, wg_ref, wu_ref, wd_ref, ys_hbm, acc, stage, osems, *, mg, nf, sub, och):
    e = pl.program_id(0)
    f = pl.program_id(1)
    spg = x_ref.shape[2]

    @pl.when(f == 0)
    def _():
        acc[...] = jnp.zeros(acc.shape, F32)

    wgb = wg_ref[0].astype(BF16)
    wub = wu_ref[0].astype(BF16)
    wdb = wd_ref[0].astype(BF16)
    for i in range(mg // sub):
        rows = slice(i * sub, (i + 1) * sub)
        x = x_ref[0, (i * sub) // spg, (i * sub) % spg:(i * sub) % spg + sub, :]
        gg = jnp.dot(x, wgb, preferred_element_type=F32)
        uu = jnp.dot(x, wub, preferred_element_type=F32)
        h = (gg * jax.nn.sigmoid(gg) * uu).astype(BF16)
        acc[rows, :] += jnp.dot(h, wdb, preferred_element_type=F32)

    @pl.when(f == nf - 1)
    def _():
        gate_t = gate_ref[0].T

        def out_copy(c):
            return pltpu.make_async_copy(stage.at[c % 2], ys_hbm.at[e, pl.ds(c * och, och)], osems.at[c % 2])

        nchunks = mg // och
        for c in range(nchunks):
            if c >= 2:
                out_copy(c - 2).wait()
            for i in range(och // LANES):
                blk = c * (och // LANES) + i
                rows = slice(blk * LANES, (blk + 1) * LANES)
                stage[c % 2, i * LANES:(i + 1) * LANES, :] = (acc[rows, :] * gate_t[:, blk:blk + 1]).astype(BF16)
            out_copy(c).start()
        for c in range(max(nchunks - 2, 0), nchunks):
            out_copy(c).wait()


def _expert_ffn(xs, gate_rows, wg, wu, wd, tf):
    ne, d, fdim = wg.shape
    bsz, cap = xs.shape[1], gate_rows.shape[1] * LANES // xs.shape[1]
    mg = bsz * cap
    nf = fdim // tf
    sub = cap
    och = min(mg, 512)
    return pl.pallas_call(
        functools.partial(_ffn_body, mg=mg, nf=nf, sub=sub, och=och),
        grid_spec=pltpu.PrefetchScalarGridSpec(
            num_scalar_prefetch=0,
            grid=(ne, nf),
            in_specs=[pl.BlockSpec((1, bsz, cap, d), lambda e, f: (e, 0, 0, 0)),
                      pl.BlockSpec((1, mg // LANES, LANES), lambda e, f: (e, 0, 0)),
                      pl.BlockSpec((1, d, tf), lambda e, f: (e, 0, f)),
                      pl.BlockSpec((1, d, tf), lambda e, f: (e, 0, f)),
                      pl.BlockSpec((1, tf, d), lambda e, f: (e, f, 0))],
            out_specs=pl.BlockSpec(memory_space=pl.ANY),
            scratch_shapes=[pltpu.VMEM((mg, d), F32),
                            pltpu.VMEM((2, och, d), BF16),
                            pltpu.SemaphoreType.DMA((2,))]),
        out_shape=jax.ShapeDtypeStruct((ne, mg, d), BF16),
        compiler_params=_params("arbitrary", "arbitrary"),
        name="expert_ffn",
    )(xs, gate_rows, wg, wu, wd)


SLAB = PACK_SLAB
TILE_ROWS = PACK_TOK // LANES


def _combine_body(rs_ref, pos_ref, ys_hbm, x1_ref, g5_ref, fng_ref, o_ref, stk, stk_x, sems, sem_x, *, cap, tps):
    b = pl.program_id(0)
    s = pl.program_id(1)
    nb = pl.num_programs(0)
    ns = pl.num_programs(1)
    ne = pos_ref.shape[2]
    nr = ns * tps * TILE_ROWS
    tok = TILE_ROWS * LANES
    step = b * ns + s
    slot = step % 2
    tile_rows = ne * SLAB

    def first_pos(bb, rr, e):
        return rs_ref[(bb * ne + e) * (nr + 1) + rr]

    def aligned(p0):
        return lax.shift_left(lax.shift_right_logical(p0, 4), 4)

    def slab_start(a0, rnd):
        return pl.multiple_of(jnp.minimum(a0 + rnd * SLAB, cap - SLAB), BF16_ROWS)

    def slab_copy(bb, e, a, dst, row, sem):
        return pltpu.make_async_copy(ys_hbm.at[e, pl.ds(bb * cap + a, SLAB)], dst.at[pl.ds(row, SLAB)], sem)

    def issue(bb, ss, sl):
        for j in range(tps):
            for e in range(ne):
                a = slab_start(aligned(first_pos(bb, (ss * tps + j) * TILE_ROWS, e)), 0)
                slab_copy(bb, e, a, stk.at[sl], j * tile_rows + e * SLAB, sems.at[sl]).start()

    @pl.when(step == 0)
    def _():
        issue(b, s, slot)

    @pl.when(step + 1 < nb * ns)
    def _():
        nxt = step + 1
        issue(nxt // ns, nxt % ns, 1 - slot)

    kiota = lax.broadcasted_iota(jnp.int32, (SLAB, tok), 0)
    tn = (((0,), (0,)), ((), ()))
    pltpu.make_async_copy(stk.at[slot], stk.at[slot], sems.at[slot]).wait()

    for j in range(tps):
        r = (s * tps + j) * TILE_ROWS
        pos = jnp.concatenate([pos_ref[0, j * TILE_ROWS + q] for q in range(TILE_ROWS)], axis=1)
        a0 = [aligned(first_pos(b, r, e)) for e in range(ne)]

        def onehot(rnd, pos=pos, a0=a0):
            blocks = []
            for e in range(ne):
                rel = pos[e:e + 1, :] - slab_start(a0[e], rnd)
                fresh = (pos[e:e + 1, :] - a0[e]) >= rnd * SLAB
                blocks.append(jnp.where((kiota == rel) & fresh, 1.0, 0.0))
            return jnp.concatenate(blocks, axis=0).astype(BF16)

        moe = lax.dot_general(onehot(0), stk[slot, j * tile_rows:(j + 1) * tile_rows, :], tn,
                              preferred_element_type=F32)

        span = first_pos(b, r + TILE_ROWS, 0) - a0[0]
        for e in range(1, ne):
            span = jnp.maximum(span, first_pos(b, r + TILE_ROWS, e) - a0[e])
        rounds = jnp.maximum(1, lax.shift_right_logical(span + (SLAB - 1), PACK_SHIFT))

        def extra(rnd, m, a0=a0, onehot=onehot):
            for e in range(ne):
                slab_copy(b, e, slab_start(a0[e], rnd), stk_x, e * SLAB, sem_x).start()
            pltpu.make_async_copy(stk_x, stk_x, sem_x).wait()
            return m + lax.dot_general(onehot(rnd), stk_x[...], tn, preferred_element_type=F32)

        moe = lax.fori_loop(1, rounds, extra, moe)

        y = x1_ref[0, j * tok:(j + 1) * tok, :] + g5_ref[0] * moe
        ms = jnp.mean(y * y, axis=-1, keepdims=True)
        o_ref[0, j * tok:(j + 1) * tok, :] = y * lax.rsqrt(ms + RMS_EPS) * fng_ref[...]


def _combine_norm(rs_ext, pos_r, ys, x1, g5, fng, cap, tps):
    bsz, n, d = x1.shape
    ns = n // (PACK_TOK * tps)
    ne = pos_r.shape[2]
    return pl.pallas_call(
        functools.partial(_combine_body, cap=cap, tps=tps),
        grid_spec=pltpu.PrefetchScalarGridSpec(
            num_scalar_prefetch=1,
            grid=(bsz, ns),
            in_specs=[pl.BlockSpec((1, tps * TILE_ROWS, ne, LANES), lambda b, s, rs: (b, s, 0, 0)),
                      pl.BlockSpec(memory_space=pl.ANY),
                      pl.BlockSpec((1, tps * PACK_TOK, d), lambda b, s, rs: (b, s, 0)),
                      pl.BlockSpec((1, 1, d), lambda b, s, rs: (b, 0, 0)),
                      pl.BlockSpec((1, d), lambda b, s, rs: (0, 0))],
            out_specs=pl.BlockSpec((1, tps * PACK_TOK, d), lambda b, s, rs: (b, s, 0)),
            scratch_shapes=[pltpu.VMEM((2, tps * ne * SLAB, d), BF16),
                            pltpu.VMEM((ne * SLAB, d), BF16),
                            pltpu.SemaphoreType.DMA((2,)),
                            pltpu.SemaphoreType.DMA(())]),
        out_shape=jax.ShapeDtypeStruct((bsz, n, d), F32),
        compiler_params=_params("arbitrary", "arbitrary"),
        name="combine_norm",
    )(rs_ext, pos_r, ys, x1, g5, fng.reshape(1, d))


def _block_diag(w):
    heads, hd, _ = w.shape
    eye = jnp.eye(heads, dtype=w.dtype)
    return (eye[:, None, :, None] * w[:, :, None, :]).reshape(heads * hd, heads * hd)


def _tile(n, pref):
    return pref if n % pref == 0 else n


def kernel(x, c, ctx, c_ctx, norm1_g, norm2_g, ada_w, ada_b, w_in, b_in, conv_dw_w, conv_dw_b, conv_ln_g, conv_ln_b, lru_conv_w, lru_conv_b, lru_wa, lru_ba, lru_wi, lru_bi, lru_lambda, w_out, b_out, router_w, exp_w_gate, exp_w_up, exp_w_down, final_norm_g):
    assert norm1_g.shape[0] == 1
    mod, x1, v, aff = _mixer(x, c, ctx, c_ctx, norm1_g[0], norm2_g[0], ada_w[0], ada_b[0], w_in[0], b_in[0],
                             conv_dw_w[0], conv_dw_b[0], conv_ln_g[0], conv_ln_b[0], lru_conv_w[0],
                             lru_conv_b[0], lru_wa[0], lru_ba[0], lru_wi[0], lru_bi[0], lru_lambda[0],
                             w_out[0], b_out[0], router_w[0])
    return _moe_norm(x1, v, aff, mod[5], exp_w_gate[0], exp_w_up[0], exp_w_down[0], final_norm_g)


def _mixer(x, c, ctx, c_ctx, norm1_g, norm2_g, ada_w, ada_b, w_in, b_in, conv_w, conv_b, ln_g, ln_b,
           lru_cw, lru_cb, lru_wa, lru_ba, lru_wi, lru_bi, lru_lam, w_out, b_out, router_w):
    bsz, n, d = x.shape
    cond8 = jnp.zeros((SUBLANES, d), F32).at[:bsz].set(c).at[bsz].set(c_ctx)
    mods = _ada_mod(cond8, ada_w, ada_b)
    mod = [mods[:bsz, k * d:(k + 1) * d].reshape(bsz, 1, d) for k in range(N_MOD)]
    mod_c = [jnp.broadcast_to(mods[bsz:bsz + 1, k * d:(k + 1) * d].reshape(1, 1, d), (bsz, 1, d))
             for k in range(2)]
    w_in_b = w_in.astype(BF16)
    w_out_b = w_out.astype(BF16)
    ch = lru_cb.shape[1]
    wgate = [jnp.concatenate([_block_diag(lru_wa[dd]), _block_diag(lru_wi[dd])], axis=1).astype(BF16)
             for dd in range(2)]
    bgate = [jnp.concatenate([lru_ba[dd], lru_bi[dd]]) for dd in range(2)]

    def lru(lx, h0, dd, reverse, merge_with=None):
        return _lru_dir(lx, h0, lru_cw[dd], lru_cb[dd], wgate[dd], bgate[dd], lru_lam[dd], reverse,
                        _tile(lx.shape[1], 256), merge_with)

    _, c_lx, _ = _in_proj(ctx, norm1_g, mod_c[0], mod_c[1], w_in_b, b_in, _tile(ctx.shape[1], 256))
    zero_h = jnp.zeros((bsz, 1, ch), F32)
    _, hf0 = lru(c_lx, zero_h, 0, False)
    _, hb0 = lru(c_lx, zero_h, 1, True)

    x_glu, x_lx, x_glg = _in_proj(x, norm1_g, mod[0], mod[1], w_in_b, b_in, _tile(n, 1024))
    hb, _ = lru(x_lx, hb0, 1, True)
    yl, _ = lru(x_lx, hf0, 0, False, merge_with=(hb, x_glg))
    conv_r, conv_c = _conv_grid(x_glu, conv_w, conv_b)
    x1, v, aff = _out_proj_route(conv_r, conv_c, ln_g, ln_b, yl, w_out_b, b_out, x, mod[2],
                                 norm2_g, mod[3], mod[4], router_w.T, _tile(n, 512))
    return mod, x1, v, aff


def _moe_norm(x1, v, aff, gate2, wg, wu, wd, final_norm_g):
    bsz, n, d = x1.shape
    ne = aff.shape[1]
    cap = EC_CAPACITY * n // ne
    gate, pos, rstart = _topk_route(aff.reshape(bsz, ne, n // LANES, LANES), cap)
    gate_rows = jnp.swapaxes(gate, 0, 1).reshape(ne, bsz * cap // LANES, LANES)
    rs_ext = jnp.concatenate([rstart[..., 0], jnp.full((bsz, ne, 1), cap, jnp.int32)], axis=-1).reshape(-1)
    pos_r = jnp.swapaxes(pos, 1, 2)
    xs = _pack_tokens(rs_ext, pos_r, v, cap)
    ys = _expert_ffn(xs, gate_rows, wg, wu, wd, 256)
    return _combine_norm(rs_ext, pos_r, ys, x1, gate2, final_norm_g, cap, 2)
```

```python
import functools

import jax
import jax.numpy as jnp
from jax import lax
from jax.experimental import pallas as pl
from jax.experimental.pallas import tpu as pltpu

GRID_W = 64
CONV_WIDTH = 31
CONV_PAD = (CONV_WIDTH - 1) // 2
LRU_CONV_WIDTH = 4
LRU_C = 8.0
N_EXPERTS = 16
EC_CAPACITY = 2
N_MOD = 6
RMS_EPS = 1e-6
LN_EPS = 1e-5

LANES = 128
SUBLANES = 8
BF16_ROWS = 16
VMEM_LIMIT = 56 * 1024 * 1024

F32 = jnp.float32
BF16 = jnp.bfloat16
HIGHEST = lax.Precision.HIGHEST


def _params(*sem):
    return pltpu.CompilerParams(dimension_semantics=sem, vmem_limit_bytes=VMEM_LIMIT)


def _mod_row(mods3, k, fixed_row=None):
    d = mods3.shape[2]

    def index_map(b, *_):
        return ((b if fixed_row is None else fixed_row) * N_MOD + k, 0, 0)

    return mods3, pl.BlockSpec((1, 1, d), index_map)


def _ada_body(c_ref, w_ref, b_ref, o_ref):
    s = c_ref[...]
    s = s * jax.nn.sigmoid(s)
    o_ref[...] = jnp.dot(s, w_ref[...], precision=HIGHEST, preferred_element_type=F32) + b_ref[...]


def _ada_mod(cond8, ada_w, ada_b):
    d, n = ada_w.shape
    tn = n // 4
    return pl.pallas_call(
        _ada_body,
        grid=(n // tn,),
        in_specs=[pl.BlockSpec((SUBLANES, d), lambda j: (0, 0)),
                  pl.BlockSpec((d, tn), lambda j: (0, j)),
                  pl.BlockSpec((1, tn), lambda j: (0, j))],
        out_specs=pl.BlockSpec((SUBLANES, tn), lambda j: (0, j)),
        out_shape=jax.ShapeDtypeStruct((SUBLANES, n), F32),
        compiler_params=_params("arbitrary"),
        name="ada_mod",
    )(cond8, ada_w, ada_b.reshape(1, n))


def _inproj_body(x_ref, g_ref, sh_ref, sc_ref, w_ref, b_ref, glu_ref, lx_ref, glg_ref, *, rb):
    gain = g_ref[...] * (1.0 + sc_ref[0])
    for i in range(x_ref.shape[1] // rb):
        rows = slice(i * rb, (i + 1) * rb)
        x = x_ref[0, rows, :]
        ms = jnp.mean(x * x, axis=-1, keepdims=True)
        u = x * lax.rsqrt(ms + RMS_EPS) * gain + sh_ref[0]
        p = jnp.dot(u.astype(BF16), w_ref[...], preferred_element_type=F32) + b_ref[...]
        cc = p.shape[1] // 4
        glu_ref[0, rows, :] = p[:, :cc] * jax.nn.sigmoid(p[:, cc:2 * cc])
        lx_ref[0, rows, :] = p[:, 2 * cc:3 * cc]
        glg_ref[0, rows, :] = jax.nn.gelu(p[:, 3 * cc:])


def _in_proj(x, norm_g, shift, scale, w_bf16, b_in, tm):
    bsz, n, d = x.shape
    (shift, shift_spec), (scale, scale_spec) = shift, scale
    n4 = w_bf16.shape[1]
    cc = n4 // 4
    tok = pl.BlockSpec((1, tm, cc), lambda b, i: (b, i, 0))
    return pl.pallas_call(
        functools.partial(_inproj_body, rb=min(tm, 256)),
        grid=(bsz, n // tm),
        in_specs=[pl.BlockSpec((1, tm, d), lambda b, i: (b, i, 0)),
                  pl.BlockSpec((1, d), lambda b, i: (0, 0)),
                  shift_spec, scale_spec,
                  pl.BlockSpec((d, n4), lambda b, i: (0, 0)),
                  pl.BlockSpec((1, n4), lambda b, i: (0, 0))],
        out_specs=[tok, tok, tok],
        out_shape=[jax.ShapeDtypeStruct((bsz, n, cc), F32)] * 3,
        compiler_params=_params("arbitrary", "arbitrary"),
        name="in_proj",
    )(x, norm_g.reshape(1, d), shift, scale, w_bf16, b_in.reshape(1, n4))


def _lru_body(*refs, reverse, tl, merge):
    x_ref, h0_ref, cw_ref, cb_ref, wg_ref, bg_ref, lam_ref = refs[:7]
    if merge:
        hother_ref, glg_ref = refs[7:9]
        refs = refs[2:]
    h_ref, hl_ref, a_ref, u_ref, carry_ref, halo_ref = refs[7:]
    i = pl.program_id(1)
    ch = x_ref.shape[2]

    @pl.when(i == 0)
    def _():
        carry_ref[...] = jnp.broadcast_to(h0_ref[0], carry_ref.shape)
        halo_ref[...] = jnp.zeros(halo_ref.shape, F32)

    x = x_ref[0]
    halo = halo_ref[...]
    row8 = lax.broadcasted_iota(jnp.int32, (SUBLANES, ch), 0)
    xc = cb_ref[...] + cw_ref[LRU_CONV_WIDTH - 1:LRU_CONV_WIDTH, :] * x
    for j in range(LRU_CONV_WIDTH - 1):
        s = LRU_CONV_WIDTH - 1 - j
        if not reverse:
            rolled = pltpu.roll(x, s, 0)
            edge = jnp.where(row8 < s, pltpu.roll(halo, s, 0), rolled[0:SUBLANES, :])
            tap = jnp.concatenate([edge, rolled[SUBLANES:, :]], axis=0)
        else:
            rolled = pltpu.roll(x, tl - s, 0)
            edge = jnp.where(row8 >= SUBLANES - s, pltpu.roll(halo, SUBLANES - s, 0), rolled[tl - SUBLANES:, :])
            tap = jnp.concatenate([rolled[:tl - SUBLANES, :], edge], axis=0)
        xc = xc + cw_ref[j:j + 1, :] * tap
    halo_ref[...] = x[0:SUBLANES, :] if reverse else x[tl - SUBLANES:tl, :]

    z = jnp.dot(xc.astype(BF16), wg_ref[...], preferred_element_type=F32) + bg_ref[...]
    half_c = (-0.5 * LRU_C) * jax.nn.softplus(-lam_ref[...])
    log_a = half_c * jnp.tanh(0.5 * z[:, :ch]) + half_c
    ig = 0.5 * jnp.tanh(0.5 * z[:, ch:]) + 0.5
    th = jnp.tanh(log_a)
    a_ref[...] = jnp.exp(log_a)
    m2 = -2.0 * th / (1.0 - th)
    u_ref[...] = jnp.where(m2 > 0.0, m2 * lax.rsqrt(m2), 0.0) * (ig * xc)

    ng = tl // SUBLANES
    rowid = lax.broadcasted_iota(jnp.int32, (SUBLANES, ch), 0)

    def group(gi, h):
        g = (ng - 1 - gi) if reverse else gi
        off = pl.multiple_of(g * SUBLANES, SUBLANES)
        a = a_ref[pl.ds(off, SUBLANES), :]
        u = u_ref[pl.ds(off, SUBLANES), :]
        for s in (1, 2, 4):
            if reverse:
                m = rowid < SUBLANES - s
                sh = SUBLANES - s
            else:
                m = rowid >= s
                sh = s
            ap = jnp.where(m, pltpu.roll(a, sh, 0), 1.0)
            up = jnp.where(m, pltpu.roll(u, sh, 0), 0.0)
            u = a * up + u
            a = a * ap
        hh = a * h + u
        if merge:
            u_ref[pl.ds(off, SUBLANES), :] = hh
        else:
            h_ref[0, pl.ds(off, SUBLANES), :] = hh
        edge = hh[0:1, :] if reverse else hh[SUBLANES - 1:SUBLANES, :]
        return jnp.broadcast_to(edge, (SUBLANES, ch))

    h = lax.fori_loop(0, ng, group, carry_ref[...], unroll=8)
    carry_ref[...] = h
    hl_ref[0] = h[0:1, :]
    if merge:
        h_ref[0] = ((u_ref[...] + hother_ref[0]) * glg_ref[0]).astype(BF16)


def _lru_dir(lx, h0, cw, cb, wgate, bgate, lam, reverse, tl, merge_with=None):
    bsz, n, ch = lx.shape
    nc = n // tl
    cmap = (lambda b, i: (b, nc - 1 - i, 0)) if reverse else (lambda b, i: (b, i, 0))
    full = lambda shape: pl.BlockSpec(shape, lambda b, i: (0,) * len(shape))
    merge = merge_with is not None
    extra = list(merge_with) if merge else []
    return pl.pallas_call(
        functools.partial(_lru_body, reverse=reverse, tl=tl, merge=merge),
        grid=(bsz, nc),
        in_specs=[pl.BlockSpec((1, tl, ch), cmap),
                  pl.BlockSpec((1, 1, ch), lambda b, i: (b, 0, 0)),
                  full((LRU_CONV_WIDTH, ch)), full((1, ch)),
                  full((ch, 2 * ch)), full((1, 2 * ch)), full((1, ch))]
                 + [pl.BlockSpec((1, tl, ch), cmap)] * len(extra),
        out_specs=[pl.BlockSpec((1, tl, ch), cmap),
                   pl.BlockSpec((1, 1, ch), lambda b, i: (b, 0, 0))],
        out_shape=[jax.ShapeDtypeStruct((bsz, n, ch), BF16 if merge else F32),
                   jax.ShapeDtypeStruct((bsz, 1, ch), F32)],
        scratch_shapes=[pltpu.VMEM((tl, ch), F32),
                        pltpu.VMEM((tl, ch), F32),
                        pltpu.VMEM((SUBLANES, ch), F32),
                        pltpu.VMEM((SUBLANES, ch), F32)],
        compiler_params=_params("arbitrary", "arbitrary"),
        name="lru_rev" if reverse else "lru_fwd",
    )(lx, h0, cw, cb.reshape(1, ch), wgate, bgate.reshape(1, 2 * ch), lam.reshape(1, ch), *extra)


ROW_STRIDE = GRID_W + 2 * BF16_ROWS


def _conv_row_body(x_ref, w_ref, b_ref, o_ref, pad_ref):
    n = x_ref.shape[1]
    nrows = n // GRID_W
    gap = jnp.zeros((BF16_ROWS, LANES), F32)

    def fill(r, c):
        base = pl.multiple_of(r * ROW_STRIDE, SUBLANES)
        src = pl.multiple_of(r * GRID_W, SUBLANES)
        pad_ref[pl.ds(base, BF16_ROWS), :] = gap
        pad_ref[pl.ds(base + BF16_ROWS, GRID_W), :] = x_ref[0, pl.ds(src, GRID_W), :]
        pad_ref[pl.ds(base + BF16_ROWS + GRID_W, BF16_ROWS), :] = gap
        return c

    lax.fori_loop(0, nrows, fill, 0)

    def row(r, c):
        base = pl.multiple_of(r * ROW_STRIDE, SUBLANES)
        acc = jnp.broadcast_to(b_ref[...], (GRID_W, LANES))
        for k in range(CONV_WIDTH):
            acc = acc + w_ref[k:k + 1, :] * pad_ref[pl.ds(base + BF16_ROWS - CONV_PAD + k, GRID_W), :]
        o_ref[0, pl.ds(pl.multiple_of(r * GRID_W, SUBLANES), GRID_W), :] = acc
        return c

    lax.fori_loop(0, nrows, row, 0, unroll=2)


def _conv_col_body(x_ref, w_ref, b_ref, o_ref, pad_ref):
    n = x_ref.shape[1]
    nrows = n // GRID_W
    halo = CONV_PAD * GRID_W
    pad_ref[0:halo, :] = jnp.zeros((halo, LANES), F32)
    pad_ref[halo + n:halo + n + halo, :] = jnp.zeros((halo, LANES), F32)

    def fill(r, c):
        src = pl.multiple_of(r * GRID_W, SUBLANES)
        pad_ref[pl.ds(halo + src, GRID_W), :] = x_ref[0, pl.ds(src, GRID_W), :]
        return c

    lax.fori_loop(0, nrows, fill, 0)

    def row(r, c):
        base = pl.multiple_of(r * GRID_W, SUBLANES)
        acc = jnp.broadcast_to(b_ref[...], (GRID_W, LANES))
        for k in range(CONV_WIDTH):
            acc = acc + w_ref[k:k + 1, :] * pad_ref[pl.ds(base + k * GRID_W, GRID_W), :]
        o_ref[0, pl.ds(base, GRID_W), :] = acc
        return c

    lax.fori_loop(0, nrows, row, 0, unroll=2)


def _conv_grid(glu, w, b):
    bsz, n, ch = glu.shape
    half = ch // 2
    ng = half // LANES
    outs = []
    for body, first, pad_rows in ((_conv_row_body, 0, (n // GRID_W) * ROW_STRIDE),
                                  (_conv_col_body, ng, n + 2 * CONV_PAD * GRID_W)):
        outs.append(pl.pallas_call(
            body,
            grid=(bsz, ng),
            in_specs=[pl.BlockSpec((1, n, LANES), lambda bb, g, first=first: (bb, 0, g + first)),
                      pl.BlockSpec((CONV_WIDTH, LANES), lambda bb, g, first=first: (0, g + first)),
                      pl.BlockSpec((1, LANES), lambda bb, g, first=first: (0, g + first))],
            out_specs=pl.BlockSpec((1, n, LANES), lambda bb, g: (bb, 0, g)),
            out_shape=jax.ShapeDtypeStruct((bsz, n, half), F32),
            scratch_shapes=[pltpu.VMEM((pad_rows, LANES), F32)],
            compiler_params=_params("arbitrary", "arbitrary"),
            name="conv_row" if first == 0 else "conv_col",
        )(glu, w, b.reshape(1, ch)))
    return outs


def _out_body(cr_ref, cc_ref, lng_ref, lnb_ref, yl_ref, wo_ref, bo_ref, x_ref,
              g1_ref, n2g_ref, sh2_ref, sc2_ref, rw_ref, x1_ref, v_ref, aff_ref):
    cv = jnp.concatenate([cr_ref[0], cc_ref[0]], axis=-1)
    mu = jnp.mean(cv, axis=-1, keepdims=True)
    dv = cv - mu
    var = jnp.mean(dv * dv, axis=-1, keepdims=True)
    yn = dv * lax.rsqrt(var + LN_EPS) * lng_ref[...] + lnb_ref[...]
    cy = yn * jax.nn.sigmoid(yn)
    cat = jnp.concatenate([cy.astype(BF16), yl_ref[0]], axis=-1)
    m = jnp.dot(cat, wo_ref[...], preferred_element_type=F32) + bo_ref[...]
    x1 = x_ref[0] + g1_ref[0] * m
    x1_ref[0] = x1
    ms = jnp.mean(x1 * x1, axis=-1, keepdims=True)
    v = x1 * lax.rsqrt(ms + RMS_EPS) * n2g_ref[...]
    v = v * (1.0 + sc2_ref[0]) + sh2_ref[0]
    v_ref[0] = v.astype(BF16)
    v_hi = v.astype(BF16)
    v_lo = (v - v_hi.astype(F32)).astype(BF16)
    rw = rw_ref[...]
    rw_hi = rw.astype(BF16)
    rw_lo = (rw - rw_hi.astype(F32)).astype(BF16)
    nt = (((1,), (1,)), ((), ()))
    ne = rw.shape[0]
    both = lax.dot_general(jnp.concatenate([rw_hi, rw_lo], axis=0), v_hi, nt, preferred_element_type=F32)
    lg = both[:ne] + both[ne:] + lax.dot_general(rw_hi, v_lo, nt, preferred_element_type=F32)
    ex = jnp.exp(lg - jnp.max(lg, axis=0, keepdims=True))
    aff_ref[0] = ex / jnp.sum(ex, axis=0, keepdims=True)


def _out_proj_route(conv_r, conv_c, ln_g, ln_b, yl, wo_bf16, b_out, x, g1, n2g, sh2, sc2, rw_t, tm):
    bsz, n, d = x.shape
    half = conv_r.shape[2]
    ch = yl.shape[2]
    ne = rw_t.shape[0]
    tok = lambda c: pl.BlockSpec((1, tm, c), lambda b, i: (b, i, 0))
    full = lambda shape: pl.BlockSpec(shape, lambda b, i: (0,) * len(shape))
    (g1, g1_spec), (sh2, sh2_spec), (sc2, sc2_spec) = g1, sh2, sc2
    return pl.pallas_call(
        _out_body,
        grid=(bsz, n // tm),
        in_specs=[tok(half), tok(half), full((1, 2 * half)), full((1, 2 * half)),
                  tok(ch), full((2 * half + ch, d)), full((1, d)), tok(d),
                  g1_spec, full((1, d)), sh2_spec, sc2_spec, full((ne, d))],
        out_specs=[tok(d), tok(d), pl.BlockSpec((1, ne, tm), lambda b, i: (b, 0, i))],
        out_shape=[jax.ShapeDtypeStruct((bsz, n, d), F32),
                   jax.ShapeDtypeStruct((bsz, n, d), BF16),
                   jax.ShapeDtypeStruct((bsz, ne, n), F32)],
        compiler_params=_params("arbitrary", "arbitrary"),
        name="out_proj_route",
    )(conv_r, conv_c, ln_g.reshape(1, -1), ln_b.reshape(1, -1), yl, wo_bf16,
      b_out.reshape(1, d), x, g1, n2g.reshape(1, d), sh2, sc2, rw_t)


def _token_cumsum(m, rows_per_expert):
    er = m.shape[0]
    li = lax.broadcasted_iota(jnp.int32, (LANES, LANES), 0)
    lj = lax.broadcasted_iota(jnp.int32, (LANES, LANES), 1)
    upper = jnp.where(li <= lj, 1.0, 0.0).astype(BF16)
    cs = jnp.dot(m.astype(BF16), upper, preferred_element_type=F32)
    rt = jnp.broadcast_to(cs[:, LANES - 1:LANES], (er, LANES)).astype(BF16)
    ri = lax.broadcasted_iota(jnp.int32, (er, er), 0)
    ci = lax.broadcasted_iota(jnp.int32, (er, er), 1)
    same = (ri // rows_per_expert) == (ci // rows_per_expert)
    lower = jnp.where(same & (ci < ri), 1.0, 0.0).astype(BF16)
    rstart = jnp.dot(lower, rt, preferred_element_type=F32)
    return cs, rstart


def _topk_body(aff_ref, gate_ref, pos_ref, rs_ref, cs_scr, rs_scr, *, cap):
    aff = aff_ref[0]
    ne, rows, _ = aff.shape

    def bisect(i, thr):
        cand = thr | jnp.left_shift(jnp.int32(1), 30 - i)
        cnt = jnp.sum((aff >= pltpu.bitcast(cand, F32)).astype(jnp.int32), axis=(1, 2), keepdims=True)
        return jnp.where(cnt >= cap, cand, thr)

    thr = lax.fori_loop(0, 31, bisect, jnp.zeros((ne, 1, 1), jnp.int32))
    gt = aff >= pltpu.bitcast(thr + 1, F32)
    eq = (aff >= pltpu.bitcast(thr, F32)) & jnp.logical_not(gt)
    need = (cap - jnp.sum(gt.astype(jnp.int32), axis=(1, 2), keepdims=True)).astype(F32)
    eqf = jnp.where(eq, 1.0, 0.0)
    rank_in_row, rank_row0 = _token_cumsum(eqf.reshape(ne * rows, LANES), rows)
    rank_excl = (rank_in_row + rank_row0).reshape(ne, rows, LANES) - eqf
    sel = gt | (eq & (rank_excl < need))
    self_ = jnp.where(sel, 1.0, 0.0)
    cs, rstart = _token_cumsum(self_.reshape(ne * rows, LANES), rows)
    cs3 = cs.reshape(ne, rows, LANES)
    rstart3 = rstart.reshape(ne, rows, LANES)
    cs_scr[...] = cs3
    rs_scr[...] = rstart3
    pos_ref[0] = jnp.where(sel, cs3 + rstart3 - 1.0, -1.0).astype(jnp.int32)
    rs_ref[0] = rstart3.astype(jnp.int32)

    jrow = lax.broadcasted_iota(jnp.int32, (1, cap), 1).astype(F32)
    sub_r = lax.broadcasted_iota(jnp.int32, (rows, cap), 0).astype(F32)
    sub_l = lax.broadcasted_iota(jnp.int32, (LANES, cap), 0).astype(F32)

    def per_expert(e, c):
        cl = cs_scr[e]
        af = aff_ref[0, e]
        rowtot = cl[:, LANES - 1:LANES]
        before = rowtot + rs_scr[e][:, 0:1] <= jrow
        rj = jnp.sum(jnp.where(before, 1.0, 0.0), axis=0, keepdims=True)
        rowbase = jnp.sum(jnp.where(before, rowtot, 0.0), axis=0, keepdims=True)
        onehot = jnp.where(sub_r == rj, 1.0, 0.0).astype(BF16)
        a1 = af.astype(BF16)
        r1 = af - a1.astype(F32)
        a2 = r1.astype(BF16)
        a3 = (r1 - a2.astype(F32)).astype(BF16)
        lhs = jnp.concatenate([cl.astype(BF16), a1, a2, a3], axis=1)
        gathered = lax.dot_general(lhs, onehot, (((0,), (0,)), ((), ())), preferred_element_type=F32)
        g = gathered[:LANES]
        ga = gathered[LANES:2 * LANES] + gathered[2 * LANES:3 * LANES] + gathered[3 * LANES:]
        lanepos = jnp.sum(jnp.where(g <= jrow - rowbase, 1.0, 0.0), axis=0, keepdims=True)
        gate = jnp.sum(jnp.where(sub_l == lanepos, ga, 0.0), axis=0, keepdims=True)
        gate_ref[0, pl.ds(e, 1), :] = gate
        return c

    lax.fori_loop(0, ne, per_expert, 0)


def _topk_route(aff4, cap):
    bsz, ne, rows, _ = aff4.shape
    blk4 = pl.BlockSpec((1, ne, rows, LANES), lambda b: (b, 0, 0, 0))
    lst = pl.BlockSpec((1, ne, cap), lambda b: (b, 0, 0))
    return pl.pallas_call(
        functools.partial(_topk_body, cap=cap),
        grid=(bsz,),
        in_specs=[blk4],
        out_specs=[lst, blk4, blk4],
        out_shape=[jax.ShapeDtypeStruct((bsz, ne, cap), F32),
                   jax.ShapeDtypeStruct((bsz, ne, rows, LANES), jnp.int32),
                   jax.ShapeDtypeStruct((bsz, ne, rows, LANES), jnp.int32)],
        scratch_shapes=[pltpu.VMEM((ne, rows, LANES), F32), pltpu.VMEM((ne, rows, LANES), F32)],
        compiler_params=_params("arbitrary"),
        name="topk_route",
    )(aff4)


PACK_TOK = 256
PACK_SLAB = 64
PACK_SHIFT = 6
PACK_WAYS = 2


def _pack_body(rs_ref, pos_ref, v_ref, xs_hbm, stage, stage_x, pend, sems, sem_x, *, cap):
    bp = pl.program_id(0)
    s = pl.program_id(1)
    q = pl.program_id(2)
    ns = pl.num_programs(1)
    b = bp * PACK_WAYS + q
    ne = pos_ref.shape[2]
    tpr = PACK_TOK // LANES
    nr = ns * tpr
    step = (bp * ns + s) * PACK_WAYS + q
    nsteps = pl.num_programs(0) * ns * PACK_WAYS
    slot = step % (PACK_WAYS + 1)
    slab = PACK_SLAB
    pq = q * ne

    @pl.when(s == 0)
    def _():
        for e in range(ne):
            pend[pq + e] = jnp.zeros(pend.shape[1:], BF16)

    def first_pos(rr, e):
        return rs_ref[(b * ne + e) * (nr + 1) + rr]

    def floor16(p):
        return lax.shift_left(lax.shift_right_logical(p, 4), 4)

    p0 = [first_pos(s * tpr, e) for e in range(ne)]
    p1 = [first_pos(s * tpr + tpr, e) for e in range(ne)]
    a0 = [floor16(p0[e]) for e in range(ne)]
    a1 = [floor16(p1[e]) for e in range(ne)]
    has = [a1[e] < p1[e] for e in range(ne)]
    prnd = [lax.shift_right_logical(a1[e] - a0[e], PACK_SHIFT) for e in range(ne)]
    poff = [pl.multiple_of(jnp.bitwise_and(a1[e] - a0[e], slab - 1), BF16_ROWS) for e in range(ne)]

    vb = v_ref[0]
    pos = [jnp.concatenate([pos_ref[0, j, e:e + 1, :] for j in range(tpr)], axis=1) for e in range(ne)]
    kiota = lax.broadcasted_iota(jnp.int32, (slab, PACK_TOK), 0)

    def onehot(rnd):
        blocks = [jnp.where(kiota == pos[e] - (a0[e] + rnd * slab), 1.0, 0.0) for e in range(ne)]
        return jnp.concatenate(blocks, axis=0).astype(BF16)

    def keep_partial(buf, e, rnd, old):
        grp = buf[pl.ds(e * slab + poff[e], BF16_ROWS), :]
        return jnp.where(jnp.logical_and(has[e], prnd[e] == rnd), grp, old)

    cur = stage.at[slot]
    cur[...] = jnp.dot(onehot(0), vb, preferred_element_type=F32).astype(BF16)
    for e in range(ne):
        cur[e * slab:e * slab + BF16_ROWS, :] += pend[pq + e]
    for e in range(ne):
        pend[pq + e] = keep_partial(cur, e, 0, jnp.zeros((BF16_ROWS, vb.shape[1]), BF16))

    def slab_bytes_wait(buf, sem):
        pltpu.make_async_copy(buf, buf, sem).wait()

    @pl.when(step >= PACK_WAYS)
    def _():
        old = (step + 1) % (PACK_WAYS + 1)
        slab_bytes_wait(stage.at[old], sems.at[old])

    @pl.when(s == 0)
    def _():
        stage_x[0:slab, :] = jnp.zeros((slab, stage_x.shape[1]), BF16)
        for e in range(ne):
            pltpu.make_async_copy(stage_x.at[0:slab], xs_hbm.at[e, b, pl.ds(cap, slab)], sem_x).start()
        slab_bytes_wait(stage_x, sem_x)

    for e in range(ne):
        pltpu.make_async_copy(stage.at[slot, e * slab:(e + 1) * slab],
                              xs_hbm.at[e, b, pl.ds(pl.multiple_of(a0[e], BF16_ROWS), slab)], sems.at[slot]).start()

    span = p1[0] - a0[0]
    for e in range(1, ne):
        span = jnp.maximum(span, p1[e] - a0[e])
    rounds = jnp.maximum(1, lax.shift_right_logical(span + (slab - 1), PACK_SHIFT))

    def extra(rnd, carry):
        stage_x[...] = jnp.dot(onehot(rnd), vb, preferred_element_type=F32).astype(BF16)
        for e in range(ne):
            pend[pq + e] = keep_partial(stage_x, e, rnd, pend[pq + e])
        for e in range(ne):
            @pl.when(p1[e] - a0[e] > rnd * slab)
            def _(e=e):
                cp = pltpu.make_async_copy(
                    stage_x.at[e * slab:(e + 1) * slab],
                    xs_hbm.at[e, b, pl.ds(pl.multiple_of(a0[e] + rnd * slab, BF16_ROWS), slab)], sem_x)
                cp.start()
                cp.wait()
        return carry

    lax.fori_loop(1, rounds, extra, 0)

    @pl.when(step == nsteps - 1)
    def _():
        for back in range(PACK_WAYS):
            last = (step + (PACK_WAYS + 1) - back) % (PACK_WAYS + 1)
            slab_bytes_wait(stage.at[last], sems.at[last])


def _pack_tokens(rs_ext, pos_r, v, cap):
    bsz, n, d = v.shape
    assert bsz % PACK_WAYS == 0
    ne = pos_r.shape[2]
    tpr = PACK_TOK // LANES
    return pl.pallas_call(
        functools.partial(_pack_body, cap=cap),
        grid_spec=pltpu.PrefetchScalarGridSpec(
            num_scalar_prefetch=1,
            grid=(bsz // PACK_WAYS, n // PACK_TOK, PACK_WAYS),
            in_specs=[pl.BlockSpec((1, tpr, ne, LANES), lambda bp, s, q, rs: (bp * PACK_WAYS + q, s, 0, 0)),
                      pl.BlockSpec((1, PACK_TOK, d), lambda bp, s, q, rs: (bp * PACK_WAYS + q, s, 0))],
            out_specs=pl.BlockSpec(memory_space=pl.ANY),
            scratch_shapes=[pltpu.VMEM((PACK_WAYS + 1, ne * PACK_SLAB, d), BF16),
                            pltpu.VMEM((ne * PACK_SLAB, d), BF16),
                            pltpu.VMEM((PACK_WAYS * ne, BF16_ROWS, d), BF16),
                            pltpu.SemaphoreType.DMA((PACK_WAYS + 1,)),
                            pltpu.SemaphoreType.DMA(())]),
        out_shape=jax.ShapeDtypeStruct((ne, bsz, cap + PACK_SLAB, d), BF16),
        compiler_params=_params("arbitrary", "arbitrary", "arbitrary"),
        name="pack_tokens",
    )(rs_ext, pos_r, v)


def _ffn_body(x_ref, gate_ref, wg_ref, wu_ref, wd_ref, ys_hbm, acc, stage, osems, *, mg, nf, sub, och):
    e = pl.program_id(0)
    f = pl.program_id(1)
    spg = x_ref.shape[2]

    @pl.when(f == 0)
    def _():
        acc[...] = jnp.zeros(acc.shape, F32)

    wgb = wg_ref[0].astype(BF16)
    wub = wu_ref[0].astype(BF16)
    wdb = wd_ref[0].astype(BF16)
    for i in range(mg // sub):
        rows = slice(i * sub, (i + 1) * sub)
        x = x_ref[0, (i * sub) // spg, (i * sub) % spg:(i * sub) % spg + sub, :]
        gg = jnp.dot(x, wgb, preferred_element_type=F32)
        uu = jnp.dot(x, wub, preferred_element_type=F32)
        h = (gg * jax.nn.sigmoid(gg) * uu).astype(BF16)
        acc[rows, :] += jnp.dot(h, wdb, preferred_element_type=F32)

    @pl.when(f == nf - 1)
    def _():
        gate_t = gate_ref[0].T

        def out_copy(c):
            return pltpu.make_async_copy(stage.at[c % 2], ys_hbm.at[e, pl.ds(c * och, och)], osems.at[c % 2])

        nchunks = mg // och
        for c in range(nchunks):
            if c >= 2:
                out_copy(c - 2).wait()
            for i in range(och // LANES):
                blk = c * (och // LANES) + i
                rows = slice(blk * LANES, (blk + 1) * LANES)
                stage[c % 2, i * LANES:(i + 1) * LANES, :] = (acc[rows, :] * gate_t[:, blk:blk + 1]).astype(BF16)
            out_copy(c).start()
        for c in range(max(nchunks - 2, 0), nchunks):
            out_copy(c).wait()


def _expert_ffn(xs, gate_rows, wg, wu, wd, tf):
    ne, d, fdim = wg.shape
    bsz, cap = xs.shape[1], gate_rows.shape[1] * LANES // xs.shape[1]
    mg = bsz * cap
    nf = fdim // tf
    sub = cap
    och = min(mg, 512)
    return pl.pallas_call(
        functools.partial(_ffn_body, mg=mg, nf=nf, sub=sub, och=och),
        grid_spec=pltpu.PrefetchScalarGridSpec(
            num_scalar_prefetch=0,
            grid=(ne, nf),
            in_specs=[pl.BlockSpec((1, bsz, cap, d), lambda e, f: (e, 0, 0, 0)),
                      pl.BlockSpec((1, mg // LANES, LANES), lambda e, f: (e, 0, 0)),
                      pl.BlockSpec((1, d, tf), lambda e, f: (e, 0, f)),
                      pl.BlockSpec((1, d, tf), lambda e, f: (e, 0, f)),
                      pl.BlockSpec((1, tf, d), lambda e, f: (e, f, 0))],
            out_specs=pl.BlockSpec(memory_space=pl.ANY),
            scratch_shapes=[pltpu.VMEM((mg, d), F32),
                            pltpu.VMEM((2, och, d), BF16),
                            pltpu.SemaphoreType.DMA((2,))]),
        out_shape=jax.ShapeDtypeStruct((ne, mg, d), BF16),
        compiler_params=_params("arbitrary", "arbitrary"),
        name="expert_ffn",
    )(xs, gate_rows, wg, wu, wd)


SLAB = PACK_SLAB
TILE_ROWS = PACK_TOK // LANES


def _combine_body(rs_ref, pos_ref, ys_hbm, x1_ref, g5_ref, fng_ref, o_ref, stk, stk_x, sems, sem_x, *, cap, tps):
    b = pl.program_id(0)
    s = pl.program_id(1)
    nb = pl.num_programs(0)
    ns = pl.num_programs(1)
    ne = pos_ref.shape[2]
    nr = ns * tps * TILE_ROWS
    tok = TILE_ROWS * LANES
    step = b * ns + s
    slot = step % 2
    tile_rows = ne * SLAB

    def first_pos(bb, rr, e):
        return rs_ref[(bb * ne + e) * (nr + 1) + rr]

    def aligned(p0):
        return lax.shift_left(lax.shift_right_logical(p0, 4), 4)

    def slab_start(a0, rnd):
        return pl.multiple_of(jnp.minimum(a0 + rnd * SLAB, cap - SLAB), BF16_ROWS)

    def slab_copy(bb, e, a, dst, row, sem):
        return pltpu.make_async_copy(ys_hbm.at[e, pl.ds(bb * cap + a, SLAB)], dst.at[pl.ds(row, SLAB)], sem)

    def issue(bb, ss, sl):
        for j in range(tps):
            for e in range(ne):
                a = slab_start(aligned(first_pos(bb, (ss * tps + j) * TILE_ROWS, e)), 0)
                slab_copy(bb, e, a, stk.at[sl], j * tile_rows + e * SLAB, sems.at[sl]).start()

    @pl.when(step == 0)
    def _():
        issue(b, s, slot)

    @pl.when(step + 1 < nb * ns)
    def _():
        nxt = step + 1
        issue(nxt // ns, nxt % ns, 1 - slot)

    kiota = lax.broadcasted_iota(jnp.int32, (SLAB, tok), 0)
    tn = (((0,), (0,)), ((), ()))
    pltpu.make_async_copy(stk.at[slot], stk.at[slot], sems.at[slot]).wait()

    for j in range(tps):
        r = (s * tps + j) * TILE_ROWS
        pos = jnp.concatenate([pos_ref[0, j * TILE_ROWS + q] for q in range(TILE_ROWS)], axis=1)
        a0 = [aligned(first_pos(b, r, e)) for e in range(ne)]

        def onehot(rnd, pos=pos, a0=a0):
            blocks = []
            for e in range(ne):
                rel = pos[e:e + 1, :] - slab_start(a0[e], rnd)
                fresh = (pos[e:e + 1, :] - a0[e]) >= rnd * SLAB
                blocks.append(jnp.where((kiota == rel) & fresh, 1.0, 0.0))
            return jnp.concatenate(blocks, axis=0).astype(BF16)

        moe = lax.dot_general(onehot(0), stk[slot, j * tile_rows:(j + 1) * tile_rows, :], tn,
                              preferred_element_type=F32)

        span = first_pos(b, r + TILE_ROWS, 0) - a0[0]
        for e in range(1, ne):
            span = jnp.maximum(span, first_pos(b, r + TILE_ROWS, e) - a0[e])
        rounds = jnp.maximum(1, lax.shift_right_logical(span + (SLAB - 1), PACK_SHIFT))

        def extra(rnd, m, a0=a0, onehot=onehot):
            for e in range(ne):
                slab_copy(b, e, slab_start(a0[e], rnd), stk_x, e * SLAB, sem_x).start()
            pltpu.make_async_copy(stk_x, stk_x, sem_x).wait()
            return m + lax.dot_general(onehot(rnd), stk_x[...], tn, preferred_element_type=F32)

        moe = lax.fori_loop(1, rounds, extra, moe)

        y = x1_ref[0, j * tok:(j + 1) * tok, :] + g5_ref[0] * moe
        ms = jnp.mean(y * y, axis=-1, keepdims=True)
        o_ref[0, j * tok:(j + 1) * tok, :] = y * lax.rsqrt(ms + RMS_EPS) * fng_ref[...]


def _combine_norm(rs_ext, pos_r, ys, x1, g5, fng, cap, tps):
    bsz, n, d = x1.shape
    g5, g5_spec = g5
    ns = n // (PACK_TOK * tps)
    ne = pos_r.shape[2]
    return pl.pallas_call(
        functools.partial(_combine_body, cap=cap, tps=tps),
        grid_spec=pltpu.PrefetchScalarGridSpec(
            num_scalar_prefetch=1,
            grid=(bsz, ns),
            in_specs=[pl.BlockSpec((1, tps * TILE_ROWS, ne, LANES), lambda b, s, rs: (b, s, 0, 0)),
                      pl.BlockSpec(memory_space=pl.ANY),
                      pl.BlockSpec((1, tps * PACK_TOK, d), lambda b, s, rs: (b, s, 0)),
                      g5_spec,
                      pl.BlockSpec((1, d), lambda b, s, rs: (0, 0))],
            out_specs=pl.BlockSpec((1, tps * PACK_TOK, d), lambda b, s, rs: (b, s, 0)),
            scratch_shapes=[pltpu.VMEM((2, tps * ne * SLAB, d), BF16),
                            pltpu.VMEM((ne * SLAB, d), BF16),
                            pltpu.SemaphoreType.DMA((2,)),
                            pltpu.SemaphoreType.DMA(())]),
        out_shape=jax.ShapeDtypeStruct((bsz, n, d), F32),
        compiler_params=_params("arbitrary", "arbitrary"),
        name="combine_norm",
    )(rs_ext, pos_r, ys, x1, g5, fng.reshape(1, d))


def _block_diag(w):
    heads, hd, _ = w.shape
    eye = jnp.eye(heads, dtype=w.dtype)
    return (eye[:, None, :, None] * w[:, :, None, :]).reshape(heads * hd, heads * hd)


def _tile(n, pref):
    return pref if n % pref == 0 else n


def kernel(x, c, ctx, c_ctx, norm1_g, norm2_g, ada_w, ada_b, w_in, b_in, conv_dw_w, conv_dw_b, conv_ln_g, conv_ln_b, lru_conv_w, lru_conv_b, lru_wa, lru_ba, lru_wi, lru_bi, lru_lambda, w_out, b_out, router_w, exp_w_gate, exp_w_up, exp_w_down, final_norm_g):
    assert norm1_g.shape[0] == 1
    mod, x1, v, aff = _mixer(x, c, ctx, c_ctx, norm1_g[0], norm2_g[0], ada_w[0], ada_b[0], w_in[0], b_in[0],
                             conv_dw_w[0], conv_dw_b[0], conv_ln_g[0], conv_ln_b[0], lru_conv_w[0],
                             lru_conv_b[0], lru_wa[0], lru_ba[0], lru_wi[0], lru_bi[0], lru_lambda[0],
                             w_out[0], b_out[0], router_w[0])
    return _moe_norm(x1, v, aff, mod[5], exp_w_gate[0], exp_w_up[0], exp_w_down[0], final_norm_g)


def _mixer(x, c, ctx, c_ctx, norm1_g, norm2_g, ada_w, ada_b, w_in, b_in, conv_w, conv_b, ln_g, ln_b,
           lru_cw, lru_cb, lru_wa, lru_ba, lru_wi, lru_bi, lru_lam, w_out, b_out, router_w):
    bsz, n, d = x.shape
    cond8 = jnp.zeros((SUBLANES, d), F32).at[:bsz].set(c).at[bsz].set(c_ctx)
    mods = _ada_mod(cond8, ada_w, ada_b)
    mods3 = mods.reshape(SUBLANES * N_MOD, 1, d)
    mod = [_mod_row(mods3, k) for k in range(N_MOD)]
    mod_c = [_mod_row(mods3, k, fixed_row=bsz) for k in range(2)]
    w_in_b = w_in.astype(BF16)
    w_out_b = w_out.astype(BF16)
    ch = lru_cb.shape[1]
    wgate = [jnp.concatenate([_block_diag(lru_wa[dd]), _block_diag(lru_wi[dd])], axis=1).astype(BF16)
             for dd in range(2)]
    bgate = [jnp.concatenate([lru_ba[dd], lru_bi[dd]]) for dd in range(2)]

    def lru(lx, h0, dd, reverse, merge_with=None):
        return _lru_dir(lx, h0, lru_cw[dd], lru_cb[dd], wgate[dd], bgate[dd], lru_lam[dd], reverse,
                        _tile(lx.shape[1], 256), merge_with)

    _, c_lx, _ = _in_proj(ctx, norm1_g, mod_c[0], mod_c[1], w_in_b, b_in, _tile(ctx.shape[1], 256))
    zero_h = jnp.zeros((bsz, 1, ch), F32)
    _, hf0 = lru(c_lx, zero_h, 0, False)
    _, hb0 = lru(c_lx, zero_h, 1, True)

    x_glu, x_lx, x_glg = _in_proj(x, norm1_g, mod[0], mod[1], w_in_b, b_in, _tile(n, 1024))
    hb, _ = lru(x_lx, hb0, 1, True)
    yl, _ = lru(x_lx, hf0, 0, False, merge_with=(hb, x_glg))
    conv_r, conv_c = _conv_grid(x_glu, conv_w, conv_b)
    x1, v, aff = _out_proj_route(conv_r, conv_c, ln_g, ln_b, yl, w_out_b, b_out, x, mod[2],
                                 norm2_g, mod[3], mod[4], router_w.T, _tile(n, 512))
    return mod, x1, v, aff


def _moe_norm(x1, v, aff, gate2, wg, wu, wd, final_norm_g):
    bsz, n, d = x1.shape
    ne = aff.shape[1]
    cap = EC_CAPACITY * n // ne
    gate, pos, rstart = _topk_route(aff.reshape(bsz, ne, n // LANES, LANES), cap)
    gate_rows = jnp.swapaxes(gate, 0, 1).reshape(ne, bsz * cap // LANES, LANES)
    rs_ext = jnp.concatenate([rstart[..., 0], jnp.full((bsz, ne, 1), cap, jnp.int32)], axis=-1).reshape(-1)
    pos_r = jnp.swapaxes(pos, 1, 2)
    xs = _pack_tokens(rs_ext, pos_r, v, cap)
    ys = _expert_ffn(xs, gate_rows, wg, wu, wd, 256)
    return _combine_norm(rs_ext, pos_r, ys, x1, gate2, final_norm_g, cap, 2)
```

```python
import functools

import jax
import jax.numpy as jnp
from jax import lax
from jax.experimental import pallas as pl
from jax.experimental.pallas import tpu as pltpu

GRID_W = 64
CONV_WIDTH = 31
CONV_PAD = (CONV_WIDTH - 1) // 2
LRU_CONV_WIDTH = 4
LRU_C = 8.0
N_EXPERTS = 16
EC_CAPACITY = 2
N_MOD = 6
RMS_EPS = 1e-6
LN_EPS = 1e-5

LANES = 128
SUBLANES = 8
BF16_ROWS = 16
VMEM_LIMIT = 56 * 1024 * 1024

F32 = jnp.float32
BF16 = jnp.bfloat16
HIGHEST = lax.Precision.HIGHEST


def _params(*sem):
    return pltpu.CompilerParams(dimension_semantics=sem, vmem_limit_bytes=VMEM_LIMIT)


def _mod_row(mods3, k, fixed_row=None):
    d = mods3.shape[2]

    def index_map(b, *_):
        return ((b if fixed_row is None else fixed_row) * N_MOD + k, 0, 0)

    return mods3, pl.BlockSpec((1, 1, d), index_map)


def _ada_body(c_ref, w_ref, b_ref, o_ref):
    s = c_ref[...]
    s = s * jax.nn.sigmoid(s)
    o_ref[...] = jnp.dot(s, w_ref[...], precision=HIGHEST, preferred_element_type=F32) + b_ref[...]


def _ada_mod(cond8, ada_w, ada_b):
    d, n = ada_w.shape
    tn = n // 4
    return pl.pallas_call(
        _ada_body,
        grid=(n // tn,),
        in_specs=[pl.BlockSpec((SUBLANES, d), lambda j: (0, 0)),
                  pl.BlockSpec((d, tn), lambda j: (0, j)),
                  pl.BlockSpec((1, tn), lambda j: (0, j))],
        out_specs=pl.BlockSpec((SUBLANES, tn), lambda j: (0, j)),
        out_shape=jax.ShapeDtypeStruct((SUBLANES, n), F32),
        compiler_params=_params("arbitrary"),
        name="ada_mod",
    )(cond8, ada_w, ada_b.reshape(1, n))


def _inproj_body(x_ref, g_ref, sh_ref, sc_ref, w_ref, b_ref, glu_ref, lx_ref, glg_ref, *, rb):
    gain = g_ref[...] * (1.0 + sc_ref[0])
    for i in range(x_ref.shape[1] // rb):
        rows = slice(i * rb, (i + 1) * rb)
        x = x_ref[0, rows, :]
        ms = jnp.mean(x * x, axis=-1, keepdims=True)
        u = x * lax.rsqrt(ms + RMS_EPS) * gain + sh_ref[0]
        p = jnp.dot(u.astype(BF16), w_ref[...], preferred_element_type=F32) + b_ref[...]
        cc = p.shape[1] // 4
        glu_ref[0, rows, :] = p[:, :cc] * jax.nn.sigmoid(p[:, cc:2 * cc])
        lx_ref[0, rows, :] = p[:, 2 * cc:3 * cc]
        glg_ref[0, rows, :] = jax.nn.gelu(p[:, 3 * cc:])


def _in_proj(x, norm_g, shift, scale, w_bf16, b_in, tm):
    bsz, n, d = x.shape
    (shift, shift_spec), (scale, scale_spec) = shift, scale
    n4 = w_bf16.shape[1]
    cc = n4 // 4
    tok = pl.BlockSpec((1, tm, cc), lambda b, i: (b, i, 0))
    return pl.pallas_call(
        functools.partial(_inproj_body, rb=min(tm, 256)),
        grid=(bsz, n // tm),
        in_specs=[pl.BlockSpec((1, tm, d), lambda b, i: (b, i, 0)),
                  pl.BlockSpec((1, d), lambda b, i: (0, 0)),
                  shift_spec, scale_spec,
                  pl.BlockSpec((d, n4), lambda b, i: (0, 0)),
                  pl.BlockSpec((1, n4), lambda b, i: (0, 0))],
        out_specs=[tok, tok, tok],
        out_shape=[jax.ShapeDtypeStruct((bsz, n, cc), F32)] * 3,
        compiler_params=_params("arbitrary", "arbitrary"),
        name="in_proj",
    )(x, norm_g.reshape(1, d), shift, scale, w_bf16, b_in.reshape(1, n4))


def _lru_body(*refs, reverse, tl, merge):
    x_ref, h0_ref, cw_ref, cb_ref, wg_ref, bg_ref, lam_ref = refs[:7]
    if merge:
        hother_ref, glg_ref = refs[7:9]
        refs = refs[2:]
    h_ref, hl_ref, a_ref, u_ref, carry_ref, halo_ref = refs[7:]
    i = pl.program_id(1)
    ch = x_ref.shape[2]

    @pl.when(i == 0)
    def _():
        carry_ref[...] = jnp.broadcast_to(h0_ref[0], carry_ref.shape)
        halo_ref[...] = jnp.zeros(halo_ref.shape, F32)

    x = x_ref[0]
    halo = halo_ref[...]
    row8 = lax.broadcasted_iota(jnp.int32, (SUBLANES, ch), 0)
    xc = cb_ref[...] + cw_ref[LRU_CONV_WIDTH - 1:LRU_CONV_WIDTH, :] * x
    for j in range(LRU_CONV_WIDTH - 1):
        s = LRU_CONV_WIDTH - 1 - j
        if not reverse:
            rolled = pltpu.roll(x, s, 0)
            edge = jnp.where(row8 < s, pltpu.roll(halo, s, 0), rolled[0:SUBLANES, :])
            tap = jnp.concatenate([edge, rolled[SUBLANES:, :]], axis=0)
        else:
            rolled = pltpu.roll(x, tl - s, 0)
            edge = jnp.where(row8 >= SUBLANES - s, pltpu.roll(halo, SUBLANES - s, 0), rolled[tl - SUBLANES:, :])
            tap = jnp.concatenate([rolled[:tl - SUBLANES, :], edge], axis=0)
        xc = xc + cw_ref[j:j + 1, :] * tap
    halo_ref[...] = x[0:SUBLANES, :] if reverse else x[tl - SUBLANES:tl, :]

    z = jnp.dot(xc.astype(BF16), wg_ref[...], preferred_element_type=F32) + bg_ref[...]
    half_c = (-0.5 * LRU_C) * jax.nn.softplus(-lam_ref[...])
    log_a = half_c * jnp.tanh(0.5 * z[:, :ch]) + half_c
    ig = 0.5 * jnp.tanh(0.5 * z[:, ch:]) + 0.5
    th = jnp.tanh(log_a)
    a_ref[...] = jnp.exp(log_a)
    m2 = -2.0 * th / (1.0 - th)
    u_ref[...] = jnp.where(m2 > 0.0, m2 * lax.rsqrt(m2), 0.0) * (ig * xc)

    ng = tl // SUBLANES
    rowid = lax.broadcasted_iota(jnp.int32, (SUBLANES, ch), 0)

    def group(gi, h):
        g = (ng - 1 - gi) if reverse else gi
        off = pl.multiple_of(g * SUBLANES, SUBLANES)
        a = a_ref[pl.ds(off, SUBLANES), :]
        u = u_ref[pl.ds(off, SUBLANES), :]
        for s in (1, 2, 4):
            if reverse:
                m = rowid < SUBLANES - s
                sh = SUBLANES - s
            else:
                m = rowid >= s
                sh = s
            ap = jnp.where(m, pltpu.roll(a, sh, 0), 1.0)
            up = jnp.where(m, pltpu.roll(u, sh, 0), 0.0)
            u = a * up + u
            a = a * ap
        hh = a * h + u
        if merge:
            u_ref[pl.ds(off, SUBLANES), :] = hh
        else:
            h_ref[0, pl.ds(off, SUBLANES), :] = hh
        edge = hh[0:1, :] if reverse else hh[SUBLANES - 1:SUBLANES, :]
        return jnp.broadcast_to(edge, (SUBLANES, ch))

    h = lax.fori_loop(0, ng, group, carry_ref[...], unroll=8)
    carry_ref[...] = h
    hl_ref[0] = h[0:1, :]
    if merge:
        h_ref[0] = ((u_ref[...] + hother_ref[0]) * glg_ref[0]).astype(BF16)


def _lru_dir(lx, h0, cw, cb, wgate, bgate, lam, reverse, tl, merge_with=None):
    bsz, n, ch = lx.shape
    nc = n // tl
    cmap = (lambda b, i: (b, nc - 1 - i, 0)) if reverse else (lambda b, i: (b, i, 0))
    full = lambda shape: pl.BlockSpec(shape, lambda b, i: (0,) * len(shape))
    merge = merge_with is not None
    extra = list(merge_with) if merge else []
    return pl.pallas_call(
        functools.partial(_lru_body, reverse=reverse, tl=tl, merge=merge),
        grid=(bsz, nc),
        in_specs=[pl.BlockSpec((1, tl, ch), cmap),
                  pl.BlockSpec((1, 1, ch), lambda b, i: (b, 0, 0)),
                  full((LRU_CONV_WIDTH, ch)), full((1, ch)),
                  full((ch, 2 * ch)), full((1, 2 * ch)), full((1, ch))]
                 + [pl.BlockSpec((1, tl, ch), cmap)] * len(extra),
        out_specs=[pl.BlockSpec((1, tl, ch), cmap),
                   pl.BlockSpec((1, 1, ch), lambda b, i: (b, 0, 0))],
        out_shape=[jax.ShapeDtypeStruct((bsz, n, ch), BF16 if merge else F32),
                   jax.ShapeDtypeStruct((bsz, 1, ch), F32)],
        scratch_shapes=[pltpu.VMEM((tl, ch), F32),
                        pltpu.VMEM((tl, ch), F32),
                        pltpu.VMEM((SUBLANES, ch), F32),
                        pltpu.VMEM((SUBLANES, ch), F32)],
        compiler_params=_params("arbitrary", "arbitrary"),
        name="lru_rev" if reverse else "lru_fwd",
    )(lx, h0, cw, cb.reshape(1, ch), wgate, bgate.reshape(1, 2 * ch), lam.reshape(1, ch), *extra)


ROW_STRIDE = GRID_W + 2 * BF16_ROWS


def _conv_row_body(x_ref, w_ref, b_ref, o_ref, pad_ref):
    n = x_ref.shape[1]
    nrows = n // GRID_W
    gap = jnp.zeros((BF16_ROWS, LANES), F32)

    def fill(r, c):
        base = pl.multiple_of(r * ROW_STRIDE, SUBLANES)
        src = pl.multiple_of(r * GRID_W, SUBLANES)
        pad_ref[pl.ds(base, BF16_ROWS), :] = gap
        pad_ref[pl.ds(base + BF16_ROWS, GRID_W), :] = x_ref[0, pl.ds(src, GRID_W), :]
        pad_ref[pl.ds(base + BF16_ROWS + GRID_W, BF16_ROWS), :] = gap
        return c

    lax.fori_loop(0, nrows, fill, 0)

    def row(r, c):
        base = pl.multiple_of(r * ROW_STRIDE, SUBLANES)
        acc = jnp.broadcast_to(b_ref[...], (GRID_W, LANES))
        for k in range(CONV_WIDTH):
            acc = acc + w_ref[k:k + 1, :] * pad_ref[pl.ds(base + BF16_ROWS - CONV_PAD + k, GRID_W), :]
        o_ref[0, pl.ds(pl.multiple_of(r * GRID_W, SUBLANES), GRID_W), :] = acc
        return c

    lax.fori_loop(0, nrows, row, 0, unroll=4)


def _conv_col_body(x_ref, w_ref, b_ref, o_ref, pad_ref):
    n = x_ref.shape[1]
    nrows = n // GRID_W
    halo = CONV_PAD * GRID_W
    pad_ref[0:halo, :] = jnp.zeros((halo, LANES), F32)
    pad_ref[halo + n:halo + n + halo, :] = jnp.zeros((halo, LANES), F32)

    def fill(r, c):
        src = pl.multiple_of(r * GRID_W, SUBLANES)
        pad_ref[pl.ds(halo + src, GRID_W), :] = x_ref[0, pl.ds(src, GRID_W), :]
        return c

    lax.fori_loop(0, nrows, fill, 0)

    def row(r, c):
        base = pl.multiple_of(r * GRID_W, SUBLANES)
        acc = jnp.broadcast_to(b_ref[...], (GRID_W, LANES))
        for k in range(CONV_WIDTH):
            acc = acc + w_ref[k:k + 1, :] * pad_ref[pl.ds(base + k * GRID_W, GRID_W), :]
        o_ref[0, pl.ds(base, GRID_W), :] = acc
        return c

    lax.fori_loop(0, nrows, row, 0, unroll=4)


def _conv_grid(glu, w, b):
    bsz, n, ch = glu.shape
    half = ch // 2
    ng = half // LANES
    outs = []
    for body, first, pad_rows in ((_conv_row_body, 0, (n // GRID_W) * ROW_STRIDE),
                                  (_conv_col_body, ng, n + 2 * CONV_PAD * GRID_W)):
        outs.append(pl.pallas_call(
            body,
            grid=(bsz, ng),
            in_specs=[pl.BlockSpec((1, n, LANES), lambda bb, g, first=first: (bb, 0, g + first)),
                      pl.BlockSpec((CONV_WIDTH, LANES), lambda bb, g, first=first: (0, g + first)),
                      pl.BlockSpec((1, LANES), lambda bb, g, first=first: (0, g + first))],
            out_specs=pl.BlockSpec((1, n, LANES), lambda bb, g: (bb, 0, g)),
            out_shape=jax.ShapeDtypeStruct((bsz, n, half), F32),
            scratch_shapes=[pltpu.VMEM((pad_rows, LANES), F32)],
            compiler_params=_params("arbitrary", "arbitrary"),
            name="conv_row" if first == 0 else "conv_col",
        )(glu, w, b.reshape(1, ch)))
    return outs


def _out_body(cr_ref, cc_ref, lng_ref, lnb_ref, yl_ref, wo_ref, bo_ref, x_ref,
              g1_ref, n2g_ref, sh2_ref, sc2_ref, rw_ref, x1_ref, v_ref, aff_ref):
    cv = jnp.concatenate([cr_ref[0], cc_ref[0]], axis=-1)
    mu = jnp.mean(cv, axis=-1, keepdims=True)
    dv = cv - mu
    var = jnp.mean(dv * dv, axis=-1, keepdims=True)
    yn = dv * lax.rsqrt(var + LN_EPS) * lng_ref[...] + lnb_ref[...]
    cy = yn * jax.nn.sigmoid(yn)
    cat = jnp.concatenate([cy.astype(BF16), yl_ref[0]], axis=-1)
    m = jnp.dot(cat, wo_ref[...], preferred_element_type=F32) + bo_ref[...]
    x1 = x_ref[0] + g1_ref[0] * m
    x1_ref[0] = x1
    ms = jnp.mean(x1 * x1, axis=-1, keepdims=True)
    v = x1 * lax.rsqrt(ms + RMS_EPS) * n2g_ref[...]
    v = v * (1.0 + sc2_ref[0]) + sh2_ref[0]
    v_ref[0] = v.astype(BF16)
    v_hi = v.astype(BF16)
    v_lo = (v - v_hi.astype(F32)).astype(BF16)
    rw = rw_ref[...]
    rw_hi = rw.astype(BF16)
    rw_lo = (rw - rw_hi.astype(F32)).astype(BF16)
    nt = (((1,), (1,)), ((), ()))
    ne = rw.shape[0]
    both = lax.dot_general(jnp.concatenate([rw_hi, rw_lo], axis=0), v_hi, nt, preferred_element_type=F32)
    lg = both[:ne] + both[ne:] + lax.dot_general(rw_hi, v_lo, nt, preferred_element_type=F32)
    ex = jnp.exp(lg - jnp.max(lg, axis=0, keepdims=True))
    aff_ref[0] = ex / jnp.sum(ex, axis=0, keepdims=True)


def _out_proj_route(conv_r, conv_c, ln_g, ln_b, yl, wo_bf16, b_out, x, g1, n2g, sh2, sc2, rw_t, tm):
    bsz, n, d = x.shape
    half = conv_r.shape[2]
    ch = yl.shape[2]
    ne = rw_t.shape[0]
    tok = lambda c: pl.BlockSpec((1, tm, c), lambda b, i: (b, i, 0))
    full = lambda shape: pl.BlockSpec(shape, lambda b, i: (0,) * len(shape))
    (g1, g1_spec), (sh2, sh2_spec), (sc2, sc2_spec) = g1, sh2, sc2
    return pl.pallas_call(
        _out_body,
        grid=(bsz, n // tm),
        in_specs=[tok(half), tok(half), full((1, 2 * half)), full((1, 2 * half)),
                  tok(ch), full((2 * half + ch, d)), full((1, d)), tok(d),
                  g1_spec, full((1, d)), sh2_spec, sc2_spec, full((ne, d))],
        out_specs=[tok(d), tok(d), pl.BlockSpec((1, ne, tm), lambda b, i: (b, 0, i))],
        out_shape=[jax.ShapeDtypeStruct((bsz, n, d), F32),
                   jax.ShapeDtypeStruct((bsz, n, d), BF16),
                   jax.ShapeDtypeStruct((bsz, ne, n), F32)],
        compiler_params=_params("arbitrary", "arbitrary"),
        name="out_proj_route",
    )(conv_r, conv_c, ln_g.reshape(1, -1), ln_b.reshape(1, -1), yl, wo_bf16,
      b_out.reshape(1, d), x, g1, n2g.reshape(1, d), sh2, sc2, rw_t)


def _token_cumsum(m, rows_per_expert):
    er = m.shape[0]
    li = lax.broadcasted_iota(jnp.int32, (LANES, LANES), 0)
    lj = lax.broadcasted_iota(jnp.int32, (LANES, LANES), 1)
    upper = jnp.where(li <= lj, 1.0, 0.0).astype(BF16)
    cs = jnp.dot(m.astype(BF16), upper, preferred_element_type=F32)
    rt = jnp.broadcast_to(cs[:, LANES - 1:LANES], (er, LANES)).astype(BF16)
    ri = lax.broadcasted_iota(jnp.int32, (er, er), 0)
    ci = lax.broadcasted_iota(jnp.int32, (er, er), 1)
    same = (ri // rows_per_expert) == (ci // rows_per_expert)
    lower = jnp.where(same & (ci < ri), 1.0, 0.0).astype(BF16)
    rstart = jnp.dot(lower, rt, preferred_element_type=F32)
    return cs, rstart


def _topk_body(aff_ref, gate_ref, pos_ref, rs_ref, cs_scr, rs_scr, *, cap):
    aff = aff_ref[0]
    ne, rows, _ = aff.shape

    def bisect(i, thr):
        cand = thr | jnp.left_shift(jnp.int32(1), 30 - i)
        cnt = jnp.sum((aff >= pltpu.bitcast(cand, F32)).astype(jnp.int32), axis=(1, 2), keepdims=True)
        return jnp.where(cnt >= cap, cand, thr)

    thr = lax.fori_loop(0, 31, bisect, jnp.zeros((ne, 1, 1), jnp.int32))
    gt = aff >= pltpu.bitcast(thr + 1, F32)
    eq = (aff >= pltpu.bitcast(thr, F32)) & jnp.logical_not(gt)
    need = (cap - jnp.sum(gt.astype(jnp.int32), axis=(1, 2), keepdims=True)).astype(F32)
    eqf = jnp.where(eq, 1.0, 0.0)
    rank_in_row, rank_row0 = _token_cumsum(eqf.reshape(ne * rows, LANES), rows)
    rank_excl = (rank_in_row + rank_row0).reshape(ne, rows, LANES) - eqf
    sel = gt | (eq & (rank_excl < need))
    self_ = jnp.where(sel, 1.0, 0.0)
    cs, rstart = _token_cumsum(self_.reshape(ne * rows, LANES), rows)
    cs3 = cs.reshape(ne, rows, LANES)
    rstart3 = rstart.reshape(ne, rows, LANES)
    cs_scr[...] = cs3
    rs_scr[...] = rstart3
    pos_ref[0] = jnp.where(sel, cs3 + rstart3 - 1.0, -1.0).astype(jnp.int32)
    rs_ref[0] = rstart3.astype(jnp.int32)

    jrow = lax.broadcasted_iota(jnp.int32, (1, cap), 1).astype(F32)
    sub_r = lax.broadcasted_iota(jnp.int32, (rows, cap), 0).astype(F32)
    sub_l = lax.broadcasted_iota(jnp.int32, (LANES, cap), 0).astype(F32)

    def per_expert(e, c):
        cl = cs_scr[e]
        af = aff_ref[0, e]
        rowtot = cl[:, LANES - 1:LANES]
        before = rowtot + rs_scr[e][:, 0:1] <= jrow
        rj = jnp.sum(jnp.where(before, 1.0, 0.0), axis=0, keepdims=True)
        rowbase = jnp.sum(jnp.where(before, rowtot, 0.0), axis=0, keepdims=True)
        onehot = jnp.where(sub_r == rj, 1.0, 0.0).astype(BF16)
        a1 = af.astype(BF16)
        r1 = af - a1.astype(F32)
        a2 = r1.astype(BF16)
        a3 = (r1 - a2.astype(F32)).astype(BF16)
        lhs = jnp.concatenate([cl.astype(BF16), a1, a2, a3], axis=1)
        gathered = lax.dot_general(lhs, onehot, (((0,), (0,)), ((), ())), preferred_element_type=F32)
        g = gathered[:LANES]
        ga = gathered[LANES:2 * LANES] + gathered[2 * LANES:3 * LANES] + gathered[3 * LANES:]
        lanepos = jnp.sum(jnp.where(g <= jrow - rowbase, 1.0, 0.0), axis=0, keepdims=True)
        gate = jnp.sum(jnp.where(sub_l == lanepos, ga, 0.0), axis=0, keepdims=True)
        gate_ref[0, pl.ds(e, 1), :] = gate
        return c

    lax.fori_loop(0, ne, per_expert, 0)


def _topk_route(aff4, cap):
    bsz, ne, rows, _ = aff4.shape
    blk4 = pl.BlockSpec((1, ne, rows, LANES), lambda b: (b, 0, 0, 0))
    lst = pl.BlockSpec((1, ne, cap), lambda b: (b, 0, 0))
    return pl.pallas_call(
        functools.partial(_topk_body, cap=cap),
        grid=(bsz,),
        in_specs=[blk4],
        out_specs=[lst, blk4, blk4],
        out_shape=[jax.ShapeDtypeStruct((bsz, ne, cap), F32),
                   jax.ShapeDtypeStruct((bsz, ne, rows, LANES), jnp.int32),
                   jax.ShapeDtypeStruct((bsz, ne, rows, LANES), jnp.int32)],
        scratch_shapes=[pltpu.VMEM((ne, rows, LANES), F32), pltpu.VMEM((ne, rows, LANES), F32)],
        compiler_params=_params("arbitrary"),
        name="topk_route",
    )(aff4)


PACK_TOK = 256
PACK_SLAB = 64
PACK_SHIFT = 6
PACK_WAYS = 2


def _pack_body(rs_ref, pos_ref, v_ref, xs_hbm, stage, stage_x, pend, sems, sem_x, *, cap):
    bp = pl.program_id(0)
    s = pl.program_id(1)
    q = pl.program_id(2)
    ns = pl.num_programs(1)
    b = bp * PACK_WAYS + q
    ne = pos_ref.shape[2]
    tpr = PACK_TOK // LANES
    nr = ns * tpr
    step = (bp * ns + s) * PACK_WAYS + q
    nsteps = pl.num_programs(0) * ns * PACK_WAYS
    slot = step % (PACK_WAYS + 1)
    slab = PACK_SLAB
    pq = q * ne

    @pl.when(s == 0)
    def _():
        for e in range(ne):
            pend[pq + e] = jnp.zeros(pend.shape[1:], BF16)

    def first_pos(rr, e):
        return rs_ref[(b * ne + e) * (nr + 1) + rr]

    def floor16(p):
        return lax.shift_left(lax.shift_right_logical(p, 4), 4)

    p0 = [first_pos(s * tpr, e) for e in range(ne)]
    p1 = [first_pos(s * tpr + tpr, e) for e in range(ne)]
    a0 = [floor16(p0[e]) for e in range(ne)]
    a1 = [floor16(p1[e]) for e in range(ne)]
    has = [a1[e] < p1[e] for e in range(ne)]
    prnd = [lax.shift_right_logical(a1[e] - a0[e], PACK_SHIFT) for e in range(ne)]
    poff = [pl.multiple_of(jnp.bitwise_and(a1[e] - a0[e], slab - 1), BF16_ROWS) for e in range(ne)]

    vb = v_ref[0]
    pos = [jnp.concatenate([pos_ref[0, j, e:e + 1, :] for j in range(tpr)], axis=1) for e in range(ne)]
    kiota = lax.broadcasted_iota(jnp.int32, (slab, PACK_TOK), 0)

    def onehot(rnd):
        blocks = [jnp.where(kiota == pos[e] - (a0[e] + rnd * slab), 1.0, 0.0) for e in range(ne)]
        return jnp.concatenate(blocks, axis=0).astype(BF16)

    def keep_partial(buf, e, rnd, old):
        grp = buf[pl.ds(e * slab + poff[e], BF16_ROWS), :]
        return jnp.where(jnp.logical_and(has[e], prnd[e] == rnd), grp, old)

    cur = stage.at[slot]
    cur[...] = jnp.dot(onehot(0), vb, preferred_element_type=F32).astype(BF16)
    for e in range(ne):
        cur[e * slab:e * slab + BF16_ROWS, :] += pend[pq + e]
    for e in range(ne):
        pend[pq + e] = keep_partial(cur, e, 0, jnp.zeros((BF16_ROWS, vb.shape[1]), BF16))

    def slab_bytes_wait(buf, sem):
        pltpu.make_async_copy(buf, buf, sem).wait()

    @pl.when(step >= PACK_WAYS)
    def _():
        old = (step + 1) % (PACK_WAYS + 1)
        slab_bytes_wait(stage.at[old], sems.at[old])

    @pl.when(s == 0)
    def _():
        stage_x[0:slab, :] = jnp.zeros((slab, stage_x.shape[1]), BF16)
        for e in range(ne):
            pltpu.make_async_copy(stage_x.at[0:slab], xs_hbm.at[e, b, pl.ds(cap, slab)], sem_x).start()
        slab_bytes_wait(stage_x, sem_x)

    for e in range(ne):
        pltpu.make_async_copy(stage.at[slot, e * slab:(e + 1) * slab],
                              xs_hbm.at[e, b, pl.ds(pl.multiple_of(a0[e], BF16_ROWS), slab)], sems.at[slot]).start()

    span = p1[0] - a0[0]
    for e in range(1, ne):
        span = jnp.maximum(span, p1[e] - a0[e])
    rounds = jnp.maximum(1, lax.shift_right_logical(span + (slab - 1), PACK_SHIFT))

    def extra(rnd, carry):
        stage_x[...] = jnp.dot(onehot(rnd), vb, preferred_element_type=F32).astype(BF16)
        for e in range(ne):
            pend[pq + e] = keep_partial(stage_x, e, rnd, pend[pq + e])
        for e in range(ne):
            @pl.when(p1[e] - a0[e] > rnd * slab)
            def _(e=e):
                cp = pltpu.make_async_copy(
                    stage_x.at[e * slab:(e + 1) * slab],
                    xs_hbm.at[e, b, pl.ds(pl.multiple_of(a0[e] + rnd * slab, BF16_ROWS), slab)], sem_x)
                cp.start()
                cp.wait()
        return carry

    lax.fori_loop(1, rounds, extra, 0)

    @pl.when(step == nsteps - 1)
    def _():
        for back in range(PACK_WAYS):
            last = (step + (PACK_WAYS + 1) - back) % (PACK_WAYS + 1)
            slab_bytes_wait(stage.at[last], sems.at[last])


def _pack_tokens(rs_ext, pos_r, v, cap):
    bsz, n, d = v.shape
    assert bsz % PACK_WAYS == 0
    ne = pos_r.shape[2]
    tpr = PACK_TOK // LANES
    return pl.pallas_call(
        functools.partial(_pack_body, cap=cap),
        grid_spec=pltpu.PrefetchScalarGridSpec(
            num_scalar_prefetch=1,
            grid=(bsz // PACK_WAYS, n // PACK_TOK, PACK_WAYS),
            in_specs=[pl.BlockSpec((1, tpr, ne, LANES), lambda bp, s, q, rs: (bp * PACK_WAYS + q, s, 0, 0)),
                      pl.BlockSpec((1, PACK_TOK, d), lambda bp, s, q, rs: (bp * PACK_WAYS + q, s, 0))],
            out_specs=pl.BlockSpec(memory_space=pl.ANY),
            scratch_shapes=[pltpu.VMEM((PACK_WAYS + 1, ne * PACK_SLAB, d), BF16),
                            pltpu.VMEM((ne * PACK_SLAB, d), BF16),
                            pltpu.VMEM((PACK_WAYS * ne, BF16_ROWS, d), BF16),
                            pltpu.SemaphoreType.DMA((PACK_WAYS + 1,)),
                            pltpu.SemaphoreType.DMA(())]),
        out_shape=jax.ShapeDtypeStruct((ne, bsz, cap + PACK_SLAB, d), BF16),
        compiler_params=_params("arbitrary", "arbitrary", "arbitrary"),
        name="pack_tokens",
    )(rs_ext, pos_r, v)


def _ffn_body(x_ref, gate_ref, wg_ref, wu_ref, wd_ref, ys_hbm, acc, stage, osems, *, mg, nf, sub, och):
    e = pl.program_id(0)
    f = pl.program_id(1)
    spg = x_ref.shape[2]

    @pl.when(f == 0)
    def _():
        acc[...] = jnp.zeros(acc.shape, F32)

    wgb = wg_ref[0].astype(BF16)
    wub = wu_ref[0].astype(BF16)
    wdb = wd_ref[0].astype(BF16)
    for i in range(mg // sub):
        rows = slice(i * sub, (i + 1) * sub)
        x = x_ref[0, (i * sub) // spg, (i * sub) % spg:(i * sub) % spg + sub, :]
        gg = jnp.dot(x, wgb, preferred_element_type=F32)
        uu = jnp.dot(x, wub, preferred_element_type=F32)
        h = (gg * jax.nn.sigmoid(gg) * uu).astype(BF16)
        acc[rows, :] += jnp.dot(h, wdb, preferred_element_type=F32)

    @pl.when(f == nf - 1)
    def _():
        gate_t = gate_ref[0].T

        def out_copy(c):
            return pltpu.make_async_copy(stage.at[c % 2], ys_hbm.at[e, pl.ds(c * och, och)], osems.at[c % 2])

        nchunks = mg // och
        for c in range(nchunks):
            if c >= 2:
                out_copy(c - 2).wait()
            for i in range(och // LANES):
                blk = c * (och // LANES) + i
                rows = slice(blk * LANES, (blk + 1) * LANES)
                stage[c % 2, i * LANES:(i + 1) * LANES, :] = (acc[rows, :] * gate_t[:, blk:blk + 1]).astype(BF16)
            out_copy(c).start()
        for c in range(max(nchunks - 2, 0), nchunks):
            out_copy(c).wait()


def _expert_ffn(xs, gate_rows, wg, wu, wd, tf):
    ne, d, fdim = wg.shape
    bsz, cap = xs.shape[1], gate_rows.shape[1] * LANES // xs.shape[1]
    mg = bsz * cap
    nf = fdim // tf
    sub = cap
    och = min(mg, 512)
    return pl.pallas_call(
        functools.partial(_ffn_body, mg=mg, nf=nf, sub=sub, och=och),
        grid_spec=pltpu.PrefetchScalarGridSpec(
            num_scalar_prefetch=0,
            grid=(ne, nf),
            in_specs=[pl.BlockSpec((1, bsz, cap, d), lambda e, f: (e, 0, 0, 0)),
                      pl.BlockSpec((1, mg // LANES, LANES), lambda e, f: (e, 0, 0)),
                      pl.BlockSpec((1, d, tf), lambda e, f: (e, 0, f)),
                      pl.BlockSpec((1, d, tf), lambda e, f: (e, 0, f)),
                      pl.BlockSpec((1, tf, d), lambda e, f: (e, f, 0))],
            out_specs=pl.BlockSpec(memory_space=pl.ANY),
            scratch_shapes=[pltpu.VMEM((mg, d), F32),
                            pltpu.VMEM((2, och, d), BF16),
                            pltpu.SemaphoreType.DMA((2,))]),
        out_shape=jax.ShapeDtypeStruct((ne, mg, d), BF16),
        compiler_params=_params("arbitrary", "arbitrary"),
        name="expert_ffn",
    )(xs, gate_rows, wg, wu, wd)


SLAB = PACK_SLAB
TILE_ROWS = PACK_TOK // LANES


def _combine_body(rs_ref, pos_ref, ys_hbm, x1_ref, g5_ref, fng_ref, o_ref, stk, stk_x, sems, sem_x, *, cap, tps):
    b = pl.program_id(0)
    s = pl.program_id(1)
    nb = pl.num_programs(0)
    ns = pl.num_programs(1)
    ne = pos_ref.shape[2]
    nr = ns * tps * TILE_ROWS
    tok = TILE_ROWS * LANES
    step = b * ns + s
    slot = step % 2
    tile_rows = ne * SLAB

    def first_pos(bb, rr, e):
        return rs_ref[(bb * ne + e) * (nr + 1) + rr]

    def aligned(p0):
        return lax.shift_left(lax.shift_right_logical(p0, 4), 4)

    def slab_start(a0, rnd):
        return pl.multiple_of(jnp.minimum(a0 + rnd * SLAB, cap - SLAB), BF16_ROWS)

    def slab_copy(bb, e, a, dst, row, sem):
        return pltpu.make_async_copy(ys_hbm.at[e, pl.ds(bb * cap + a, SLAB)], dst.at[pl.ds(row, SLAB)], sem)

    def issue(bb, ss, sl):
        for j in range(tps):
            for e in range(ne):
                a = slab_start(aligned(first_pos(bb, (ss * tps + j) * TILE_ROWS, e)), 0)
                slab_copy(bb, e, a, stk.at[sl], j * tile_rows + e * SLAB, sems.at[sl]).start()

    @pl.when(step == 0)
    def _():
        issue(b, s, slot)

    @pl.when(step + 1 < nb * ns)
    def _():
        nxt = step + 1
        issue(nxt // ns, nxt % ns, 1 - slot)

    kiota = lax.broadcasted_iota(jnp.int32, (SLAB, tok), 0)
    tn = (((0,), (0,)), ((), ()))
    pltpu.make_async_copy(stk.at[slot], stk.at[slot], sems.at[slot]).wait()

    for j in range(tps):
        r = (s * tps + j) * TILE_ROWS
        pos = jnp.concatenate([pos_ref[0, j * TILE_ROWS + q] for q in range(TILE_ROWS)], axis=1)
        a0 = [aligned(first_pos(b, r, e)) for e in range(ne)]

        def onehot(rnd, pos=pos, a0=a0):
            blocks = []
            for e in range(ne):
                rel = pos[e:e + 1, :] - slab_start(a0[e], rnd)
                fresh = (pos[e:e + 1, :] - a0[e]) >= rnd * SLAB
                blocks.append(jnp.where((kiota == rel) & fresh, 1.0, 0.0))
            return jnp.concatenate(blocks, axis=0).astype(BF16)

        moe = lax.dot_general(onehot(0), stk[slot, j * tile_rows:(j + 1) * tile_rows, :], tn,
                              preferred_element_type=F32)

        span = first_pos(b, r + TILE_ROWS, 0) - a0[0]
        for e in range(1, ne):
            span = jnp.maximum(span, first_pos(b, r + TILE_ROWS, e) - a0[e])
        rounds = jnp.maximum(1, lax.shift_right_logical(span + (SLAB - 1), PACK_SHIFT))

        def extra(rnd, m, a0=a0, onehot=onehot):
            for e in range(ne):
                slab_copy(b, e, slab_start(a0[e], rnd), stk_x, e * SLAB, sem_x).start()
            pltpu.make_async_copy(stk_x, stk_x, sem_x).wait()
            return m + lax.dot_general(onehot(rnd), stk_x[...], tn, preferred_element_type=F32)

        moe = lax.fori_loop(1, rounds, extra, moe)

        y = x1_ref[0, j * tok:(j + 1) * tok, :] + g5_ref[0] * moe
        ms = jnp.mean(y * y, axis=-1, keepdims=True)
        o_ref[0, j * tok:(j + 1) * tok, :] = y * lax.rsqrt(ms + RMS_EPS) * fng_ref[...]


def _combine_norm(rs_ext, pos_r, ys, x1, g5, fng, cap, tps):
    bsz, n, d = x1.shape
    g5, g5_spec = g5
    ns = n // (PACK_TOK * tps)
    ne = pos_r.shape[2]
    return pl.pallas_call(
        functools.partial(_combine_body, cap=cap, tps=tps),
        grid_spec=pltpu.PrefetchScalarGridSpec(
            num_scalar_prefetch=1,
            grid=(bsz, ns),
            in_specs=[pl.BlockSpec((1, tps * TILE_ROWS, ne, LANES), lambda b, s, rs: (b, s, 0, 0)),
                      pl.BlockSpec(memory_space=pl.ANY),
                      pl.BlockSpec((1, tps * PACK_TOK, d), lambda b, s, rs: (b, s, 0)),
                      g5_spec,
                      pl.BlockSpec((1, d), lambda b, s, rs: (0, 0))],
            out_specs=pl.BlockSpec((1, tps * PACK_TOK, d), lambda b, s, rs: (b, s, 0)),
            scratch_shapes=[pltpu.VMEM((2, tps * ne * SLAB, d), BF16),
                            pltpu.VMEM((ne * SLAB, d), BF16),
                            pltpu.SemaphoreType.DMA((2,)),
                            pltpu.SemaphoreType.DMA(())]),
        out_shape=jax.ShapeDtypeStruct((bsz, n, d), F32),
        compiler_params=_params("arbitrary", "arbitrary"),
        name="combine_norm",
    )(rs_ext, pos_r, ys, x1, g5, fng.reshape(1, d))


def _block_diag(w):
    heads, hd, _ = w.shape
    eye = jnp.eye(heads, dtype=w.dtype)
    return (eye[:, None, :, None] * w[:, :, None, :]).reshape(heads * hd, heads * hd)


IN_PROJ_ROWS = 1024
OUT_PROJ_ROWS = 512
LRU_CHUNK = 256
FFN_COLS = 256
COMBINE_TILES = 2


def _tile(n, pref):
    return pref if n % pref == 0 else n


def kernel(x, c, ctx, c_ctx, norm1_g, norm2_g, ada_w, ada_b, w_in, b_in, conv_dw_w, conv_dw_b, conv_ln_g, conv_ln_b, lru_conv_w, lru_conv_b, lru_wa, lru_ba, lru_wi, lru_bi, lru_lambda, w_out, b_out, router_w, exp_w_gate, exp_w_up, exp_w_down, final_norm_g):
    assert norm1_g.shape[0] == 1
    mod, x1, v, aff = _mixer(x, c, ctx, c_ctx, norm1_g[0], norm2_g[0], ada_w[0], ada_b[0], w_in[0], b_in[0],
                             conv_dw_w[0], conv_dw_b[0], conv_ln_g[0], conv_ln_b[0], lru_conv_w[0],
                             lru_conv_b[0], lru_wa[0], lru_ba[0], lru_wi[0], lru_bi[0], lru_lambda[0],
                             w_out[0], b_out[0], router_w[0])
    return _moe_norm(x1, v, aff, mod[5], exp_w_gate[0], exp_w_up[0], exp_w_down[0], final_norm_g)


def _mixer(x, c, ctx, c_ctx, norm1_g, norm2_g, ada_w, ada_b, w_in, b_in, conv_w, conv_b, ln_g, ln_b,
           lru_cw, lru_cb, lru_wa, lru_ba, lru_wi, lru_bi, lru_lam, w_out, b_out, router_w):
    bsz, n, d = x.shape
    cond8 = jnp.zeros((SUBLANES, d), F32).at[:bsz].set(c).at[bsz].set(c_ctx)
    mods = _ada_mod(cond8, ada_w, ada_b)
    mods3 = mods.reshape(SUBLANES * N_MOD, 1, d)
    mod = [_mod_row(mods3, k) for k in range(N_MOD)]
    mod_c = [_mod_row(mods3, k, fixed_row=bsz) for k in range(2)]
    w_in_b = w_in.astype(BF16)
    w_out_b = w_out.astype(BF16)
    ch = lru_cb.shape[1]
    wgate = [jnp.concatenate([_block_diag(lru_wa[dd]), _block_diag(lru_wi[dd])], axis=1).astype(BF16)
             for dd in range(2)]
    bgate = [jnp.concatenate([lru_ba[dd], lru_bi[dd]]) for dd in range(2)]

    def lru(lx, h0, dd, reverse, merge_with=None):
        return _lru_dir(lx, h0, lru_cw[dd], lru_cb[dd], wgate[dd], bgate[dd], lru_lam[dd], reverse,
                        _tile(lx.shape[1], LRU_CHUNK), merge_with)

    _, c_lx, _ = _in_proj(ctx, norm1_g, mod_c[0], mod_c[1], w_in_b, b_in, _tile(ctx.shape[1], IN_PROJ_ROWS))
    zero_h = jnp.zeros((bsz, 1, ch), F32)
    _, hf0 = lru(c_lx, zero_h, 0, False)
    _, hb0 = lru(c_lx, zero_h, 1, True)

    x_glu, x_lx, x_glg = _in_proj(x, norm1_g, mod[0], mod[1], w_in_b, b_in, _tile(n, IN_PROJ_ROWS))
    hb, _ = lru(x_lx, hb0, 1, True)
    yl, _ = lru(x_lx, hf0, 0, False, merge_with=(hb, x_glg))
    conv_r, conv_c = _conv_grid(x_glu, conv_w, conv_b)
    x1, v, aff = _out_proj_route(conv_r, conv_c, ln_g, ln_b, yl, w_out_b, b_out, x, mod[2],
                                 norm2_g, mod[3], mod[4], router_w.T, _tile(n, OUT_PROJ_ROWS))
    return mod, x1, v, aff


def _moe_norm(x1, v, aff, gate2, wg, wu, wd, final_norm_g):
    bsz, n, d = x1.shape
    ne = aff.shape[1]
    cap = EC_CAPACITY * n // ne
    gate, pos, rstart = _topk_route(aff.reshape(bsz, ne, n // LANES, LANES), cap)
    gate_rows = jnp.swapaxes(gate, 0, 1).reshape(ne, bsz * cap // LANES, LANES)
    rs_ext = jnp.concatenate([rstart[..., 0], jnp.full((bsz, ne, 1), cap, jnp.int32)], axis=-1).reshape(-1)
    pos_r = jnp.swapaxes(pos, 1, 2)
    xs = _pack_tokens(rs_ext, pos_r, v, cap)
    ys = _expert_ffn(xs, gate_rows, wg, wu, wd, FFN_COLS)
    return _combine_norm(rs_ext, pos_r, ys, x1, gate2, final_norm_g, cap, COMBINE_TILES)
```

```python
import functools

import jax
import jax.numpy as jnp
from jax import lax
from jax.experimental import pallas as pl
from jax.experimental.pallas import tpu as pltpu

GRID_W = 64
CONV_WIDTH = 31
CONV_PAD = (CONV_WIDTH - 1) // 2
LRU_CONV_WIDTH = 4
LRU_C = 8.0
N_EXPERTS = 16
EC_CAPACITY = 2
N_MOD = 6
RMS_EPS = 1e-6
LN_EPS = 1e-5

LANES = 128
SUBLANES = 8
BF16_ROWS = 16
VMEM_LIMIT = 56 * 1024 * 1024

F32 = jnp.float32
BF16 = jnp.bfloat16


def _params(*sem):
    return pltpu.CompilerParams(dimension_semantics=sem, vmem_limit_bytes=VMEM_LIMIT)


def _mod_row(mods3, k, fixed_row=None):
    d = mods3.shape[2]

    def index_map(b, *_):
        return ((b if fixed_row is None else fixed_row) * N_MOD + k, 0, 0)

    return mods3, pl.BlockSpec((1, 1, d), index_map)


def _ada_body(c_ref, w_ref, b_ref, o_ref):
    s = c_ref[...]
    s = s * jax.nn.sigmoid(s)
    w = w_ref[...]
    s_hi = s.astype(BF16)
    s_lo = (s - s_hi.astype(F32)).astype(BF16)
    w_hi = w.astype(BF16)
    w_lo = (w - w_hi.astype(F32)).astype(BF16)
    rows = s.shape[0]
    both = jnp.dot(jnp.concatenate([s_hi, s_lo], axis=0), w_hi, preferred_element_type=F32)
    o_ref[...] = both[:rows] + both[rows:] + jnp.dot(s_hi, w_lo, preferred_element_type=F32) + b_ref[...]


def _ada_mod(cond8, ada_w, ada_b):
    d, n = ada_w.shape
    tn = n // 4
    return pl.pallas_call(
        _ada_body,
        grid=(n // tn,),
        in_specs=[pl.BlockSpec((SUBLANES, d), lambda j: (0, 0)),
                  pl.BlockSpec((d, tn), lambda j: (0, j)),
                  pl.BlockSpec((1, tn), lambda j: (0, j))],
        out_specs=pl.BlockSpec((SUBLANES, tn), lambda j: (0, j)),
        out_shape=jax.ShapeDtypeStruct((SUBLANES, n), F32),
        compiler_params=_params("arbitrary"),
        name="ada_mod",
    )(cond8, ada_w, ada_b.reshape(1, n))


def _inproj_body(x_ref, g_ref, sh_ref, sc_ref, w_ref, b_ref, glu_ref, lx_ref, glg_ref, *, rb):
    gain = g_ref[...] * (1.0 + sc_ref[0])
    for i in range(x_ref.shape[1] // rb):
        rows = slice(i * rb, (i + 1) * rb)
        x = x_ref[0, rows, :]
        ms = jnp.mean(x * x, axis=-1, keepdims=True)
        u = x * lax.rsqrt(ms + RMS_EPS) * gain + sh_ref[0]
        p = jnp.dot(u.astype(BF16), w_ref[...], preferred_element_type=F32) + b_ref[...]
        cc = p.shape[1] // 4
        glu_ref[0, rows, :] = p[:, :cc] * jax.nn.sigmoid(p[:, cc:2 * cc])
        lx_ref[0, rows, :] = p[:, 2 * cc:3 * cc]
        glg_ref[0, rows, :] = jax.nn.gelu(p[:, 3 * cc:])


def _in_proj(x, norm_g, shift, scale, w_bf16, b_in, tm):
    bsz, n, d = x.shape
    (shift, shift_spec), (scale, scale_spec) = shift, scale
    n4 = w_bf16.shape[1]
    cc = n4 // 4
    tok = pl.BlockSpec((1, tm, cc), lambda b, i: (b, i, 0))
    return pl.pallas_call(
        functools.partial(_inproj_body, rb=min(tm, 256)),
        grid=(bsz, n // tm),
        in_specs=[pl.BlockSpec((1, tm, d), lambda b, i: (b, i, 0)),
                  pl.BlockSpec((1, d), lambda b, i: (0, 0)),
                  shift_spec, scale_spec,
                  pl.BlockSpec((d, n4), lambda b, i: (0, 0)),
                  pl.BlockSpec((1, n4), lambda b, i: (0, 0))],
        out_specs=[tok, tok, tok],
        out_shape=[jax.ShapeDtypeStruct((bsz, n, cc), F32)] * 3,
        compiler_params=_params("arbitrary", "arbitrary"),
        name="in_proj",
    )(x, norm_g.reshape(1, d), shift, scale, w_bf16, b_in.reshape(1, n4))


def _lru_body(*refs, reverse, tl, merge):
    x_ref, h0_ref, cw_ref, cb_ref, wg_ref, bg_ref, lam_ref = refs[:7]
    if merge:
        hother_ref, glg_ref = refs[7:9]
        refs = refs[2:]
    h_ref, hl_ref, a_ref, u_ref, carry_ref, halo_ref = refs[7:]
    i = pl.program_id(1)
    ch = x_ref.shape[2]

    @pl.when(i == 0)
    def _():
        carry_ref[...] = jnp.broadcast_to(h0_ref[0], carry_ref.shape)
        halo_ref[...] = jnp.zeros(halo_ref.shape, F32)

    x = x_ref[0]
    halo = halo_ref[...]
    row8 = lax.broadcasted_iota(jnp.int32, (SUBLANES, ch), 0)
    xc = cb_ref[...] + cw_ref[LRU_CONV_WIDTH - 1:LRU_CONV_WIDTH, :] * x
    for j in range(LRU_CONV_WIDTH - 1):
        s = LRU_CONV_WIDTH - 1 - j
        if not reverse:
            rolled = pltpu.roll(x, s, 0)
            edge = jnp.where(row8 < s, pltpu.roll(halo, s, 0), rolled[0:SUBLANES, :])
            tap = jnp.concatenate([edge, rolled[SUBLANES:, :]], axis=0)
        else:
            rolled = pltpu.roll(x, tl - s, 0)
            edge = jnp.where(row8 >= SUBLANES - s, pltpu.roll(halo, SUBLANES - s, 0), rolled[tl - SUBLANES:, :])
            tap = jnp.concatenate([rolled[:tl - SUBLANES, :], edge], axis=0)
        xc = xc + cw_ref[j:j + 1, :] * tap
    halo_ref[...] = x[0:SUBLANES, :] if reverse else x[tl - SUBLANES:tl, :]

    z = jnp.dot(xc.astype(BF16), wg_ref[...], preferred_element_type=F32) + bg_ref[...]
    half_c = (-0.5 * LRU_C) * jax.nn.softplus(-lam_ref[...])
    log_a = half_c * jnp.tanh(0.5 * z[:, :ch]) + half_c
    ig = 0.5 * jnp.tanh(0.5 * z[:, ch:]) + 0.5
    th = jnp.tanh(log_a)
    a_ref[...] = jnp.exp(log_a)
    m2 = -2.0 * th / (1.0 - th)
    u_ref[...] = jnp.where(m2 > 0.0, m2 * lax.rsqrt(m2), 0.0) * (ig * xc)

    ng = tl // SUBLANES
    rowid = lax.broadcasted_iota(jnp.int32, (SUBLANES, ch), 0)

    def group(gi, h):
        g = (ng - 1 - gi) if reverse else gi
        off = pl.multiple_of(g * SUBLANES, SUBLANES)
        a = a_ref[pl.ds(off, SUBLANES), :]
        u = u_ref[pl.ds(off, SUBLANES), :]
        for s in (1, 2, 4):
            if reverse:
                m = rowid < SUBLANES - s
                sh = SUBLANES - s
            else:
                m = rowid >= s
                sh = s
            ap = jnp.where(m, pltpu.roll(a, sh, 0), 1.0)
            up = jnp.where(m, pltpu.roll(u, sh, 0), 0.0)
            u = a * up + u
            a = a * ap
        hh = a * h + u
        if merge:
            u_ref[pl.ds(off, SUBLANES), :] = hh
        else:
            h_ref[0, pl.ds(off, SUBLANES), :] = hh
        edge = hh[0:1, :] if reverse else hh[SUBLANES - 1:SUBLANES, :]
        return jnp.broadcast_to(edge, (SUBLANES, ch))

    h = lax.fori_loop(0, ng, group, carry_ref[...], unroll=8)
    carry_ref[...] = h
    hl_ref[0] = h[0:1, :]
    if merge:
        h_ref[0] = ((u_ref[...] + hother_ref[0]) * glg_ref[0]).astype(BF16)


def _lru_dir(lx, h0, cw, cb, wgate, bgate, lam, reverse, tl, merge_with=None):
    bsz, n, ch = lx.shape
    nc = n // tl
    cmap = (lambda b, i: (b, nc - 1 - i, 0)) if reverse else (lambda b, i: (b, i, 0))
    full = lambda shape: pl.BlockSpec(shape, lambda b, i: (0,) * len(shape))
    merge = merge_with is not None
    extra = list(merge_with) if merge else []
    return pl.pallas_call(
        functools.partial(_lru_body, reverse=reverse, tl=tl, merge=merge),
        grid=(bsz, nc),
        in_specs=[pl.BlockSpec((1, tl, ch), cmap),
                  pl.BlockSpec((1, 1, ch), lambda b, i: (b, 0, 0)),
                  full((LRU_CONV_WIDTH, ch)), full((1, ch)),
                  full((ch, 2 * ch)), full((1, 2 * ch)), full((1, ch))]
                 + [pl.BlockSpec((1, tl, ch), cmap)] * len(extra),
        out_specs=[pl.BlockSpec((1, tl, ch), cmap),
                   pl.BlockSpec((1, 1, ch), lambda b, i: (b, 0, 0))],
        out_shape=[jax.ShapeDtypeStruct((bsz, n, ch), BF16 if merge else F32),
                   jax.ShapeDtypeStruct((bsz, 1, ch), F32)],
        scratch_shapes=[pltpu.VMEM((tl, ch), F32),
                        pltpu.VMEM((tl, ch), F32),
                        pltpu.VMEM((SUBLANES, ch), F32),
                        pltpu.VMEM((SUBLANES, ch), F32)],
        compiler_params=_params("arbitrary", "arbitrary"),
        name="lru_rev" if reverse else "lru_fwd",
    )(lx, h0, cw, cb.reshape(1, ch), wgate, bgate.reshape(1, 2 * ch), lam.reshape(1, ch), *extra)


ROW_STRIDE = GRID_W + 2 * BF16_ROWS


def _conv_row_body(x_ref, w_ref, b_ref, o_ref, pad_ref):
    n = x_ref.shape[1]
    nrows = n // GRID_W
    gap = jnp.zeros((BF16_ROWS, LANES), F32)

    def fill(r, c):
        base = pl.multiple_of(r * ROW_STRIDE, SUBLANES)
        src = pl.multiple_of(r * GRID_W, SUBLANES)
        pad_ref[pl.ds(base, BF16_ROWS), :] = gap
        pad_ref[pl.ds(base + BF16_ROWS, GRID_W), :] = x_ref[0, pl.ds(src, GRID_W), :]
        pad_ref[pl.ds(base + BF16_ROWS + GRID_W, BF16_ROWS), :] = gap
        return c

    lax.fori_loop(0, nrows, fill, 0)

    def row(r, c):
        base = pl.multiple_of(r * ROW_STRIDE, SUBLANES)
        acc = jnp.broadcast_to(b_ref[...], (GRID_W, LANES))
        for k in range(CONV_WIDTH):
            acc = acc + w_ref[k:k + 1, :] * pad_ref[pl.ds(base + BF16_ROWS - CONV_PAD + k, GRID_W), :]
        o_ref[0, pl.ds(pl.multiple_of(r * GRID_W, SUBLANES), GRID_W), :] = acc
        return c

    lax.fori_loop(0, nrows, row, 0, unroll=4)


def _conv_col_body(x_ref, w_ref, b_ref, o_ref, pad_ref):
    n = x_ref.shape[1]
    nrows = n // GRID_W
    halo = CONV_PAD * GRID_W
    pad_ref[0:halo, :] = jnp.zeros((halo, LANES), F32)
    pad_ref[halo + n:halo + n + halo, :] = jnp.zeros((halo, LANES), F32)

    def fill(r, c):
        src = pl.multiple_of(r * GRID_W, SUBLANES)
        pad_ref[pl.ds(halo + src, GRID_W), :] = x_ref[0, pl.ds(src, GRID_W), :]
        return c

    lax.fori_loop(0, nrows, fill, 0)

    def row(r, c):
        base = pl.multiple_of(r * GRID_W, SUBLANES)
        acc = jnp.broadcast_to(b_ref[...], (GRID_W, LANES))
        for k in range(CONV_WIDTH):
            acc = acc + w_ref[k:k + 1, :] * pad_ref[pl.ds(base + k * GRID_W, GRID_W), :]
        o_ref[0, pl.ds(base, GRID_W), :] = acc
        return c

    lax.fori_loop(0, nrows, row, 0, unroll=4)


def _conv_grid(glu, w, b):
    bsz, n, ch = glu.shape
    half = ch // 2
    ng = half // LANES
    outs = []
    for body, first, pad_rows in ((_conv_row_body, 0, (n // GRID_W) * ROW_STRIDE),
                                  (_conv_col_body, ng, n + 2 * CONV_PAD * GRID_W)):
        outs.append(pl.pallas_call(
            body,
            grid=(bsz, ng),
            in_specs=[pl.BlockSpec((1, n, LANES), lambda bb, g, first=first: (bb, 0, g + first)),
                      pl.BlockSpec((CONV_WIDTH, LANES), lambda bb, g, first=first: (0, g + first)),
                      pl.BlockSpec((1, LANES), lambda bb, g, first=first: (0, g + first))],
            out_specs=pl.BlockSpec((1, n, LANES), lambda bb, g: (bb, 0, g)),
            out_shape=jax.ShapeDtypeStruct((bsz, n, half), F32),
            scratch_shapes=[pltpu.VMEM((pad_rows, LANES), F32)],
            compiler_params=_params("arbitrary", "arbitrary"),
            name="conv_row" if first == 0 else "conv_col",
        )(glu, w, b.reshape(1, ch)))
    return outs


def _out_body(cr_ref, cc_ref, lng_ref, lnb_ref, yl_ref, wo_ref, bo_ref, x_ref,
              g1_ref, n2g_ref, sh2_ref, sc2_ref, rw_ref, x1_ref, v_ref, aff_ref):
    cv = jnp.concatenate([cr_ref[0], cc_ref[0]], axis=-1)
    mu = jnp.mean(cv, axis=-1, keepdims=True)
    dv = cv - mu
    var = jnp.mean(dv * dv, axis=-1, keepdims=True)
    yn = dv * lax.rsqrt(var + LN_EPS) * lng_ref[...] + lnb_ref[...]
    cy = yn * jax.nn.sigmoid(yn)
    cat = jnp.concatenate([cy.astype(BF16), yl_ref[0]], axis=-1)
    m = jnp.dot(cat, wo_ref[...], preferred_element_type=F32) + bo_ref[...]
    x1 = x_ref[0] + g1_ref[0] * m
    x1_ref[0] = x1
    ms = jnp.mean(x1 * x1, axis=-1, keepdims=True)
    v = x1 * lax.rsqrt(ms + RMS_EPS) * n2g_ref[...]
    v = v * (1.0 + sc2_ref[0]) + sh2_ref[0]
    v_ref[0] = v.astype(BF16)
    v_hi = v.astype(BF16)
    v_lo = (v - v_hi.astype(F32)).astype(BF16)
    rw = rw_ref[...]
    rw_hi = rw.astype(BF16)
    rw_lo = (rw - rw_hi.astype(F32)).astype(BF16)
    nt = (((1,), (1,)), ((), ()))
    ne = rw.shape[0]
    both = lax.dot_general(jnp.concatenate([rw_hi, rw_lo], axis=0), v_hi, nt, preferred_element_type=F32)
    lg = both[:ne] + both[ne:] + lax.dot_general(rw_hi, v_lo, nt, preferred_element_type=F32)
    ex = jnp.exp(lg - jnp.max(lg, axis=0, keepdims=True))
    aff_ref[0] = ex / jnp.sum(ex, axis=0, keepdims=True)


def _out_proj_route(conv_r, conv_c, ln_g, ln_b, yl, wo_bf16, b_out, x, g1, n2g, sh2, sc2, rw_t, tm):
    bsz, n, d = x.shape
    half = conv_r.shape[2]
    ch = yl.shape[2]
    ne = rw_t.shape[0]
    tok = lambda c: pl.BlockSpec((1, tm, c), lambda b, i: (b, i, 0))
    full = lambda shape: pl.BlockSpec(shape, lambda b, i: (0,) * len(shape))
    (g1, g1_spec), (sh2, sh2_spec), (sc2, sc2_spec) = g1, sh2, sc2
    return pl.pallas_call(
        _out_body,
        grid=(bsz, n // tm),
        in_specs=[tok(half), tok(half), full((1, 2 * half)), full((1, 2 * half)),
                  tok(ch), full((2 * half + ch, d)), full((1, d)), tok(d),
                  g1_spec, full((1, d)), sh2_spec, sc2_spec, full((ne, d))],
        out_specs=[tok(d), tok(d), pl.BlockSpec((1, ne, tm), lambda b, i: (b, 0, i))],
        out_shape=[jax.ShapeDtypeStruct((bsz, n, d), F32),
                   jax.ShapeDtypeStruct((bsz, n, d), BF16),
                   jax.ShapeDtypeStruct((bsz, ne, n), F32)],
        compiler_params=_params("arbitrary", "arbitrary"),
        name="out_proj_route",
    )(conv_r, conv_c, ln_g.reshape(1, -1), ln_b.reshape(1, -1), yl, wo_bf16,
      b_out.reshape(1, d), x, g1, n2g.reshape(1, d), sh2, sc2, rw_t)


def _token_cumsum(m, rows_per_expert):
    er = m.shape[0]
    li = lax.broadcasted_iota(jnp.int32, (LANES, LANES), 0)
    lj = lax.broadcasted_iota(jnp.int32, (LANES, LANES), 1)
    upper = jnp.where(li <= lj, 1.0, 0.0).astype(BF16)
    cs = jnp.dot(m.astype(BF16), upper, preferred_element_type=F32)
    rt = jnp.broadcast_to(cs[:, LANES - 1:LANES], (er, LANES)).astype(BF16)
    ri = lax.broadcasted_iota(jnp.int32, (er, er), 0)
    ci = lax.broadcasted_iota(jnp.int32, (er, er), 1)
    same = (ri // rows_per_expert) == (ci // rows_per_expert)
    lower = jnp.where(same & (ci < ri), 1.0, 0.0).astype(BF16)
    rstart = jnp.dot(lower, rt, preferred_element_type=F32)
    return cs, rstart


def _topk_body(aff_ref, gate_ref, pos_ref, rs_ref, cs_scr, rs_scr, *, cap):
    aff = aff_ref[0]
    ne, rows, _ = aff.shape

    def bisect(i, thr):
        cand = thr | jnp.left_shift(jnp.int32(1), 30 - i)
        cnt = jnp.sum((aff >= pltpu.bitcast(cand, F32)).astype(jnp.int32), axis=(1, 2), keepdims=True)
        return jnp.where(cnt >= cap, cand, thr)

    thr = lax.fori_loop(0, 31, bisect, jnp.zeros((ne, 1, 1), jnp.int32))
    gt = aff >= pltpu.bitcast(thr + 1, F32)
    eq = (aff >= pltpu.bitcast(thr, F32)) & jnp.logical_not(gt)
    need = (cap - jnp.sum(gt.astype(jnp.int32), axis=(1, 2), keepdims=True)).astype(F32)
    eqf = jnp.where(eq, 1.0, 0.0)
    rank_in_row, rank_row0 = _token_cumsum(eqf.reshape(ne * rows, LANES), rows)
    rank_excl = (rank_in_row + rank_row0).reshape(ne, rows, LANES) - eqf
    sel = gt | (eq & (rank_excl < need))
    self_ = jnp.where(sel, 1.0, 0.0)
    cs, rstart = _token_cumsum(self_.reshape(ne * rows, LANES), rows)
    cs3 = cs.reshape(ne, rows, LANES)
    rstart3 = rstart.reshape(ne, rows, LANES)
    cs_scr[...] = cs3
    rs_scr[...] = rstart3
    pos_ref[0] = jnp.where(sel, cs3 + rstart3 - 1.0, -1.0).astype(jnp.int32)
    rs_ref[0] = rstart3.astype(jnp.int32)

    jrow = lax.broadcasted_iota(jnp.int32, (1, cap), 1).astype(F32)
    sub_r = lax.broadcasted_iota(jnp.int32, (rows, cap), 0).astype(F32)
    sub_l = lax.broadcasted_iota(jnp.int32, (LANES, cap), 0).astype(F32)

    def per_expert(e, c):
        cl = cs_scr[e]
        af = aff_ref[0, e]
        rowtot = cl[:, LANES - 1:LANES]
        before = rowtot + rs_scr[e][:, 0:1] <= jrow
        rj = jnp.sum(jnp.where(before, 1.0, 0.0), axis=0, keepdims=True)
        rowbase = jnp.sum(jnp.where(before, rowtot, 0.0), axis=0, keepdims=True)
        onehot = jnp.where(sub_r == rj, 1.0, 0.0).astype(BF16)
        a1 = af.astype(BF16)
        r1 = af - a1.astype(F32)
        a2 = r1.astype(BF16)
        a3 = (r1 - a2.astype(F32)).astype(BF16)
        lhs = jnp.concatenate([cl.astype(BF16), a1, a2, a3], axis=1)
        gathered = lax.dot_general(lhs, onehot, (((0,), (0,)), ((), ())), preferred_element_type=F32)
        g = gathered[:LANES]
        ga = gathered[LANES:2 * LANES] + gathered[2 * LANES:3 * LANES] + gathered[3 * LANES:]
        lanepos = jnp.sum(jnp.where(g <= jrow - rowbase, 1.0, 0.0), axis=0, keepdims=True)
        gate = jnp.sum(jnp.where(sub_l == lanepos, ga, 0.0), axis=0, keepdims=True)
        gate_ref[0, pl.ds(e, 1), :] = gate
        return c

    lax.fori_loop(0, ne, per_expert, 0)


def _topk_route(aff4, cap):
    bsz, ne, rows, _ = aff4.shape
    blk4 = pl.BlockSpec((1, ne, rows, LANES), lambda b: (b, 0, 0, 0))
    lst = pl.BlockSpec((1, ne, cap), lambda b: (b, 0, 0))
    return pl.pallas_call(
        functools.partial(_topk_body, cap=cap),
        grid=(bsz,),
        in_specs=[blk4],
        out_specs=[lst, blk4, blk4],
        out_shape=[jax.ShapeDtypeStruct((bsz, ne, cap), F32),
                   jax.ShapeDtypeStruct((bsz, ne, rows, LANES), jnp.int32),
                   jax.ShapeDtypeStruct((bsz, ne, rows, LANES), jnp.int32)],
        scratch_shapes=[pltpu.VMEM((ne, rows, LANES), F32), pltpu.VMEM((ne, rows, LANES), F32)],
        compiler_params=_params("arbitrary"),
        name="topk_route",
    )(aff4)


PACK_TOK = 256
PACK_SLAB = 64
PACK_SHIFT = 6
PACK_WAYS = 2


def _pack_body(rs_ref, pos_ref, v_ref, xs_hbm, stage, stage_x, pend, sems, sem_x, *, cap):
    bp = pl.program_id(0)
    s = pl.program_id(1)
    q = pl.program_id(2)
    ns = pl.num_programs(1)
    b = bp * PACK_WAYS + q
    ne = pos_ref.shape[2]
    tpr = PACK_TOK // LANES
    nr = ns * tpr
    step = (bp * ns + s) * PACK_WAYS + q
    nsteps = pl.num_programs(0) * ns * PACK_WAYS
    slot = step % (PACK_WAYS + 1)
    slab = PACK_SLAB
    pq = q * ne

    @pl.when(s == 0)
    def _():
        for e in range(ne):
            pend[pq + e] = jnp.zeros(pend.shape[1:], BF16)

    def first_pos(rr, e):
        return rs_ref[(b * ne + e) * (nr + 1) + rr]

    def floor16(p):
        return lax.shift_left(lax.shift_right_logical(p, 4), 4)

    p0 = [first_pos(s * tpr, e) for e in range(ne)]
    p1 = [first_pos(s * tpr + tpr, e) for e in range(ne)]
    a0 = [floor16(p0[e]) for e in range(ne)]
    a1 = [floor16(p1[e]) for e in range(ne)]
    has = [a1[e] < p1[e] for e in range(ne)]
    prnd = [lax.shift_right_logical(a1[e] - a0[e], PACK_SHIFT) for e in range(ne)]
    poff = [pl.multiple_of(jnp.bitwise_and(a1[e] - a0[e], slab - 1), BF16_ROWS) for e in range(ne)]

    vb = v_ref[0]
    pos = [jnp.concatenate([pos_ref[0, j, e:e + 1, :] for j in range(tpr)], axis=1) for e in range(ne)]
    kiota = lax.broadcasted_iota(jnp.int32, (slab, PACK_TOK), 0)

    def onehot(rnd):
        blocks = [jnp.where(kiota == pos[e] - (a0[e] + rnd * slab), 1.0, 0.0) for e in range(ne)]
        return jnp.concatenate(blocks, axis=0).astype(BF16)

    def keep_partial(buf, e, rnd, old):
        grp = buf[pl.ds(e * slab + poff[e], BF16_ROWS), :]
        return jnp.where(jnp.logical_and(has[e], prnd[e] == rnd), grp, old)

    cur = stage.at[slot]
    cur[...] = jnp.dot(onehot(0), vb, preferred_element_type=F32).astype(BF16)
    for e in range(ne):
        cur[e * slab:e * slab + BF16_ROWS, :] += pend[pq + e]
    for e in range(ne):
        pend[pq + e] = keep_partial(cur, e, 0, jnp.zeros((BF16_ROWS, vb.shape[1]), BF16))

    def slab_bytes_wait(buf, sem):
        pltpu.make_async_copy(buf, buf, sem).wait()

    @pl.when(step >= PACK_WAYS)
    def _():
        old = (step + 1) % (PACK_WAYS + 1)
        slab_bytes_wait(stage.at[old], sems.at[old])

    @pl.when(s == 0)
    def _():
        stage_x[0:slab, :] = jnp.zeros((slab, stage_x.shape[1]), BF16)
        for e in range(ne):
            pltpu.make_async_copy(stage_x.at[0:slab], xs_hbm.at[e, b, pl.ds(cap, slab)], sem_x).start()
        slab_bytes_wait(stage_x, sem_x)

    for e in range(ne):
        pltpu.make_async_copy(stage.at[slot, e * slab:(e + 1) * slab],
                              xs_hbm.at[e, b, pl.ds(pl.multiple_of(a0[e], BF16_ROWS), slab)], sems.at[slot]).start()

    span = p1[0] - a0[0]
    for e in range(1, ne):
        span = jnp.maximum(span, p1[e] - a0[e])
    rounds = jnp.maximum(1, lax.shift_right_logical(span + (slab - 1), PACK_SHIFT))

    def extra(rnd, carry):
        stage_x[...] = jnp.dot(onehot(rnd), vb, preferred_element_type=F32).astype(BF16)
        for e in range(ne):
            pend[pq + e] = keep_partial(stage_x, e, rnd, pend[pq + e])
        for e in range(ne):
            @pl.when(p1[e] - a0[e] > rnd * slab)
            def _(e=e):
                cp = pltpu.make_async_copy(
                    stage_x.at[e * slab:(e + 1) * slab],
                    xs_hbm.at[e, b, pl.ds(pl.multiple_of(a0[e] + rnd * slab, BF16_ROWS), slab)], sem_x)
                cp.start()
                cp.wait()
        return carry

    lax.fori_loop(1, rounds, extra, 0)

    @pl.when(step == nsteps - 1)
    def _():
        for back in range(PACK_WAYS):
            last = (step + (PACK_WAYS + 1) - back) % (PACK_WAYS + 1)
            slab_bytes_wait(stage.at[last], sems.at[last])


def _pack_tokens(rs_ext, pos_r, v, cap):
    bsz, n, d = v.shape
    assert bsz % PACK_WAYS == 0
    ne = pos_r.shape[2]
    tpr = PACK_TOK // LANES
    return pl.pallas_call(
        functools.partial(_pack_body, cap=cap),
        grid_spec=pltpu.PrefetchScalarGridSpec(
            num_scalar_prefetch=1,
            grid=(bsz // PACK_WAYS, n // PACK_TOK, PACK_WAYS),
            in_specs=[pl.BlockSpec((1, tpr, ne, LANES), lambda bp, s, q, rs: (bp * PACK_WAYS + q, s, 0, 0)),
                      pl.BlockSpec((1, PACK_TOK, d), lambda bp, s, q, rs: (bp * PACK_WAYS + q, s, 0))],
            out_specs=pl.BlockSpec(memory_space=pl.ANY),
            scratch_shapes=[pltpu.VMEM((PACK_WAYS + 1, ne * PACK_SLAB, d), BF16),
                            pltpu.VMEM((ne * PACK_SLAB, d), BF16),
                            pltpu.VMEM((PACK_WAYS * ne, BF16_ROWS, d), BF16),
                            pltpu.SemaphoreType.DMA((PACK_WAYS + 1,)),
                            pltpu.SemaphoreType.DMA(())]),
        out_shape=jax.ShapeDtypeStruct((ne, bsz, cap + PACK_SLAB, d), BF16),
        compiler_params=_params("arbitrary", "arbitrary", "arbitrary"),
        name="pack_tokens",
    )(rs_ext, pos_r, v)


def _ffn_body(x_ref, gate_ref, wg_ref, wu_ref, wd_ref, ys_hbm, acc, stage, osems, *, mg, nf, sub, och):
    e = pl.program_id(0)
    f = pl.program_id(1)
    spg = x_ref.shape[2]

    def chunk_step(first):
        wgb = wg_ref[0].astype(BF16)
        wub = wu_ref[0].astype(BF16)
        wdb = wd_ref[0].astype(BF16)
        for i in range(mg // sub):
            rows = slice(i * sub, (i + 1) * sub)
            x = x_ref[0, (i * sub) // spg, (i * sub) % spg:(i * sub) % spg + sub, :]
            gg = jnp.dot(x, wgb, preferred_element_type=F32)
            uu = jnp.dot(x, wub, preferred_element_type=F32)
            h = (gg * jax.nn.sigmoid(gg) * uu).astype(BF16)
            y = jnp.dot(h, wdb, preferred_element_type=F32)
            if first:
                acc[rows, :] = y
            else:
                acc[rows, :] += y

    pl.when(f == 0)(functools.partial(chunk_step, True))
    pl.when(f != 0)(functools.partial(chunk_step, False))

    @pl.when(f == nf - 1)
    def _():
        gate_t = gate_ref[0].T

        def out_copy(c):
            return pltpu.make_async_copy(stage.at[c % 2], ys_hbm.at[e, pl.ds(c * och, och)], osems.at[c % 2])

        nchunks = mg // och
        for c in range(nchunks):
            if c >= 2:
                out_copy(c - 2).wait()
            for i in range(och // LANES):
                blk = c * (och // LANES) + i
                rows = slice(blk * LANES, (blk + 1) * LANES)
                stage[c % 2, i * LANES:(i + 1) * LANES, :] = (acc[rows, :] * gate_t[:, blk:blk + 1]).astype(BF16)
            out_copy(c).start()
        for c in range(max(nchunks - 2, 0), nchunks):
            out_copy(c).wait()


def _expert_ffn(xs, gate_rows, wg, wu, wd, tf):
    ne, d, fdim = wg.shape
    bsz, cap = xs.shape[1], gate_rows.shape[1] * LANES // xs.shape[1]
    mg = bsz * cap
    nf = fdim // tf
    sub = cap
    och = min(mg, 512)
    return pl.pallas_call(
        functools.partial(_ffn_body, mg=mg, nf=nf, sub=sub, och=och),
        grid_spec=pltpu.PrefetchScalarGridSpec(
            num_scalar_prefetch=0,
            grid=(ne, nf),
            in_specs=[pl.BlockSpec((1, bsz, cap, d), lambda e, f: (e, 0, 0, 0)),
                      pl.BlockSpec((1, mg // LANES, LANES), lambda e, f: (e, 0, 0)),
                      pl.BlockSpec((1, d, tf), lambda e, f: (e, 0, f)),
                      pl.BlockSpec((1, d, tf), lambda e, f: (e, 0, f)),
                      pl.BlockSpec((1, tf, d), lambda e, f: (e, f, 0))],
            out_specs=pl.BlockSpec(memory_space=pl.ANY),
            scratch_shapes=[pltpu.VMEM((mg, d), F32),
                            pltpu.VMEM((2, och, d), BF16),
                            pltpu.SemaphoreType.DMA((2,))]),
        out_shape=jax.ShapeDtypeStruct((ne, mg, d), BF16),
        compiler_params=_params("arbitrary", "arbitrary"),
        name="expert_ffn",
    )(xs, gate_rows, wg, wu, wd)


SLAB = PACK_SLAB
TILE_ROWS = PACK_TOK // LANES


def _combine_body(rs_ref, pos_ref, ys_hbm, x1_ref, g5_ref, fng_ref, o_ref, stk, stk_x, sems, sem_x, *, cap, tps):
    b = pl.program_id(0)
    s = pl.program_id(1)
    nb = pl.num_programs(0)
    ns = pl.num_programs(1)
    ne = pos_ref.shape[2]
    nr = ns * tps * TILE_ROWS
    tok = TILE_ROWS * LANES
    step = b * ns + s
    slot = step % 2
    tile_rows = ne * SLAB

    def first_pos(bb, rr, e):
        return rs_ref[(bb * ne + e) * (nr + 1) + rr]

    def aligned(p0):
        return lax.shift_left(lax.shift_right_logical(p0, 4), 4)

    def slab_start(a0, rnd):
        return pl.multiple_of(jnp.minimum(a0 + rnd * SLAB, cap - SLAB), BF16_ROWS)

    def slab_copy(bb, e, a, dst, row, sem):
        return pltpu.make_async_copy(ys_hbm.at[e, pl.ds(bb * cap + a, SLAB)], dst.at[pl.ds(row, SLAB)], sem)

    def issue(bb, ss, sl):
        for j in range(tps):
            for e in range(ne):
                a = slab_start(aligned(first_pos(bb, (ss * tps + j) * TILE_ROWS, e)), 0)
                slab_copy(bb, e, a, stk.at[sl], j * tile_rows + e * SLAB, sems.at[sl]).start()

    @pl.when(step == 0)
    def _():
        issue(b, s, slot)

    @pl.when(step + 1 < nb * ns)
    def _():
        nxt = step + 1
        issue(nxt // ns, nxt % ns, 1 - slot)

    kiota = lax.broadcasted_iota(jnp.int32, (SLAB, tok), 0)
    tn = (((0,), (0,)), ((), ()))
    pltpu.make_async_copy(stk.at[slot], stk.at[slot], sems.at[slot]).wait()

    for j in range(tps):
        r = (s * tps + j) * TILE_ROWS
        pos = jnp.concatenate([pos_ref[0, j * TILE_ROWS + q] for q in range(TILE_ROWS)], axis=1)
        a0 = [aligned(first_pos(b, r, e)) for e in range(ne)]

        def onehot(rnd, pos=pos, a0=a0):
            blocks = []
            for e in range(ne):
                rel = pos[e:e + 1, :] - slab_start(a0[e], rnd)
                fresh = (pos[e:e + 1, :] - a0[e]) >= rnd * SLAB
                blocks.append(jnp.where((kiota == rel) & fresh, 1.0, 0.0))
            return jnp.concatenate(blocks, axis=0).astype(BF16)

        moe = lax.dot_general(onehot(0), stk[slot, j * tile_rows:(j + 1) * tile_rows, :], tn,
                              preferred_element_type=F32)

        span = first_pos(b, r + TILE_ROWS, 0) - a0[0]
        for e in range(1, ne):
            span = jnp.maximum(span, first_pos(b, r + TILE_ROWS, e) - a0[e])
        rounds = jnp.maximum(1, lax.shift_right_logical(span + (SLAB - 1), PACK_SHIFT))

        def extra(rnd, m, a0=a0, onehot=onehot):
            for e in range(ne):
                slab_copy(b, e, slab_start(a0[e], rnd), stk_x, e * SLAB, sem_x).start()
            pltpu.make_async_copy(stk_x, stk_x, sem_x).wait()
            return m + lax.dot_general(onehot(rnd), stk_x[...], tn, preferred_element_type=F32)

        moe = lax.fori_loop(1, rounds, extra, moe)

        y = x1_ref[0, j * tok:(j + 1) * tok, :] + g5_ref[0] * moe
        ms = jnp.mean(y * y, axis=-1, keepdims=True)
        o_ref[0, j * tok:(j + 1) * tok, :] = y * lax.rsqrt(ms + RMS_EPS) * fng_ref[...]


def _combine_norm(rs_ext, pos_r, ys, x1, g5, fng, cap, tps):
    bsz, n, d = x1.shape
    g5, g5_spec = g5
    ns = n // (PACK_TOK * tps)
    ne = pos_r.shape[2]
    return pl.pallas_call(
        functools.partial(_combine_body, cap=cap, tps=tps),
        grid_spec=pltpu.PrefetchScalarGridSpec(
            num_scalar_prefetch=1,
            grid=(bsz, ns),
            in_specs=[pl.BlockSpec((1, tps * TILE_ROWS, ne, LANES), lambda b, s, rs: (b, s, 0, 0)),
                      pl.BlockSpec(memory_space=pl.ANY),
                      pl.BlockSpec((1, tps * PACK_TOK, d), lambda b, s, rs: (b, s, 0)),
                      g5_spec,
                      pl.BlockSpec((1, d), lambda b, s, rs: (0, 0))],
            out_specs=pl.BlockSpec((1, tps * PACK_TOK, d), lambda b, s, rs: (b, s, 0)),
            scratch_shapes=[pltpu.VMEM((2, tps * ne * SLAB, d), BF16),
                            pltpu.VMEM((ne * SLAB, d), BF16),
                            pltpu.SemaphoreType.DMA((2,)),
                            pltpu.SemaphoreType.DMA(())]),
        out_shape=jax.ShapeDtypeStruct((bsz, n, d), F32),
        compiler_params=_params("arbitrary", "arbitrary"),
        name="combine_norm",
    )(rs_ext, pos_r, ys, x1, g5, fng.reshape(1, d))


def _block_diag(w):
    heads, hd, _ = w.shape
    eye = jnp.eye(heads, dtype=w.dtype)
    return (eye[:, None, :, None] * w[:, :, None, :]).reshape(heads * hd, heads * hd)


IN_PROJ_ROWS = 1024
OUT_PROJ_ROWS = 512
LRU_CHUNK = 256
FFN_COLS = 256
COMBINE_TILES = 2


def _tile(n, pref):
    return pref if n % pref == 0 else n


def kernel(x, c, ctx, c_ctx, norm1_g, norm2_g, ada_w, ada_b, w_in, b_in, conv_dw_w, conv_dw_b, conv_ln_g, conv_ln_b, lru_conv_w, lru_conv_b, lru_wa, lru_ba, lru_wi, lru_bi, lru_lambda, w_out, b_out, router_w, exp_w_gate, exp_w_up, exp_w_down, final_norm_g):
    assert norm1_g.shape[0] == 1
    mod, x1, v, aff = _mixer(x, c, ctx, c_ctx, norm1_g[0], norm2_g[0], ada_w[0], ada_b[0], w_in[0], b_in[0],
                             conv_dw_w[0], conv_dw_b[0], conv_ln_g[0], conv_ln_b[0], lru_conv_w[0],
                             lru_conv_b[0], lru_wa[0], lru_ba[0], lru_wi[0], lru_bi[0], lru_lambda[0],
                             w_out[0], b_out[0], router_w[0])
    return _moe_norm(x1, v, aff, mod[5], exp_w_gate[0], exp_w_up[0], exp_w_down[0], final_norm_g)


def _mixer(x, c, ctx, c_ctx, norm1_g, norm2_g, ada_w, ada_b, w_in, b_in, conv_w, conv_b, ln_g, ln_b,
           lru_cw, lru_cb, lru_wa, lru_ba, lru_wi, lru_bi, lru_lam, w_out, b_out, router_w):
    bsz, n, d = x.shape
    cond8 = jnp.zeros((SUBLANES, d), F32).at[:bsz].set(c).at[bsz].set(c_ctx)
    mods = _ada_mod(cond8, ada_w, ada_b)
    mods3 = mods.reshape(SUBLANES * N_MOD, 1, d)
    mod = [_mod_row(mods3, k) for k in range(N_MOD)]
    mod_c = [_mod_row(mods3, k, fixed_row=bsz) for k in range(2)]
    w_in_b = w_in.astype(BF16)
    w_out_b = w_out.astype(BF16)
    ch = lru_cb.shape[1]
    wgate = [jnp.concatenate([_block_diag(lru_wa[dd]), _block_diag(lru_wi[dd])], axis=1).astype(BF16)
             for dd in range(2)]
    bgate = [jnp.concatenate([lru_ba[dd], lru_bi[dd]]) for dd in range(2)]

    def lru(lx, h0, dd, reverse, merge_with=None):
        return _lru_dir(lx, h0, lru_cw[dd], lru_cb[dd], wgate[dd], bgate[dd], lru_lam[dd], reverse,
                        _tile(lx.shape[1], LRU_CHUNK), merge_with)

    _, c_lx, _ = _in_proj(ctx, norm1_g, mod_c[0], mod_c[1], w_in_b, b_in, _tile(ctx.shape[1], IN_PROJ_ROWS))
    zero_h = jnp.zeros((bsz, 1, ch), F32)
    _, hf0 = lru(c_lx, zero_h, 0, False)
    _, hb0 = lru(c_lx, zero_h, 1, True)

    x_glu, x_lx, x_glg = _in_proj(x, norm1_g, mod[0], mod[1], w_in_b, b_in, _tile(n, IN_PROJ_ROWS))
    hb, _ = lru(x_lx, hb0, 1, True)
    yl, _ = lru(x_lx, hf0, 0, False, merge_with=(hb, x_glg))
    conv_r, conv_c = _conv_grid(x_glu, conv_w, conv_b)
    x1, v, aff = _out_proj_route(conv_r, conv_c, ln_g, ln_b, yl, w_out_b, b_out, x, mod[2],
                                 norm2_g, mod[3], mod[4], router_w.T, _tile(n, OUT_PROJ_ROWS))
    return mod, x1, v, aff


def _moe_norm(x1, v, aff, gate2, wg, wu, wd, final_norm_g):
    bsz, n, d = x1.shape
    ne = aff.shape[1]
    cap = EC_CAPACITY * n // ne
    gate, pos, rstart = _topk_route(aff.reshape(bsz, ne, n // LANES, LANES), cap)
    gate_rows = jnp.swapaxes(gate, 0, 1).reshape(ne, bsz * cap // LANES, LANES)
    rs_ext = jnp.concatenate([rstart[..., 0], jnp.full((bsz, ne, 1), cap, jnp.int32)], axis=-1).reshape(-1)
    pos_r = jnp.swapaxes(pos, 1, 2)
    xs = _pack_tokens(rs_ext, pos_r, v, cap)
    ys = _expert_ffn(xs, gate_rows, wg, wu, wd, FFN_COLS)
    return _combine_norm(rs_ext, pos_r, ys, x1, gate2, final_norm_g, cap, COMBINE_TILES)
```

```python
import functools

import jax
import jax.numpy as jnp
from jax import lax
from jax.experimental import pallas as pl
from jax.experimental.pallas import tpu as pltpu

GRID_W = 64
CONV_WIDTH = 31
CONV_PAD = (CONV_WIDTH - 1) // 2
LRU_CONV_WIDTH = 4
LRU_C = 8.0
N_EXPERTS = 16
EC_CAPACITY = 2
N_MOD = 6
RMS_EPS = 1e-6
LN_EPS = 1e-5

LANES = 128
SUBLANES = 8
BF16_ROWS = 16
VMEM_LIMIT = 56 * 1024 * 1024

F32 = jnp.float32
BF16 = jnp.bfloat16


def _params(*sem):
    return pltpu.CompilerParams(dimension_semantics=sem, vmem_limit_bytes=VMEM_LIMIT)


def _mod_row(mods3, k, fixed_row=None):
    d = mods3.shape[2]

    def index_map(b, *_):
        return ((b if fixed_row is None else fixed_row) * N_MOD + k, 0, 0)

    return mods3, pl.BlockSpec((1, 1, d), index_map)


def _ada_body(c_ref, w_ref, b_ref, o_ref):
    s = c_ref[...]
    s = s * jax.nn.sigmoid(s)
    w = w_ref[...]
    s_hi = s.astype(BF16)
    s_lo = (s - s_hi.astype(F32)).astype(BF16)
    w_hi = w.astype(BF16)
    w_lo = (w - w_hi.astype(F32)).astype(BF16)
    rows = s.shape[0]
    both = jnp.dot(jnp.concatenate([s_hi, s_lo], axis=0), w_hi, preferred_element_type=F32)
    o_ref[...] = both[:rows] + both[rows:] + jnp.dot(s_hi, w_lo, preferred_element_type=F32) + b_ref[...]


def _ada_mod(cond8, ada_w, ada_b):
    d, n = ada_w.shape
    tn = n // 4
    return pl.pallas_call(
        _ada_body,
        grid=(n // tn,),
        in_specs=[pl.BlockSpec((SUBLANES, d), lambda j: (0, 0)),
                  pl.BlockSpec((d, tn), lambda j: (0, j)),
                  pl.BlockSpec((1, tn), lambda j: (0, j))],
        out_specs=pl.BlockSpec((SUBLANES, tn), lambda j: (0, j)),
        out_shape=jax.ShapeDtypeStruct((SUBLANES, n), F32),
        compiler_params=_params("arbitrary"),
        name="ada_mod",
    )(cond8, ada_w, ada_b.reshape(1, n))


def _inproj_body(x_ref, g_ref, sh_ref, sc_ref, w_ref, b_ref, glu_ref, lx_ref, glg_ref, *, rb):
    gain = g_ref[...] * (1.0 + sc_ref[0])
    for i in range(x_ref.shape[1] // rb):
        rows = slice(i * rb, (i + 1) * rb)
        x = x_ref[0, rows, :]
        ms = jnp.mean(x * x, axis=-1, keepdims=True)
        u = x * lax.rsqrt(ms + RMS_EPS) * gain + sh_ref[0]
        p = jnp.dot(u.astype(BF16), w_ref[...], preferred_element_type=F32) + b_ref[...]
        cc = p.shape[1] // 4
        glu_ref[0, rows, :] = p[:, :cc] * jax.nn.sigmoid(p[:, cc:2 * cc])
        lx_ref[0, rows, :] = p[:, 2 * cc:3 * cc]
        glg_ref[0, rows, :] = jax.nn.gelu(p[:, 3 * cc:])


def _in_proj(x, norm_g, shift, scale, w_bf16, b_in, tm):
    bsz, n, d = x.shape
    (shift, shift_spec), (scale, scale_spec) = shift, scale
    n4 = w_bf16.shape[1]
    cc = n4 // 4
    tok = pl.BlockSpec((1, tm, cc), lambda b, i: (b, i, 0))
    return pl.pallas_call(
        functools.partial(_inproj_body, rb=min(tm, 256)),
        grid=(bsz, n // tm),
        in_specs=[pl.BlockSpec((1, tm, d), lambda b, i: (b, i, 0)),
                  pl.BlockSpec((1, d), lambda b, i: (0, 0)),
                  shift_spec, scale_spec,
                  pl.BlockSpec((d, n4), lambda b, i: (0, 0)),
                  pl.BlockSpec((1, n4), lambda b, i: (0, 0))],
        out_specs=[tok, tok, tok],
        out_shape=[jax.ShapeDtypeStruct((bsz, n, cc), F32)] * 3,
        compiler_params=_params("arbitrary", "arbitrary"),
        name="in_proj",
    )(x, norm_g.reshape(1, d), shift, scale, w_bf16, b_in.reshape(1, n4))


def _lru_body(*refs, reverse, tl, merge):
    x_ref, h0_ref, cw_ref, cb_ref, wg_ref, bg_ref, lam_ref = refs[:7]
    if merge:
        hother_ref, glg_ref = refs[7:9]
        refs = refs[2:]
    h_ref, hl_ref, a_ref, u_ref, carry_ref, halo_ref = refs[7:]
    i = pl.program_id(1)
    ch = x_ref.shape[2]

    @pl.when(i == 0)
    def _():
        carry_ref[...] = jnp.broadcast_to(h0_ref[0], carry_ref.shape)
        halo_ref[...] = jnp.zeros(halo_ref.shape, F32)

    x = x_ref[0]
    halo = halo_ref[...]
    row8 = lax.broadcasted_iota(jnp.int32, (SUBLANES, ch), 0)
    xc = cb_ref[...] + cw_ref[LRU_CONV_WIDTH - 1:LRU_CONV_WIDTH, :] * x
    for j in range(LRU_CONV_WIDTH - 1):
        s = LRU_CONV_WIDTH - 1 - j
        if not reverse:
            rolled = pltpu.roll(x, s, 0)
            edge = jnp.where(row8 < s, pltpu.roll(halo, s, 0), rolled[0:SUBLANES, :])
            tap = jnp.concatenate([edge, rolled[SUBLANES:, :]], axis=0)
        else:
            rolled = pltpu.roll(x, tl - s, 0)
            edge = jnp.where(row8 >= SUBLANES - s, pltpu.roll(halo, SUBLANES - s, 0), rolled[tl - SUBLANES:, :])
            tap = jnp.concatenate([rolled[:tl - SUBLANES, :], edge], axis=0)
        xc = xc + cw_ref[j:j + 1, :] * tap
    halo_ref[...] = x[0:SUBLANES, :] if reverse else x[tl - SUBLANES:tl, :]

    z = jnp.dot(xc.astype(BF16), wg_ref[...], preferred_element_type=F32) + bg_ref[...]
    half_c = (-0.5 * LRU_C) * jax.nn.softplus(-lam_ref[...])
    log_a = half_c * jnp.tanh(0.5 * z[:, :ch]) + half_c
    ig = 0.5 * jnp.tanh(0.5 * z[:, ch:]) + 0.5
    th = jnp.tanh(log_a)
    a_ref[...] = jnp.exp(log_a)
    m2 = -2.0 * th / (1.0 - th)
    u_ref[...] = jnp.where(m2 > 0.0, m2 * lax.rsqrt(m2), 0.0) * (ig * xc)

    ng = tl // SUBLANES
    rowid = lax.broadcasted_iota(jnp.int32, (SUBLANES, ch), 0)

    def group(gi, h):
        g = (ng - 1 - gi) if reverse else gi
        off = pl.multiple_of(g * SUBLANES, SUBLANES)
        a = a_ref[pl.ds(off, SUBLANES), :]
        u = u_ref[pl.ds(off, SUBLANES), :]
        for s in (1, 2, 4):
            if reverse:
                m = rowid < SUBLANES - s
                sh = SUBLANES - s
            else:
                m = rowid >= s
                sh = s
            ap = jnp.where(m, pltpu.roll(a, sh, 0), 1.0)
            up = jnp.where(m, pltpu.roll(u, sh, 0), 0.0)
            u = a * up + u
            a = a * ap
        hh = a * h + u
        if merge:
            u_ref[pl.ds(off, SUBLANES), :] = hh
        else:
            h_ref[0, pl.ds(off, SUBLANES), :] = hh
        edge = hh[0:1, :] if reverse else hh[SUBLANES - 1:SUBLANES, :]
        return jnp.broadcast_to(edge, (SUBLANES, ch))

    h = lax.fori_loop(0, ng, group, carry_ref[...], unroll=8)
    carry_ref[...] = h
    hl_ref[0] = h[0:1, :]
    if merge:
        h_ref[0] = ((u_ref[...] + hother_ref[0]) * glg_ref[0]).astype(BF16)


def _lru_dir(lx, h0, cw, cb, wgate, bgate, lam, reverse, tl, merge_with=None):
    bsz, n, ch = lx.shape
    nc = n // tl
    cmap = (lambda b, i: (b, nc - 1 - i, 0)) if reverse else (lambda b, i: (b, i, 0))
    full = lambda shape: pl.BlockSpec(shape, lambda b, i: (0,) * len(shape))
    merge = merge_with is not None
    extra = list(merge_with) if merge else []
    return pl.pallas_call(
        functools.partial(_lru_body, reverse=reverse, tl=tl, merge=merge),
        grid=(bsz, nc),
        in_specs=[pl.BlockSpec((1, tl, ch), cmap),
                  pl.BlockSpec((1, 1, ch), lambda b, i: (b, 0, 0)),
                  full((LRU_CONV_WIDTH, ch)), full((1, ch)),
                  full((ch, 2 * ch)), full((1, 2 * ch)), full((1, ch))]
                 + [pl.BlockSpec((1, tl, ch), cmap)] * len(extra),
        out_specs=[pl.BlockSpec((1, tl, ch), cmap),
                   pl.BlockSpec((1, 1, ch), lambda b, i: (b, 0, 0))],
        out_shape=[jax.ShapeDtypeStruct((bsz, n, ch), BF16 if merge else F32),
                   jax.ShapeDtypeStruct((bsz, 1, ch), F32)],
        scratch_shapes=[pltpu.VMEM((tl, ch), F32),
                        pltpu.VMEM((tl, ch), F32),
                        pltpu.VMEM((SUBLANES, ch), F32),
                        pltpu.VMEM((SUBLANES, ch), F32)],
        compiler_params=_params("arbitrary", "arbitrary"),
        name="lru_rev" if reverse else "lru_fwd",
    )(lx, h0, cw, cb.reshape(1, ch), wgate, bgate.reshape(1, 2 * ch), lam.reshape(1, ch), *extra)


ROW_STRIDE = GRID_W + 2 * BF16_ROWS


def _conv_row_body(x_ref, w_ref, b_ref, o_ref, pad_ref):
    n = x_ref.shape[1]
    nrows = n // GRID_W
    gap = jnp.zeros((BF16_ROWS, LANES), F32)

    def fill(r, c):
        base = pl.multiple_of(r * ROW_STRIDE, SUBLANES)
        src = pl.multiple_of(r * GRID_W, SUBLANES)
        pad_ref[pl.ds(base, BF16_ROWS), :] = gap
        pad_ref[pl.ds(base + BF16_ROWS, GRID_W), :] = x_ref[0, pl.ds(src, GRID_W), :]
        pad_ref[pl.ds(base + BF16_ROWS + GRID_W, BF16_ROWS), :] = gap
        return c

    lax.fori_loop(0, nrows, fill, 0)

    def row(r, c):
        base = pl.multiple_of(r * ROW_STRIDE, SUBLANES)
        acc = jnp.broadcast_to(b_ref[...], (GRID_W, LANES))
        for k in range(CONV_WIDTH):
            acc = acc + w_ref[k:k + 1, :] * pad_ref[pl.ds(base + BF16_ROWS - CONV_PAD + k, GRID_W), :]
        o_ref[0, pl.ds(pl.multiple_of(r * GRID_W, SUBLANES), GRID_W), :] = acc
        return c

    lax.fori_loop(0, nrows, row, 0, unroll=4)


def _conv_col_body(x_ref, w_ref, b_ref, o_ref, pad_ref):
    n = x_ref.shape[1]
    nrows = n // GRID_W
    halo = CONV_PAD * GRID_W
    pad_ref[0:halo, :] = jnp.zeros((halo, LANES), F32)
    pad_ref[halo + n:halo + n + halo, :] = jnp.zeros((halo, LANES), F32)

    def fill(r, c):
        src = pl.multiple_of(r * GRID_W, SUBLANES)
        pad_ref[pl.ds(halo + src, GRID_W), :] = x_ref[0, pl.ds(src, GRID_W), :]
        return c

    lax.fori_loop(0, nrows, fill, 0)

    def row(r, c):
        base = pl.multiple_of(r * GRID_W, SUBLANES)
        acc = jnp.broadcast_to(b_ref[...], (GRID_W, LANES))
        for k in range(CONV_WIDTH):
            acc = acc + w_ref[k:k + 1, :] * pad_ref[pl.ds(base + k * GRID_W, GRID_W), :]
        o_ref[0, pl.ds(base, GRID_W), :] = acc
        return c

    lax.fori_loop(0, nrows, row, 0, unroll=4)


def _conv_grid(glu, w, b):
    bsz, n, ch = glu.shape
    half = ch // 2
    ng = half // LANES
    outs = []
    for body, first, pad_rows in ((_conv_row_body, 0, (n // GRID_W) * ROW_STRIDE),
                                  (_conv_col_body, ng, n + 2 * CONV_PAD * GRID_W)):
        outs.append(pl.pallas_call(
            body,
            grid=(bsz, ng),
            in_specs=[pl.BlockSpec((1, n, LANES), lambda bb, g, first=first: (bb, 0, g + first)),
                      pl.BlockSpec((CONV_WIDTH, LANES), lambda bb, g, first=first: (0, g + first)),
                      pl.BlockSpec((1, LANES), lambda bb, g, first=first: (0, g + first))],
            out_specs=pl.BlockSpec((1, n, LANES), lambda bb, g: (bb, 0, g)),
            out_shape=jax.ShapeDtypeStruct((bsz, n, half), F32),
            scratch_shapes=[pltpu.VMEM((pad_rows, LANES), F32)],
            compiler_params=_params("arbitrary", "arbitrary"),
            name="conv_row" if first == 0 else "conv_col",
        )(glu, w, b.reshape(1, ch)))
    return outs


def _out_body(cr_ref, cc_ref, lng_ref, lnb_ref, yl_ref, wo_ref, bo_ref, x_ref,
              g1_ref, n2g_ref, sh2_ref, sc2_ref, rw_ref, x1_ref, v_ref, aff_ref):
    cv = jnp.concatenate([cr_ref[0], cc_ref[0]], axis=-1)
    mu = jnp.mean(cv, axis=-1, keepdims=True)
    dv = cv - mu
    var = jnp.mean(dv * dv, axis=-1, keepdims=True)
    yn = dv * lax.rsqrt(var + LN_EPS) * lng_ref[...] + lnb_ref[...]
    cy = yn * jax.nn.sigmoid(yn)
    cat = jnp.concatenate([cy.astype(BF16), yl_ref[0]], axis=-1)
    m = jnp.dot(cat, wo_ref[...], preferred_element_type=F32) + bo_ref[...]
    x1 = x_ref[0] + g1_ref[0] * m
    x1_ref[0] = x1
    ms = jnp.mean(x1 * x1, axis=-1, keepdims=True)
    v = x1 * lax.rsqrt(ms + RMS_EPS) * n2g_ref[...]
    v = v * (1.0 + sc2_ref[0]) + sh2_ref[0]
    v_ref[0] = v.astype(BF16)
    v_hi = v.astype(BF16)
    v_lo = (v - v_hi.astype(F32)).astype(BF16)
    rw = rw_ref[...]
    rw_hi = rw.astype(BF16)
    rw_lo = (rw - rw_hi.astype(F32)).astype(BF16)
    nt = (((1,), (1,)), ((), ()))
    ne = rw.shape[0]
    both = lax.dot_general(jnp.concatenate([rw_hi, rw_lo], axis=0), v_hi, nt, preferred_element_type=F32)
    lg = both[:ne] + both[ne:] + lax.dot_general(rw_hi, v_lo, nt, preferred_element_type=F32)
    ex = jnp.exp(lg - jnp.max(lg, axis=0, keepdims=True))
    aff_ref[0] = ex / jnp.sum(ex, axis=0, keepdims=True)


def _out_proj_route(conv_r, conv_c, ln_g, ln_b, yl, wo_bf16, b_out, x, g1, n2g, sh2, sc2, rw_t, tm):
    bsz, n, d = x.shape
    half = conv_r.shape[2]
    ch = yl.shape[2]
    ne = rw_t.shape[0]
    tok = lambda c: pl.BlockSpec((1, tm, c), lambda b, i: (b, i, 0))
    full = lambda shape: pl.BlockSpec(shape, lambda b, i: (0,) * len(shape))
    (g1, g1_spec), (sh2, sh2_spec), (sc2, sc2_spec) = g1, sh2, sc2
    return pl.pallas_call(
        _out_body,
        grid=(bsz, n // tm),
        in_specs=[tok(half), tok(half), full((1, 2 * half)), full((1, 2 * half)),
                  tok(ch), full((2 * half + ch, d)), full((1, d)), tok(d),
                  g1_spec, full((1, d)), sh2_spec, sc2_spec, full((ne, d))],
        out_specs=[tok(d), tok(d), pl.BlockSpec((1, ne, tm), lambda b, i: (b, 0, i))],
        out_shape=[jax.ShapeDtypeStruct((bsz, n, d), F32),
                   jax.ShapeDtypeStruct((bsz, n, d), BF16),
                   jax.ShapeDtypeStruct((bsz, ne, n), F32)],
        compiler_params=_params("arbitrary", "arbitrary"),
        name="out_proj_route",
    )(conv_r, conv_c, ln_g.reshape(1, -1), ln_b.reshape(1, -1), yl, wo_bf16,
      b_out.reshape(1, d), x, g1, n2g.reshape(1, d), sh2, sc2, rw_t)


def _token_cumsum(m, rows_per_expert):
    er = m.shape[0]
    li = lax.broadcasted_iota(jnp.int32, (LANES, LANES), 0)
    lj = lax.broadcasted_iota(jnp.int32, (LANES, LANES), 1)
    upper = jnp.where(li <= lj, 1.0, 0.0).astype(BF16)
    cs = jnp.dot(m.astype(BF16), upper, preferred_element_type=F32)
    rt = jnp.broadcast_to(cs[:, LANES - 1:LANES], (er, LANES)).astype(BF16)
    ri = lax.broadcasted_iota(jnp.int32, (er, er), 0)
    ci = lax.broadcasted_iota(jnp.int32, (er, er), 1)
    same = (ri // rows_per_expert) == (ci // rows_per_expert)
    lower = jnp.where(same & (ci < ri), 1.0, 0.0).astype(BF16)
    rstart = jnp.dot(lower, rt, preferred_element_type=F32)
    return cs, rstart


def _topk_body(aff_ref, gate_ref, pos_ref, rs_ref, cs_scr, rs_scr, *, cap):
    aff = aff_ref[0]
    ne, rows, _ = aff.shape

    def bisect(i, thr):
        cand = thr | jnp.left_shift(jnp.int32(1), 30 - i)
        cnt = jnp.sum((aff >= pltpu.bitcast(cand, F32)).astype(jnp.int32), axis=(1, 2), keepdims=True)
        return jnp.where(cnt >= cap, cand, thr)

    thr = lax.fori_loop(0, 31, bisect, jnp.zeros((ne, 1, 1), jnp.int32))
    gt = aff >= pltpu.bitcast(thr + 1, F32)
    eq = (aff >= pltpu.bitcast(thr, F32)) & jnp.logical_not(gt)
    need = (cap - jnp.sum(gt.astype(jnp.int32), axis=(1, 2), keepdims=True)).astype(F32)
    eqf = jnp.where(eq, 1.0, 0.0)
    rank_in_row, rank_row0 = _token_cumsum(eqf.reshape(ne * rows, LANES), rows)
    rank_excl = (rank_in_row + rank_row0).reshape(ne, rows, LANES) - eqf
    sel = gt | (eq & (rank_excl < need))
    self_ = jnp.where(sel, 1.0, 0.0)
    cs, rstart = _token_cumsum(self_.reshape(ne * rows, LANES), rows)
    cs3 = cs.reshape(ne, rows, LANES)
    rstart3 = rstart.reshape(ne, rows, LANES)
    cs_scr[...] = cs3
    rs_scr[...] = rstart3
    pos_ref[0] = jnp.where(sel, cs3 + rstart3 - 1.0, -1.0).astype(jnp.int32)
    rs_ref[0] = rstart3.astype(jnp.int32)

    jrow = lax.broadcasted_iota(jnp.int32, (1, cap), 1).astype(F32)
    sub_r = lax.broadcasted_iota(jnp.int32, (rows, cap), 0).astype(F32)
    sub_l = lax.broadcasted_iota(jnp.int32, (LANES, cap), 0).astype(F32)

    def per_expert(e, c):
        cl = cs_scr[e]
        af = aff_ref[0, e]
        rowtot = cl[:, LANES - 1:LANES]
        before = rowtot + rs_scr[e][:, 0:1] <= jrow
        rj = jnp.sum(jnp.where(before, 1.0, 0.0), axis=0, keepdims=True)
        rowbase = jnp.sum(jnp.where(before, rowtot, 0.0), axis=0, keepdims=True)
        onehot = jnp.where(sub_r == rj, 1.0, 0.0).astype(BF16)
        a1 = af.astype(BF16)
        r1 = af - a1.astype(F32)
        a2 = r1.astype(BF16)
        a3 = (r1 - a2.astype(F32)).astype(BF16)
        lhs = jnp.concatenate([cl.astype(BF16), a1, a2, a3], axis=1)
        gathered = lax.dot_general(lhs, onehot, (((0,), (0,)), ((), ())), preferred_element_type=F32)
        g = gathered[:LANES]
        ga = gathered[LANES:2 * LANES] + gathered[2 * LANES:3 * LANES] + gathered[3 * LANES:]
        lanepos = jnp.sum(jnp.where(g <= jrow - rowbase, 1.0, 0.0), axis=0, keepdims=True)
        gate = jnp.sum(jnp.where(sub_l == lanepos, ga, 0.0), axis=0, keepdims=True)
        gate_ref[0, pl.ds(e, 1), :] = gate
        return c

    lax.fori_loop(0, ne, per_expert, 0)


def _topk_route(aff4, cap):
    bsz, ne, rows, _ = aff4.shape
    blk4 = pl.BlockSpec((1, ne, rows, LANES), lambda b: (b, 0, 0, 0))
    lst = pl.BlockSpec((1, ne, cap), lambda b: (b, 0, 0))
    return pl.pallas_call(
        functools.partial(_topk_body, cap=cap),
        grid=(bsz,),
        in_specs=[blk4],
        out_specs=[lst, blk4, blk4],
        out_shape=[jax.ShapeDtypeStruct((bsz, ne, cap), F32),
                   jax.ShapeDtypeStruct((bsz, ne, rows, LANES), jnp.int32),
                   jax.ShapeDtypeStruct((bsz, ne, rows, LANES), jnp.int32)],
        scratch_shapes=[pltpu.VMEM((ne, rows, LANES), F32), pltpu.VMEM((ne, rows, LANES), F32)],
        compiler_params=_params("arbitrary"),
        name="topk_route",
    )(aff4)


PACK_TOK = 256
PACK_SLAB = 64
PACK_SHIFT = 6
PACK_WAYS = 2


def _pack_body(rs_ref, pos_ref, v_ref, xs_hbm, stage, stage_x, pend, sems, sem_x, *, cap):
    bp = pl.program_id(0)
    s = pl.program_id(1)
    q = pl.program_id(2)
    ns = pl.num_programs(1)
    b = bp * PACK_WAYS + q
    ne = pos_ref.shape[2]
    tpr = PACK_TOK // LANES
    nr = ns * tpr
    step = (bp * ns + s) * PACK_WAYS + q
    nsteps = pl.num_programs(0) * ns * PACK_WAYS
    slot = step % (PACK_WAYS + 1)
    slab = PACK_SLAB
    pq = q * ne

    @pl.when(s == 0)
    def _():
        for e in range(ne):
            pend[pq + e] = jnp.zeros(pend.shape[1:], BF16)

    def first_pos(rr, e):
        return rs_ref[(b * ne + e) * (nr + 1) + rr]

    def floor16(p):
        return lax.shift_left(lax.shift_right_logical(p, 4), 4)

    p0 = [first_pos(s * tpr, e) for e in range(ne)]
    p1 = [first_pos(s * tpr + tpr, e) for e in range(ne)]
    a0 = [floor16(p0[e]) for e in range(ne)]
    a1 = [floor16(p1[e]) for e in range(ne)]
    has = [a1[e] < p1[e] for e in range(ne)]
    prnd = [lax.shift_right_logical(a1[e] - a0[e], PACK_SHIFT) for e in range(ne)]
    poff = [pl.multiple_of(jnp.bitwise_and(a1[e] - a0[e], slab - 1), BF16_ROWS) for e in range(ne)]

    vb = v_ref[0]
    pos = [jnp.concatenate([pos_ref[0, j, e:e + 1, :] for j in range(tpr)], axis=1) for e in range(ne)]
    kiota = lax.broadcasted_iota(jnp.int32, (slab, PACK_TOK), 0)

    def onehot(rnd):
        blocks = [jnp.where(kiota == pos[e] - (a0[e] + rnd * slab), 1.0, 0.0) for e in range(ne)]
        return jnp.concatenate(blocks, axis=0).astype(BF16)

    def keep_partial(buf, e, rnd, old):
        grp = buf[pl.ds(e * slab + poff[e], BF16_ROWS), :]
        return jnp.where(jnp.logical_and(has[e], prnd[e] == rnd), grp, old)

    cur = stage.at[slot]
    cur[...] = jnp.dot(onehot(0), vb, preferred_element_type=F32).astype(BF16)
    for e in range(ne):
        cur[e * slab:e * slab + BF16_ROWS, :] += pend[pq + e]
    for e in range(ne):
        pend[pq + e] = keep_partial(cur, e, 0, jnp.zeros((BF16_ROWS, vb.shape[1]), BF16))

    def slab_bytes_wait(buf, sem):
        pltpu.make_async_copy(buf, buf, sem).wait()

    @pl.when(step >= PACK_WAYS)
    def _():
        old = (step + 1) % (PACK_WAYS + 1)
        slab_bytes_wait(stage.at[old], sems.at[old])

    @pl.when(s == 0)
    def _():
        stage_x[0:slab, :] = jnp.zeros((slab, stage_x.shape[1]), BF16)
        for e in range(ne):
            pltpu.make_async_copy(stage_x.at[0:slab], xs_hbm.at[e, b, pl.ds(cap, slab)], sem_x).start()
        slab_bytes_wait(stage_x, sem_x)

    for e in range(ne):
        pltpu.make_async_copy(stage.at[slot, e * slab:(e + 1) * slab],
                              xs_hbm.at[e, b, pl.ds(pl.multiple_of(a0[e], BF16_ROWS), slab)], sems.at[slot]).start()

    span = p1[0] - a0[0]
    for e in range(1, ne):
        span = jnp.maximum(span, p1[e] - a0[e])
    rounds = jnp.maximum(1, lax.shift_right_logical(span + (slab - 1), PACK_SHIFT))

    def extra(rnd, carry):
        stage_x[...] = jnp.dot(onehot(rnd), vb, preferred_element_type=F32).astype(BF16)
        for e in range(ne):
            pend[pq + e] = keep_partial(stage_x, e, rnd, pend[pq + e])
        for e in range(ne):
            @pl.when(p1[e] - a0[e] > rnd * slab)
            def _(e=e):
                cp = pltpu.make_async_copy(
                    stage_x.at[e * slab:(e + 1) * slab],
                    xs_hbm.at[e, b, pl.ds(pl.multiple_of(a0[e] + rnd * slab, BF16_ROWS), slab)], sem_x)
                cp.start()
                cp.wait()
        return carry

    lax.fori_loop(1, rounds, extra, 0)

    @pl.when(step == nsteps - 1)
    def _():
        for back in range(PACK_WAYS):
            last = (step + (PACK_WAYS + 1) - back) % (PACK_WAYS + 1)
            slab_bytes_wait(stage.at[last], sems.at[last])


def _pack_tokens(rs_ext, pos_r, v, cap):
    bsz, n, d = v.shape
    assert bsz % PACK_WAYS == 0
    ne = pos_r.shape[2]
    tpr = PACK_TOK // LANES
    return pl.pallas_call(
        functools.partial(_pack_body, cap=cap),
        grid_spec=pltpu.PrefetchScalarGridSpec(
            num_scalar_prefetch=1,
            grid=(bsz // PACK_WAYS, n // PACK_TOK, PACK_WAYS),
            in_specs=[pl.BlockSpec((1, tpr, ne, LANES), lambda bp, s, q, rs: (bp * PACK_WAYS + q, s, 0, 0)),
                      pl.BlockSpec((1, PACK_TOK, d), lambda bp, s, q, rs: (bp * PACK_WAYS + q, s, 0))],
            out_specs=pl.BlockSpec(memory_space=pl.ANY),
            scratch_shapes=[pltpu.VMEM((PACK_WAYS + 1, ne * PACK_SLAB, d), BF16),
                            pltpu.VMEM((ne * PACK_SLAB, d), BF16),
                            pltpu.VMEM((PACK_WAYS * ne, BF16_ROWS, d), BF16),
                            pltpu.SemaphoreType.DMA((PACK_WAYS + 1,)),
                            pltpu.SemaphoreType.DMA(())]),
        out_shape=jax.ShapeDtypeStruct((ne, bsz, cap + PACK_SLAB, d), BF16),
        compiler_params=_params("arbitrary", "arbitrary", "arbitrary"),
        name="pack_tokens",
    )(rs_ext, pos_r, v)


def _ffn_body(x_ref, gate_ref, wg_ref, wu_ref, wd_ref, ys_hbm, acc, stage, osems, *, mg, nf, sub, och):
    e = pl.program_id(0)
    f = pl.program_id(1)
    spg = x_ref.shape[2]

    def chunk_step(first):
        wgb = wg_ref[0].astype(BF16)
        wub = wu_ref[0].astype(BF16)
        wdb = wd_ref[0].astype(BF16)
        for i in range(mg // sub):
            rows = slice(i * sub, (i + 1) * sub)
            x = x_ref[0, (i * sub) // spg, (i * sub) % spg:(i * sub) % spg + sub, :]
            gg = jnp.dot(x, wgb, preferred_element_type=F32)
            uu = jnp.dot(x, wub, preferred_element_type=F32)
            h = (gg * jax.nn.sigmoid(gg) * uu).astype(BF16)
            y = jnp.dot(h, wdb, preferred_element_type=F32)
            if first:
                acc[rows, :] = y
            else:
                acc[rows, :] += y

    pl.when(f == 0)(functools.partial(chunk_step, True))
    pl.when(f != 0)(functools.partial(chunk_step, False))

    @pl.when(f == nf - 1)
    def _():
        gate_t = gate_ref[0].T

        def out_copy(c):
            return pltpu.make_async_copy(stage.at[c % 2], ys_hbm.at[e, pl.ds(c * och, och)], osems.at[c % 2])

        nchunks = mg // och
        for c in range(nchunks):
            if c >= 2:
                out_copy(c - 2).wait()
            for i in range(och // LANES):
                blk = c * (och // LANES) + i
                rows = slice(blk * LANES, (blk + 1) * LANES)
                stage[c % 2, i * LANES:(i + 1) * LANES, :] = (acc[rows, :] * gate_t[:, blk:blk + 1]).astype(BF16)
            out_copy(c).start()
        for c in range(max(nchunks - 2, 0), nchunks):
            out_copy(c).wait()


def _expert_ffn(xs, gate_rows, wg, wu, wd, tf):
    ne, d, fdim = wg.shape
    bsz, cap = xs.shape[1], gate_rows.shape[1] * LANES // xs.shape[1]
    mg = bsz * cap
    nf = fdim // tf
    sub = cap
    och = min(mg, 512)
    return pl.pallas_call(
        functools.partial(_ffn_body, mg=mg, nf=nf, sub=sub, och=och),
        grid_spec=pltpu.PrefetchScalarGridSpec(
            num_scalar_prefetch=0,
            grid=(ne, nf),
            in_specs=[pl.BlockSpec((1, bsz, cap, d), lambda e, f: (e, 0, 0, 0)),
                      pl.BlockSpec((1, mg // LANES, LANES), lambda e, f: (e, 0, 0)),
                      pl.BlockSpec((1, d, tf), lambda e, f: (e, 0, f)),
                      pl.BlockSpec((1, d, tf), lambda e, f: (e, 0, f)),
                      pl.BlockSpec((1, tf, d), lambda e, f: (e, f, 0))],
            out_specs=pl.BlockSpec(memory_space=pl.ANY),
            scratch_shapes=[pltpu.VMEM((mg, d), F32),
                            pltpu.VMEM((2, och, d), BF16),
                            pltpu.SemaphoreType.DMA((2,))]),
        out_shape=jax.ShapeDtypeStruct((ne, mg, d), BF16),
        compiler_params=_params("arbitrary", "arbitrary"),
        name="expert_ffn",
    )(xs, gate_rows, wg, wu, wd)


SLAB = PACK_SLAB
TILE_ROWS = PACK_TOK // LANES


def _combine_body(rs_ref, pos_ref, ys_hbm, x1_ref, g5_ref, fng_ref, o_ref, stk, stk_x, sems, sem_x, *, cap, tps):
    b = pl.program_id(0)
    s = pl.program_id(1)
    nb = pl.num_programs(0)
    ns = pl.num_programs(1)
    ne = pos_ref.shape[2]
    nr = ns * tps * TILE_ROWS
    tok = TILE_ROWS * LANES
    step = b * ns + s
    slot = step % 2
    tile_rows = ne * SLAB

    def first_pos(bb, rr, e):
        return rs_ref[(bb * ne + e) * (nr + 1) + rr]

    def aligned(p0):
        return lax.shift_left(lax.shift_right_logical(p0, 4), 4)

    def slab_start(a0, rnd):
        return pl.multiple_of(jnp.minimum(a0 + rnd * SLAB, cap - SLAB), BF16_ROWS)

    def slab_copy(bb, e, a, dst, row, sem):
        return pltpu.make_async_copy(ys_hbm.at[e, pl.ds(bb * cap + a, SLAB)], dst.at[pl.ds(row, SLAB)], sem)

    def issue(bb, ss, sl):
        for j in range(tps):
            for e in range(ne):
                a = slab_start(aligned(first_pos(bb, (ss * tps + j) * TILE_ROWS, e)), 0)
                slab_copy(bb, e, a, stk.at[sl], j * tile_rows + e * SLAB, sems.at[sl]).start()

    @pl.when(step == 0)
    def _():
        issue(b, s, slot)

    @pl.when(step + 1 < nb * ns)
    def _():
        nxt = step + 1
        issue(nxt // ns, nxt % ns, 1 - slot)

    kiota = lax.broadcasted_iota(jnp.int32, (SLAB, tok), 0)
    tn = (((0,), (0,)), ((), ()))
    pltpu.make_async_copy(stk.at[slot], stk.at[slot], sems.at[slot]).wait()

    for j in range(tps):
        r = (s * tps + j) * TILE_ROWS
        pos = jnp.concatenate([pos_ref[0, j * TILE_ROWS + q] for q in range(TILE_ROWS)], axis=1)
        a0 = [aligned(first_pos(b, r, e)) for e in range(ne)]

        def onehot(rnd, pos=pos, a0=a0):
            blocks = []
            for e in range(ne):
                rel = pos[e:e + 1, :] - slab_start(a0[e], rnd)
                fresh = (pos[e:e + 1, :] - a0[e]) >= rnd * SLAB
                blocks.append(jnp.where((kiota == rel) & fresh, 1.0, 0.0))
            return jnp.concatenate(blocks, axis=0).astype(BF16)

        moe = lax.dot_general(onehot(0), stk[slot, j * tile_rows:(j + 1) * tile_rows, :], tn,
                              preferred_element_type=F32)

        span = first_pos(b, r + TILE_ROWS, 0) - a0[0]
        for e in range(1, ne):
            span = jnp.maximum(span, first_pos(b, r + TILE_ROWS, e) - a0[e])
        rounds = jnp.maximum(1, lax.shift_right_logical(span + (SLAB - 1), PACK_SHIFT))

        def extra(rnd, m, a0=a0, onehot=onehot):
            for e in range(ne):
                slab_copy(b, e, slab_start(a0[e], rnd), stk_x, e * SLAB, sem_x).start()
            pltpu.make_async_copy(stk_x, stk_x, sem_x).wait()
            return m + lax.dot_general(onehot(rnd), stk_x[...], tn, preferred_element_type=F32)

        moe = lax.fori_loop(1, rounds, extra, moe)

        y = x1_ref[0, j * tok:(j + 1) * tok, :] + g5_ref[0] * moe
        ms = jnp.mean(y * y, axis=-1, keepdims=True)
        o_ref[0, j * tok:(j + 1) * tok, :] = y * lax.rsqrt(ms + RMS_EPS) * fng_ref[...]


def _combine_norm(rs_ext, pos_r, ys, x1, g5, fng, cap, tps):
    bsz, n, d = x1.shape
    g5, g5_spec = g5
    ns = n // (PACK_TOK * tps)
    ne = pos_r.shape[2]
    return pl.pallas_call(
        functools.partial(_combine_body, cap=cap, tps=tps),
        grid_spec=pltpu.PrefetchScalarGridSpec(
            num_scalar_prefetch=1,
            grid=(bsz, ns),
            in_specs=[pl.BlockSpec((1, tps * TILE_ROWS, ne, LANES), lambda b, s, rs: (b, s, 0, 0)),
                      pl.BlockSpec(memory_space=pl.ANY),
                      pl.BlockSpec((1, tps * PACK_TOK, d), lambda b, s, rs: (b, s, 0)),
                      g5_spec,
                      pl.BlockSpec((1, d), lambda b, s, rs: (0, 0))],
            out_specs=pl.BlockSpec((1, tps * PACK_TOK, d), lambda b, s, rs: (b, s, 0)),
            scratch_shapes=[pltpu.VMEM((2, tps * ne * SLAB, d), BF16),
                            pltpu.VMEM((ne * SLAB, d), BF16),
                            pltpu.SemaphoreType.DMA((2,)),
                            pltpu.SemaphoreType.DMA(())]),
        out_shape=jax.ShapeDtypeStruct((bsz, n, d), F32),
        compiler_params=_params("arbitrary", "arbitrary"),
        name="combine_norm",
    )(rs_ext, pos_r, ys, x1, g5, fng.reshape(1, d))


def _block_diag(w):
    heads, hd, _ = w.shape
    eye = jnp.eye(heads, dtype=w.dtype)
    return (eye[:, None, :, None] * w[:, :, None, :]).reshape(heads * hd, heads * hd)


IN_PROJ_ROWS = 1024
OUT_PROJ_ROWS = 1024
LRU_CHUNK = 512
FFN_COLS = 256
COMBINE_TILES = 2


def _tile(n, pref):
    return pref if n % pref == 0 else n


def kernel(x, c, ctx, c_ctx, norm1_g, norm2_g, ada_w, ada_b, w_in, b_in, conv_dw_w, conv_dw_b, conv_ln_g, conv_ln_b, lru_conv_w, lru_conv_b, lru_wa, lru_ba, lru_wi, lru_bi, lru_lambda, w_out, b_out, router_w, exp_w_gate, exp_w_up, exp_w_down, final_norm_g):
    assert norm1_g.shape[0] == 1
    mod, x1, v, aff = _mixer(x, c, ctx, c_ctx, norm1_g[0], norm2_g[0], ada_w[0], ada_b[0], w_in[0], b_in[0],
                             conv_dw_w[0], conv_dw_b[0], conv_ln_g[0], conv_ln_b[0], lru_conv_w[0],
                             lru_conv_b[0], lru_wa[0], lru_ba[0], lru_wi[0], lru_bi[0], lru_lambda[0],
                             w_out[0], b_out[0], router_w[0])
    return _moe_norm(x1, v, aff, mod[5], exp_w_gate[0], exp_w_up[0], exp_w_down[0], final_norm_g)


def _mixer(x, c, ctx, c_ctx, norm1_g, norm2_g, ada_w, ada_b, w_in, b_in, conv_w, conv_b, ln_g, ln_b,
           lru_cw, lru_cb, lru_wa, lru_ba, lru_wi, lru_bi, lru_lam, w_out, b_out, router_w):
    bsz, n, d = x.shape
    cond8 = jnp.zeros((SUBLANES, d), F32).at[:bsz].set(c).at[bsz].set(c_ctx)
    mods = _ada_mod(cond8, ada_w, ada_b)
    mods3 = mods.reshape(SUBLANES * N_MOD, 1, d)
    mod = [_mod_row(mods3, k) for k in range(N_MOD)]
    mod_c = [_mod_row(mods3, k, fixed_row=bsz) for k in range(2)]
    w_in_b = w_in.astype(BF16)
    w_out_b = w_out.astype(BF16)
    ch = lru_cb.shape[1]
    wgate = [jnp.concatenate([_block_diag(lru_wa[dd]), _block_diag(lru_wi[dd])], axis=1).astype(BF16)
             for dd in range(2)]
    bgate = [jnp.concatenate([lru_ba[dd], lru_bi[dd]]) for dd in range(2)]

    def lru(lx, h0, dd, reverse, merge_with=None):
        return _lru_dir(lx, h0, lru_cw[dd], lru_cb[dd], wgate[dd], bgate[dd], lru_lam[dd], reverse,
                        _tile(lx.shape[1], LRU_CHUNK), merge_with)

    _, c_lx, _ = _in_proj(ctx, norm1_g, mod_c[0], mod_c[1], w_in_b, b_in, _tile(ctx.shape[1], IN_PROJ_ROWS))
    zero_h = jnp.zeros((bsz, 1, ch), F32)
    _, hf0 = lru(c_lx, zero_h, 0, False)
    _, hb0 = lru(c_lx, zero_h, 1, True)

    x_glu, x_lx, x_glg = _in_proj(x, norm1_g, mod[0], mod[1], w_in_b, b_in, _tile(n, IN_PROJ_ROWS))
    hb, _ = lru(x_lx, hb0, 1, True)
    yl, _ = lru(x_lx, hf0, 0, False, merge_with=(hb, x_glg))
    conv_r, conv_c = _conv_grid(x_glu, conv_w, conv_b)
    x1, v, aff = _out_proj_route(conv_r, conv_c, ln_g, ln_b, yl, w_out_b, b_out, x, mod[2],
                                 norm2_g, mod[3], mod[4], router_w.T, _tile(n, OUT_PROJ_ROWS))
    return mod, x1, v, aff


def _moe_norm(x1, v, aff, gate2, wg, wu, wd, final_norm_g):
    bsz, n, d = x1.shape
    ne = aff.shape[1]
    cap = EC_CAPACITY * n // ne
    gate, pos, rstart = _topk_route(aff.reshape(bsz, ne, n // LANES, LANES), cap)
    gate_rows = jnp.swapaxes(gate, 0, 1).reshape(ne, bsz * cap // LANES, LANES)
    rs_ext = jnp.concatenate([rstart[..., 0], jnp.full((bsz, ne, 1), cap, jnp.int32)], axis=-1).reshape(-1)
    pos_r = jnp.swapaxes(pos, 1, 2)
    xs = _pack_tokens(rs_ext, pos_r, v, cap)
    ys = _expert_ffn(xs, gate_rows, wg, wu, wd, FFN_COLS)
    return _combine_norm(rs_ext, pos_r, ys, x1, gate2, final_norm_g, cap, COMBINE_TILES)
```

```python
import functools

import jax
import jax.numpy as jnp
from jax import lax
from jax.experimental import pallas as pl
from jax.experimental.pallas import tpu as pltpu

GRID_W = 64
CONV_WIDTH = 31
CONV_PAD = (CONV_WIDTH - 1) // 2
LRU_CONV_WIDTH = 4
LRU_C = 8.0
N_EXPERTS = 16
EC_CAPACITY = 2
N_MOD = 6
RMS_EPS = 1e-6
LN_EPS = 1e-5

LANES = 128
SUBLANES = 8
BF16_ROWS = 16
VMEM_LIMIT = 56 * 1024 * 1024

F32 = jnp.float32
BF16 = jnp.bfloat16


def _params(*sem):
    return pltpu.CompilerParams(dimension_semantics=sem, vmem_limit_bytes=VMEM_LIMIT)


def _mod_row(mods3, k, fixed_row=None):
    d = mods3.shape[2]

    def index_map(b, *_):
        return ((b if fixed_row is None else fixed_row) * N_MOD + k, 0, 0)

    return mods3, pl.BlockSpec((1, 1, d), index_map)


def _ada_body(c_ref, w_ref, b_ref, o_ref):
    s = c_ref[...]
    s = s * jax.nn.sigmoid(s)
    w = w_ref[...]
    s_hi = s.astype(BF16)
    s_lo = (s - s_hi.astype(F32)).astype(BF16)
    w_hi = w.astype(BF16)
    w_lo = (w - w_hi.astype(F32)).astype(BF16)
    rows = s.shape[0]
    both = jnp.dot(jnp.concatenate([s_hi, s_lo], axis=0), w_hi, preferred_element_type=F32)
    o_ref[...] = both[:rows] + both[rows:] + jnp.dot(s_hi, w_lo, preferred_element_type=F32) + b_ref[...]


def _ada_mod(cond8, ada_w, ada_b):
    d, n = ada_w.shape
    tn = n // 4
    return pl.pallas_call(
        _ada_body,
        grid=(n // tn,),
        in_specs=[pl.BlockSpec((SUBLANES, d), lambda j: (0, 0)),
                  pl.BlockSpec((d, tn), lambda j: (0, j)),
                  pl.BlockSpec((1, tn), lambda j: (0, j))],
        out_specs=pl.BlockSpec((SUBLANES, tn), lambda j: (0, j)),
        out_shape=jax.ShapeDtypeStruct((SUBLANES, n), F32),
        compiler_params=_params("arbitrary"),
        name="ada_mod",
    )(cond8, ada_w, ada_b.reshape(1, n))


def _inproj_body(x_ref, g_ref, sh_ref, sc_ref, w_ref, b_ref, glu_ref, lx_ref, glg_ref, *, rb):
    gain = g_ref[...] * (1.0 + sc_ref[0])
    for i in range(x_ref.shape[1] // rb):
        rows = slice(i * rb, (i + 1) * rb)
        x = x_ref[0, rows, :]
        ms = jnp.mean(x * x, axis=-1, keepdims=True)
        u = x * lax.rsqrt(ms + RMS_EPS) * gain + sh_ref[0]
        p = jnp.dot(u.astype(BF16), w_ref[...], preferred_element_type=F32) + b_ref[...]
        cc = p.shape[1] // 4
        glu_ref[0, rows, :] = p[:, :cc] * jax.nn.sigmoid(p[:, cc:2 * cc])
        lx_ref[0, rows, :] = p[:, 2 * cc:3 * cc]
        glg_ref[0, rows, :] = jax.nn.gelu(p[:, 3 * cc:])


def _in_proj(x, norm_g, shift, scale, w_bf16, b_in, tm):
    bsz, n, d = x.shape
    (shift, shift_spec), (scale, scale_spec) = shift, scale
    n4 = w_bf16.shape[1]
    cc = n4 // 4
    tok = pl.BlockSpec((1, tm, cc), lambda b, i: (b, i, 0))
    return pl.pallas_call(
        functools.partial(_inproj_body, rb=min(tm, 256)),
        grid=(bsz, n // tm),
        in_specs=[pl.BlockSpec((1, tm, d), lambda b, i: (b, i, 0)),
                  pl.BlockSpec((1, d), lambda b, i: (0, 0)),
                  shift_spec, scale_spec,
                  pl.BlockSpec((d, n4), lambda b, i: (0, 0)),
                  pl.BlockSpec((1, n4), lambda b, i: (0, 0))],
        out_specs=[tok, tok, tok],
        out_shape=[jax.ShapeDtypeStruct((bsz, n, cc), F32)] * 3,
        compiler_params=_params("arbitrary", "arbitrary"),
        name="in_proj",
    )(x, norm_g.reshape(1, d), shift, scale, w_bf16, b_in.reshape(1, n4))


def _lru_body(*refs, reverse, tl, merge):
    x_ref, h0_ref, cw_ref, cb_ref, wg_ref, bg_ref, lam_ref = refs[:7]
    if merge:
        hother_ref, glg_ref = refs[7:9]
        refs = refs[2:]
    h_ref, hl_ref, a_ref, u_ref, carry_ref, halo_ref = refs[7:]
    i = pl.program_id(1)
    ch = x_ref.shape[2]

    @pl.when(i == 0)
    def _():
        carry_ref[...] = jnp.broadcast_to(h0_ref[0], carry_ref.shape)
        halo_ref[...] = jnp.zeros(halo_ref.shape, F32)

    x = x_ref[0]
    halo = halo_ref[...]
    row8 = lax.broadcasted_iota(jnp.int32, (SUBLANES, ch), 0)
    xc = cb_ref[...] + cw_ref[LRU_CONV_WIDTH - 1:LRU_CONV_WIDTH, :] * x
    for j in range(LRU_CONV_WIDTH - 1):
        s = LRU_CONV_WIDTH - 1 - j
        if not reverse:
            rolled = pltpu.roll(x, s, 0)
            edge = jnp.where(row8 < s, pltpu.roll(halo, s, 0), rolled[0:SUBLANES, :])
            tap = jnp.concatenate([edge, rolled[SUBLANES:, :]], axis=0)
        else:
            rolled = pltpu.roll(x, tl - s, 0)
            edge = jnp.where(row8 >= SUBLANES - s, pltpu.roll(halo, SUBLANES - s, 0), rolled[tl - SUBLANES:, :])
            tap = jnp.concatenate([rolled[:tl - SUBLANES, :], edge], axis=0)
        xc = xc + cw_ref[j:j + 1, :] * tap
    halo_ref[...] = x[0:SUBLANES, :] if reverse else x[tl - SUBLANES:tl, :]

    z = jnp.dot(xc.astype(BF16), wg_ref[...], preferred_element_type=F32) + bg_ref[...]
    half_c = (-0.5 * LRU_C) * jax.nn.softplus(-lam_ref[...])
    log_a = half_c * jnp.tanh(0.5 * z[:, :ch]) + half_c
    ig = 0.5 * jnp.tanh(0.5 * z[:, ch:]) + 0.5
    th = jnp.tanh(log_a)
    a_ref[...] = jnp.exp(log_a)
    m2 = -2.0 * th / (1.0 - th)
    u_ref[...] = jnp.where(m2 > 0.0, m2 * lax.rsqrt(m2), 0.0) * (ig * xc)

    ng = tl // SUBLANES
    rowid = lax.broadcasted_iota(jnp.int32, (SUBLANES, ch), 0)

    def group(gi, h):
        g = (ng - 1 - gi) if reverse else gi
        off = pl.multiple_of(g * SUBLANES, SUBLANES)
        a = a_ref[pl.ds(off, SUBLANES), :]
        u = u_ref[pl.ds(off, SUBLANES), :]
        for s in (1, 2, 4):
            if reverse:
                m = rowid < SUBLANES - s
                sh = SUBLANES - s
            else:
                m = rowid >= s
                sh = s
            ap = jnp.where(m, pltpu.roll(a, sh, 0), 1.0)
            up = jnp.where(m, pltpu.roll(u, sh, 0), 0.0)
            u = a * up + u
            a = a * ap
        hh = a * h + u
        if merge:
            u_ref[pl.ds(off, SUBLANES), :] = hh
        else:
            h_ref[0, pl.ds(off, SUBLANES), :] = hh
        edge = hh[0:1, :] if reverse else hh[SUBLANES - 1:SUBLANES, :]
        return jnp.broadcast_to(edge, (SUBLANES, ch))

    h = lax.fori_loop(0, ng, group, carry_ref[...], unroll=8)
    carry_ref[...] = h
    hl_ref[0] = h[0:1, :]
    if merge:
        h_ref[0] = ((u_ref[...] + hother_ref[0]) * glg_ref[0]).astype(BF16)


def _lru_dir(lx, h0, cw, cb, wgate, bgate, lam, reverse, tl, merge_with=None):
    bsz, n, ch = lx.shape
    nc = n // tl
    cmap = (lambda b, i: (b, nc - 1 - i, 0)) if reverse else (lambda b, i: (b, i, 0))
    full = lambda shape: pl.BlockSpec(shape, lambda b, i: (0,) * len(shape))
    merge = merge_with is not None
    extra = list(merge_with) if merge else []
    return pl.pallas_call(
        functools.partial(_lru_body, reverse=reverse, tl=tl, merge=merge),
        grid=(bsz, nc),
        in_specs=[pl.BlockSpec((1, tl, ch), cmap),
                  pl.BlockSpec((1, 1, ch), lambda b, i: (b, 0, 0)),
                  full((LRU_CONV_WIDTH, ch)), full((1, ch)),
                  full((ch, 2 * ch)), full((1, 2 * ch)), full((1, ch))]
                 + [pl.BlockSpec((1, tl, ch), cmap)] * len(extra),
        out_specs=[pl.BlockSpec((1, tl, ch), cmap),
                   pl.BlockSpec((1, 1, ch), lambda b, i: (b, 0, 0))],
        out_shape=[jax.ShapeDtypeStruct((bsz, n, ch), BF16 if merge else F32),
                   jax.ShapeDtypeStruct((bsz, 1, ch), F32)],
        scratch_shapes=[pltpu.VMEM((tl, ch), F32),
                        pltpu.VMEM((tl, ch), F32),
                        pltpu.VMEM((SUBLANES, ch), F32),
                        pltpu.VMEM((SUBLANES, ch), F32)],
        compiler_params=_params("arbitrary", "arbitrary"),
        name="lru_rev" if reverse else "lru_fwd",
    )(lx, h0, cw, cb.reshape(1, ch), wgate, bgate.reshape(1, 2 * ch), lam.reshape(1, ch), *extra)


ROW_STRIDE = GRID_W + 2 * BF16_ROWS


def _conv_row_body(x_ref, w_ref, b_ref, o_ref, pad_ref):
    n = x_ref.shape[1]
    nrows = n // GRID_W
    gap = jnp.zeros((BF16_ROWS, LANES), F32)

    def fill(r, c):
        base = pl.multiple_of(r * ROW_STRIDE, SUBLANES)
        src = pl.multiple_of(r * GRID_W, SUBLANES)
        pad_ref[pl.ds(base, BF16_ROWS), :] = gap
        pad_ref[pl.ds(base + BF16_ROWS, GRID_W), :] = x_ref[0, pl.ds(src, GRID_W), :]
        pad_ref[pl.ds(base + BF16_ROWS + GRID_W, BF16_ROWS), :] = gap
        return c

    lax.fori_loop(0, nrows, fill, 0)

    def row(r, c):
        base = pl.multiple_of(r * ROW_STRIDE, SUBLANES)
        acc = jnp.broadcast_to(b_ref[...], (GRID_W, LANES))
        for k in range(CONV_WIDTH):
            acc = acc + w_ref[k:k + 1, :] * pad_ref[pl.ds(base + BF16_ROWS - CONV_PAD + k, GRID_W), :]
        o_ref[0, pl.ds(pl.multiple_of(r * GRID_W, SUBLANES), GRID_W), :] = acc
        return c

    lax.fori_loop(0, nrows, row, 0, unroll=4)


def _conv_col_body(x_ref, w_ref, b_ref, o_ref, pad_ref):
    n = x_ref.shape[1]
    nrows = n // GRID_W
    halo = CONV_PAD * GRID_W
    pad_ref[0:halo, :] = jnp.zeros((halo, LANES), F32)
    pad_ref[halo + n:halo + n + halo, :] = jnp.zeros((halo, LANES), F32)

    def fill(r, c):
        src = pl.multiple_of(r * GRID_W, SUBLANES)
        pad_ref[pl.ds(halo + src, GRID_W), :] = x_ref[0, pl.ds(src, GRID_W), :]
        return c

    lax.fori_loop(0, nrows, fill, 0)

    def row(r, c):
        base = pl.multiple_of(r * GRID_W, SUBLANES)
        acc = jnp.broadcast_to(b_ref[...], (GRID_W, LANES))
        for k in range(CONV_WIDTH):
            acc = acc + w_ref[k:k + 1, :] * pad_ref[pl.ds(base + k * GRID_W, GRID_W), :]
        o_ref[0, pl.ds(base, GRID_W), :] = acc
        return c

    lax.fori_loop(0, nrows, row, 0, unroll=4)


def _conv_grid(glu, w, b):
    bsz, n, ch = glu.shape
    half = ch // 2
    ng = half // LANES
    outs = []
    for body, first, pad_rows in ((_conv_row_body, 0, (n // GRID_W) * ROW_STRIDE),
                                  (_conv_col_body, ng, n + 2 * CONV_PAD * GRID_W)):
        outs.append(pl.pallas_call(
            body,
            grid=(bsz, ng),
            in_specs=[pl.BlockSpec((1, n, LANES), lambda bb, g, first=first: (bb, 0, g + first)),
                      pl.BlockSpec((CONV_WIDTH, LANES), lambda bb, g, first=first: (0, g + first)),
                      pl.BlockSpec((1, LANES), lambda bb, g, first=first: (0, g + first))],
            out_specs=pl.BlockSpec((1, n, LANES), lambda bb, g: (bb, 0, g)),
            out_shape=jax.ShapeDtypeStruct((bsz, n, half), F32),
            scratch_shapes=[pltpu.VMEM((pad_rows, LANES), F32)],
            compiler_params=_params("arbitrary", "arbitrary"),
            name="conv_row" if first == 0 else "conv_col",
        )(glu, w, b.reshape(1, ch)))
    return outs


def _out_body(cr_ref, cc_ref, lng_ref, lnb_ref, yl_ref, wo_ref, bo_ref, x_ref,
              g1_ref, n2g_ref, sh2_ref, sc2_ref, rw_ref, x1_ref, v_ref, aff_ref):
    cv = jnp.concatenate([cr_ref[0], cc_ref[0]], axis=-1)
    mu = jnp.mean(cv, axis=-1, keepdims=True)
    dv = cv - mu
    var = jnp.mean(dv * dv, axis=-1, keepdims=True)
    yn = dv * lax.rsqrt(var + LN_EPS) * lng_ref[...] + lnb_ref[...]
    cy = yn * jax.nn.sigmoid(yn)
    cat = jnp.concatenate([cy.astype(BF16), yl_ref[0]], axis=-1)
    m = jnp.dot(cat, wo_ref[...], preferred_element_type=F32) + bo_ref[...]
    x1 = x_ref[0] + g1_ref[0] * m
    x1_ref[0] = x1
    ms = jnp.mean(x1 * x1, axis=-1, keepdims=True)
    v = x1 * lax.rsqrt(ms + RMS_EPS) * n2g_ref[...]
    v = v * (1.0 + sc2_ref[0]) + sh2_ref[0]
    v_ref[0] = v.astype(BF16)
    v_hi = v.astype(BF16)
    v_lo = (v - v_hi.astype(F32)).astype(BF16)
    rw = rw_ref[...]
    rw_hi = rw.astype(BF16)
    rw_lo = (rw - rw_hi.astype(F32)).astype(BF16)
    nt = (((1,), (1,)), ((), ()))
    ne = rw.shape[0]
    both = lax.dot_general(jnp.concatenate([rw_hi, rw_lo], axis=0), v_hi, nt, preferred_element_type=F32)
    lg = both[:ne] + both[ne:] + lax.dot_general(rw_hi, v_lo, nt, preferred_element_type=F32)
    ex = jnp.exp(lg - jnp.max(lg, axis=0, keepdims=True))
    aff_ref[0] = ex / jnp.sum(ex, axis=0, keepdims=True)


def _out_proj_route(conv_r, conv_c, ln_g, ln_b, yl, wo_bf16, b_out, x, g1, n2g, sh2, sc2, rw_t, tm):
    bsz, n, d = x.shape
    half = conv_r.shape[2]
    ch = yl.shape[2]
    ne = rw_t.shape[0]
    tok = lambda c: pl.BlockSpec((1, tm, c), lambda b, i: (b, i, 0))
    full = lambda shape: pl.BlockSpec(shape, lambda b, i: (0,) * len(shape))
    (g1, g1_spec), (sh2, sh2_spec), (sc2, sc2_spec) = g1, sh2, sc2
    return pl.pallas_call(
        _out_body,
        grid=(bsz, n // tm),
        in_specs=[tok(half), tok(half), full((1, 2 * half)), full((1, 2 * half)),
                  tok(ch), full((2 * half + ch, d)), full((1, d)), tok(d),
                  g1_spec, full((1, d)), sh2_spec, sc2_spec, full((ne, d))],
        out_specs=[tok(d), tok(d), pl.BlockSpec((1, ne, tm), lambda b, i: (b, 0, i))],
        out_shape=[jax.ShapeDtypeStruct((bsz, n, d), F32),
                   jax.ShapeDtypeStruct((bsz, n, d), BF16),
                   jax.ShapeDtypeStruct((bsz, ne, n), F32)],
        compiler_params=_params("arbitrary", "arbitrary"),
        name="out_proj_route",
    )(conv_r, conv_c, ln_g.reshape(1, -1), ln_b.reshape(1, -1), yl, wo_bf16,
      b_out.reshape(1, d), x, g1, n2g.reshape(1, d), sh2, sc2, rw_t)


def _token_cumsum(m, rows_per_expert):
    er = m.shape[0]
    li = lax.broadcasted_iota(jnp.int32, (LANES, LANES), 0)
    lj = lax.broadcasted_iota(jnp.int32, (LANES, LANES), 1)
    upper = jnp.where(li <= lj, 1.0, 0.0).astype(BF16)
    cs = jnp.dot(m.astype(BF16), upper, preferred_element_type=F32)
    rt = jnp.broadcast_to(cs[:, LANES - 1:LANES], (er, LANES)).astype(BF16)
    ri = lax.broadcasted_iota(jnp.int32, (er, er), 0)
    ci = lax.broadcasted_iota(jnp.int32, (er, er), 1)
    same = (ri // rows_per_expert) == (ci // rows_per_expert)
    lower = jnp.where(same & (ci < ri), 1.0, 0.0).astype(BF16)
    rstart = jnp.dot(lower, rt, preferred_element_type=F32)
    return cs, rstart


def _topk_body(aff_ref, gate_ref, pos_ref, rs_ref, cs_scr, rs_scr, *, cap):
    aff = aff_ref[0]
    ne, rows, _ = aff.shape

    def bisect(i, thr):
        cand = thr | jnp.left_shift(jnp.int32(1), 30 - i)
        cnt = jnp.sum((aff >= pltpu.bitcast(cand, F32)).astype(jnp.int32), axis=(1, 2), keepdims=True)
        return jnp.where(cnt >= cap, cand, thr)

    thr = lax.fori_loop(0, 31, bisect, jnp.zeros((ne, 1, 1), jnp.int32))
    gt = aff >= pltpu.bitcast(thr + 1, F32)
    eq = (aff >= pltpu.bitcast(thr, F32)) & jnp.logical_not(gt)
    need = (cap - jnp.sum(gt.astype(jnp.int32), axis=(1, 2), keepdims=True)).astype(F32)
    eqf = jnp.where(eq, 1.0, 0.0)
    rank_in_row, rank_row0 = _token_cumsum(eqf.reshape(ne * rows, LANES), rows)
    rank_excl = (rank_in_row + rank_row0).reshape(ne, rows, LANES) - eqf
    sel = gt | (eq & (rank_excl < need))
    self_ = jnp.where(sel, 1.0, 0.0)
    cs, rstart = _token_cumsum(self_.reshape(ne * rows, LANES), rows)
    cs3 = cs.reshape(ne, rows, LANES)
    rstart3 = rstart.reshape(ne, rows, LANES)
    cs_scr[...] = cs3
    rs_scr[...] = rstart3
    pos_ref[0] = jnp.where(sel, cs3 + rstart3 - 1.0, -1.0).astype(jnp.int32)
    rs_ref[0] = rstart3.astype(jnp.int32)

    jrow = lax.broadcasted_iota(jnp.int32, (1, cap), 1).astype(F32)
    sub_r = lax.broadcasted_iota(jnp.int32, (rows, cap), 0).astype(F32)
    sub_l = lax.broadcasted_iota(jnp.int32, (LANES, cap), 0).astype(F32)

    def per_expert(e, c):
        cl = cs_scr[e]
        af = aff_ref[0, e]
        rowtot = cl[:, LANES - 1:LANES]
        before = rowtot + rs_scr[e][:, 0:1] <= jrow
        rj = jnp.sum(jnp.where(before, 1.0, 0.0), axis=0, keepdims=True)
        rowbase = jnp.sum(jnp.where(before, rowtot, 0.0), axis=0, keepdims=True)
        onehot = jnp.where(sub_r == rj, 1.0, 0.0).astype(BF16)
        a1 = af.astype(BF16)
        r1 = af - a1.astype(F32)
        a2 = r1.astype(BF16)
        a3 = (r1 - a2.astype(F32)).astype(BF16)
        lhs = jnp.concatenate([cl.astype(BF16), a1, a2, a3], axis=1)
        gathered = lax.dot_general(lhs, onehot, (((0,), (0,)), ((), ())), preferred_element_type=F32)
        g = gathered[:LANES]
        ga = gathered[LANES:2 * LANES] + gathered[2 * LANES:3 * LANES] + gathered[3 * LANES:]
        lanepos = jnp.sum(jnp.where(g <= jrow - rowbase, 1.0, 0.0), axis=0, keepdims=True)
        gate = jnp.sum(jnp.where(sub_l == lanepos, ga, 0.0), axis=0, keepdims=True)
        gate_ref[0, pl.ds(e, 1), :] = gate
        return c

    lax.fori_loop(0, ne, per_expert, 0)


def _topk_route(aff4, cap):
    bsz, ne, rows, _ = aff4.shape
    blk4 = pl.BlockSpec((1, ne, rows, LANES), lambda b: (b, 0, 0, 0))
    lst = pl.BlockSpec((1, ne, cap), lambda b: (b, 0, 0))
    return pl.pallas_call(
        functools.partial(_topk_body, cap=cap),
        grid=(bsz,),
        in_specs=[blk4],
        out_specs=[lst, blk4, blk4],
        out_shape=[jax.ShapeDtypeStruct((bsz, ne, cap), F32),
                   jax.ShapeDtypeStruct((bsz, ne, rows, LANES), jnp.int32),
                   jax.ShapeDtypeStruct((bsz, ne, rows, LANES), jnp.int32)],
        scratch_shapes=[pltpu.VMEM((ne, rows, LANES), F32), pltpu.VMEM((ne, rows, LANES), F32)],
        compiler_params=_params("arbitrary"),
        name="topk_route",
    )(aff4)


PACK_TOK = 256
PACK_SLAB = 64
PACK_SHIFT = 6
PACK_WAYS = 2


def _pack_body(rs_ref, pos_ref, v_ref, xs_hbm, stage, stage_x, pend, sems, sem_x, *, cap):
    bp = pl.program_id(0)
    s = pl.program_id(1)
    q = pl.program_id(2)
    ns = pl.num_programs(1)
    b = bp * PACK_WAYS + q
    ne = pos_ref.shape[2]
    tpr = PACK_TOK // LANES
    nr = ns * tpr
    step = (bp * ns + s) * PACK_WAYS + q
    nsteps = pl.num_programs(0) * ns * PACK_WAYS
    slot = step % (PACK_WAYS + 1)
    slab = PACK_SLAB
    pq = q * ne

    @pl.when(s == 0)
    def _():
        for e in range(ne):
            pend[pq + e] = jnp.zeros(pend.shape[1:], BF16)

    def first_pos(rr, e):
        return rs_ref[(b * ne + e) * (nr + 1) + rr]

    def floor16(p):
        return lax.shift_left(lax.shift_right_logical(p, 4), 4)

    p0 = [first_pos(s * tpr, e) for e in range(ne)]
    p1 = [first_pos(s * tpr + tpr, e) for e in range(ne)]
    a0 = [floor16(p0[e]) for e in range(ne)]
    a1 = [floor16(p1[e]) for e in range(ne)]
    has = [a1[e] < p1[e] for e in range(ne)]
    prnd = [lax.shift_right_logical(a1[e] - a0[e], PACK_SHIFT) for e in range(ne)]
    poff = [pl.multiple_of(jnp.bitwise_and(a1[e] - a0[e], slab - 1), BF16_ROWS) for e in range(ne)]

    vb = v_ref[0]
    pos = [jnp.concatenate([pos_ref[0, j, e:e + 1, :] for j in range(tpr)], axis=1) for e in range(ne)]
    kiota = lax.broadcasted_iota(jnp.int32, (slab, PACK_TOK), 0)

    def onehot(rnd):
        blocks = [jnp.where(kiota == pos[e] - (a0[e] + rnd * slab), 1.0, 0.0) for e in range(ne)]
        return jnp.concatenate(blocks, axis=0).astype(BF16)

    def keep_partial(buf, e, rnd, old):
        grp = buf[pl.ds(e * slab + poff[e], BF16_ROWS), :]
        return jnp.where(jnp.logical_and(has[e], prnd[e] == rnd), grp, old)

    cur = stage.at[slot]
    cur[...] = jnp.dot(onehot(0), vb, preferred_element_type=F32).astype(BF16)
    for e in range(ne):
        cur[e * slab:e * slab + BF16_ROWS, :] += pend[pq + e]
    for e in range(ne):
        pend[pq + e] = keep_partial(cur, e, 0, jnp.zeros((BF16_ROWS, vb.shape[1]), BF16))

    def slab_bytes_wait(buf, sem):
        pltpu.make_async_copy(buf, buf, sem).wait()

    @pl.when(step >= PACK_WAYS)
    def _():
        old = (step + 1) % (PACK_WAYS + 1)
        slab_bytes_wait(stage.at[old], sems.at[old])

    @pl.when(s == 0)
    def _():
        stage_x[0:slab, :] = jnp.zeros((slab, stage_x.shape[1]), BF16)
        for e in range(ne):
            pltpu.make_async_copy(stage_x.at[0:slab], xs_hbm.at[e, b, pl.ds(cap, slab)], sem_x).start()
        slab_bytes_wait(stage_x, sem_x)

    for e in range(ne):
        pltpu.make_async_copy(stage.at[slot, e * slab:(e + 1) * slab],
                              xs_hbm.at[e, b, pl.ds(pl.multiple_of(a0[e], BF16_ROWS), slab)], sems.at[slot]).start()

    span = p1[0] - a0[0]
    for e in range(1, ne):
        span = jnp.maximum(span, p1[e] - a0[e])
    rounds = jnp.maximum(1, lax.shift_right_logical(span + (slab - 1), PACK_SHIFT))

    def extra(rnd, carry):
        stage_x[...] = jnp.dot(onehot(rnd), vb, preferred_element_type=F32).astype(BF16)
        for e in range(ne):
            pend[pq + e] = keep_partial(stage_x, e, rnd, pend[pq + e])
        for e in range(ne):
            @pl.when(p1[e] - a0[e] > rnd * slab)
            def _(e=e):
                cp = pltpu.make_async_copy(
                    stage_x.at[e * slab:(e + 1) * slab],
                    xs_hbm.at[e, b, pl.ds(pl.multiple_of(a0[e] + rnd * slab, BF16_ROWS), slab)], sem_x)
                cp.start()
                cp.wait()
        return carry

    lax.fori_loop(1, rounds, extra, 0)

    @pl.when(step == nsteps - 1)
    def _():
        for back in range(PACK_WAYS):
            last = (step + (PACK_WAYS + 1) - back) % (PACK_WAYS + 1)
            slab_bytes_wait(stage.at[last], sems.at[last])


def _pack_tokens(rs_ext, pos_r, v, cap):
    bsz, n, d = v.shape
    assert bsz % PACK_WAYS == 0
    ne = pos_r.shape[2]
    tpr = PACK_TOK // LANES
    return pl.pallas_call(
        functools.partial(_pack_body, cap=cap),
        grid_spec=pltpu.PrefetchScalarGridSpec(
            num_scalar_prefetch=1,
            grid=(bsz // PACK_WAYS, n // PACK_TOK, PACK_WAYS),
            in_specs=[pl.BlockSpec((1, tpr, ne, LANES), lambda bp, s, q, rs: (bp * PACK_WAYS + q, s, 0, 0)),
                      pl.BlockSpec((1, PACK_TOK, d), lambda bp, s, q, rs: (bp * PACK_WAYS + q, s, 0))],
            out_specs=pl.BlockSpec(memory_space=pl.ANY),
            scratch_shapes=[pltpu.VMEM((PACK_WAYS + 1, ne * PACK_SLAB, d), BF16),
                            pltpu.VMEM((ne * PACK_SLAB, d), BF16),
                            pltpu.VMEM((PACK_WAYS * ne, BF16_ROWS, d), BF16),
                            pltpu.SemaphoreType.DMA((PACK_WAYS + 1,)),
                            pltpu.SemaphoreType.DMA(())]),
        out_shape=jax.ShapeDtypeStruct((ne, bsz, cap + PACK_SLAB, d), BF16),
        compiler_params=_params("arbitrary", "arbitrary", "arbitrary"),
        name="pack_tokens",
    )(rs_ext, pos_r, v)


def _ffn_body(x_ref, gate_ref, wg_ref, wu_ref, wd_ref, ys_hbm, acc, stage, osems, *, mg, nf, sub, och):
    e = pl.program_id(0)
    f = pl.program_id(1)
    spg = x_ref.shape[2]

    def chunk_step(first):
        wgb = wg_ref[0].astype(BF16)
        wub = wu_ref[0].astype(BF16)
        wdb = wd_ref[0].astype(BF16)
        for i in range(mg // sub):
            rows = slice(i * sub, (i + 1) * sub)
            x = x_ref[0, (i * sub) // spg, (i * sub) % spg:(i * sub) % spg + sub, :]
            gg = jnp.dot(x, wgb, preferred_element_type=F32)
            uu = jnp.dot(x, wub, preferred_element_type=F32)
            h = (gg * jax.nn.sigmoid(gg) * uu).astype(BF16)
            y = jnp.dot(h, wdb, preferred_element_type=F32)
            if first:
                acc[rows, :] = y
            else:
                acc[rows, :] += y

    pl.when(f == 0)(functools.partial(chunk_step, True))
    pl.when(f != 0)(functools.partial(chunk_step, False))

    @pl.when(f == nf - 1)
    def _():
        gate_t = gate_ref[0].T

        def out_copy(c):
            return pltpu.make_async_copy(stage.at[c % 2], ys_hbm.at[e, pl.ds(c * och, och)], osems.at[c % 2])

        nchunks = mg // och
        for c in range(nchunks):
            if c >= 2:
                out_copy(c - 2).wait()
            for i in range(och // LANES):
                blk = c * (och // LANES) + i
                rows = slice(blk * LANES, (blk + 1) * LANES)
                stage[c % 2, i * LANES:(i + 1) * LANES, :] = (acc[rows, :] * gate_t[:, blk:blk + 1]).astype(BF16)
            out_copy(c).start()
        for c in range(max(nchunks - 2, 0), nchunks):
            out_copy(c).wait()


def _expert_ffn(xs, gate_rows, wg, wu, wd, tf):
    ne, d, fdim = wg.shape
    bsz, cap = xs.shape[1], gate_rows.shape[1] * LANES // xs.shape[1]
    mg = bsz * cap
    nf = fdim // tf
    sub = cap
    och = min(mg, 512)
    return pl.pallas_call(
        functools.partial(_ffn_body, mg=mg, nf=nf, sub=sub, och=och),
        grid_spec=pltpu.PrefetchScalarGridSpec(
            num_scalar_prefetch=0,
            grid=(ne, nf),
            in_specs=[pl.BlockSpec((1, bsz, cap, d), lambda e, f: (e, 0, 0, 0)),
                      pl.BlockSpec((1, mg // LANES, LANES), lambda e, f: (e, 0, 0)),
                      pl.BlockSpec((1, d, tf), lambda e, f: (e, 0, f)),
                      pl.BlockSpec((1, d, tf), lambda e, f: (e, 0, f)),
                      pl.BlockSpec((1, tf, d), lambda e, f: (e, f, 0))],
            out_specs=pl.BlockSpec(memory_space=pl.ANY),
            scratch_shapes=[pltpu.VMEM((mg, d), F32),
                            pltpu.VMEM((2, och, d), BF16),
                            pltpu.SemaphoreType.DMA((2,))]),
        out_shape=jax.ShapeDtypeStruct((ne, mg, d), BF16),
        compiler_params=_params("arbitrary", "arbitrary"),
        name="expert_ffn",
    )(xs, gate_rows, wg, wu, wd)


SLAB = PACK_SLAB
TILE_ROWS = PACK_TOK // LANES


def _combine_body(rs_ref, pos_ref, ys_hbm, x1_ref, g5_ref, fng_ref, o_ref, stk, stk_x, sems, sem_x, *, cap, tps):
    b = pl.program_id(0)
    s = pl.program_id(1)
    nb = pl.num_programs(0)
    ns = pl.num_programs(1)
    ne = pos_ref.shape[2]
    nr = ns * tps * TILE_ROWS
    tok = TILE_ROWS * LANES
    step = b * ns + s
    slot = step % 2
    tile_rows = ne * SLAB

    def first_pos(bb, rr, e):
        return rs_ref[(bb * ne + e) * (nr + 1) + rr]

    def aligned(p0):
        return lax.shift_left(lax.shift_right_logical(p0, 4), 4)

    def slab_start(a0, rnd):
        return pl.multiple_of(jnp.minimum(a0 + rnd * SLAB, cap - SLAB), BF16_ROWS)

    def slab_copy(bb, e, a, dst, row, sem):
        return pltpu.make_async_copy(ys_hbm.at[e, pl.ds(bb * cap + a, SLAB)], dst.at[pl.ds(row, SLAB)], sem)

    def issue(bb, ss, sl):
        for j in range(tps):
            for e in range(ne):
                a = slab_start(aligned(first_pos(bb, (ss * tps + j) * TILE_ROWS, e)), 0)
                slab_copy(bb, e, a, stk.at[sl], j * tile_rows + e * SLAB, sems.at[sl]).start()

    @pl.when(step == 0)
    def _():
        issue(b, s, slot)

    @pl.when(step + 1 < nb * ns)
    def _():
        nxt = step + 1
        issue(nxt // ns, nxt % ns, 1 - slot)

    kiota = lax.broadcasted_iota(jnp.int32, (SLAB, tok), 0)
    tn = (((0,), (0,)), ((), ()))
    pltpu.make_async_copy(stk.at[slot], stk.at[slot], sems.at[slot]).wait()

    for j in range(tps):
        r = (s * tps + j) * TILE_ROWS
        pos = jnp.concatenate([pos_ref[0, j * TILE_ROWS + q] for q in range(TILE_ROWS)], axis=1)
        a0 = [aligned(first_pos(b, r, e)) for e in range(ne)]

        def onehot(rnd, pos=pos, a0=a0):
            blocks = []
            for e in range(ne):
                rel = pos[e:e + 1, :] - slab_start(a0[e], rnd)
                fresh = (pos[e:e + 1, :] - a0[e]) >= rnd * SLAB
                blocks.append(jnp.where((kiota == rel) & fresh, 1.0, 0.0))
            return jnp.concatenate(blocks, axis=0).astype(BF16)

        moe = lax.dot_general(onehot(0), stk[slot, j * tile_rows:(j + 1) * tile_rows, :], tn,
                              preferred_element_type=F32)

        span = first_pos(b, r + TILE_ROWS, 0) - a0[0]
        for e in range(1, ne):
            span = jnp.maximum(span, first_pos(b, r + TILE_ROWS, e) - a0[e])
        rounds = jnp.maximum(1, lax.shift_right_logical(span + (SLAB - 1), PACK_SHIFT))

        def extra(rnd, m, a0=a0, onehot=onehot):
            for e in range(ne):
                slab_copy(b, e, slab_start(a0[e], rnd), stk_x, e * SLAB, sem_x).start()
            pltpu.make_async_copy(stk_x, stk_x, sem_x).wait()
            return m + lax.dot_general(onehot(rnd), stk_x[...], tn, preferred_element_type=F32)

        moe = lax.fori_loop(1, rounds, extra, moe)

        y = x1_ref[0, j * tok:(j + 1) * tok, :] + g5_ref[0] * moe
        ms = jnp.mean(y * y, axis=-1, keepdims=True)
        o_ref[0, j * tok:(j + 1) * tok, :] = y * lax.rsqrt(ms + RMS_EPS) * fng_ref[...]


def _combine_norm(rs_ext, pos_r, ys, x1, g5, fng, cap, tps):
    bsz, n, d = x1.shape
    g5, g5_spec = g5
    ns = n // (PACK_TOK * tps)
    ne = pos_r.shape[2]
    return pl.pallas_call(
        functools.partial(_combine_body, cap=cap, tps=tps),
        grid_spec=pltpu.PrefetchScalarGridSpec(
            num_scalar_prefetch=1,
            grid=(bsz, ns),
            in_specs=[pl.BlockSpec((1, tps * TILE_ROWS, ne, LANES), lambda b, s, rs: (b, s, 0, 0)),
                      pl.BlockSpec(memory_space=pl.ANY),
                      pl.BlockSpec((1, tps * PACK_TOK, d), lambda b, s, rs: (b, s, 0)),
                      g5_spec,
                      pl.BlockSpec((1, d), lambda b, s, rs: (0, 0))],
            out_specs=pl.BlockSpec((1, tps * PACK_TOK, d), lambda b, s, rs: (b, s, 0)),
            scratch_shapes=[pltpu.VMEM((2, tps * ne * SLAB, d), BF16),
                            pltpu.VMEM((ne * SLAB, d), BF16),
                            pltpu.SemaphoreType.DMA((2,)),
                            pltpu.SemaphoreType.DMA(())]),
        out_shape=jax.ShapeDtypeStruct((bsz, n, d), F32),
        compiler_params=_params("arbitrary", "arbitrary"),
        name="combine_norm",
    )(rs_ext, pos_r, ys, x1, g5, fng.reshape(1, d))


def _block_diag(w):
    heads, hd, _ = w.shape
    eye = jnp.eye(heads, dtype=w.dtype)
    return (eye[:, None, :, None] * w[:, :, None, :]).reshape(heads * hd, heads * hd)


IN_PROJ_ROWS = 1024
OUT_PROJ_ROWS = 1024
LRU_CHUNK = 1024
FFN_COLS = 256
COMBINE_TILES = 2


def _tile(n, pref):
    return pref if n % pref == 0 else n


def kernel(x, c, ctx, c_ctx, norm1_g, norm2_g, ada_w, ada_b, w_in, b_in, conv_dw_w, conv_dw_b, conv_ln_g, conv_ln_b, lru_conv_w, lru_conv_b, lru_wa, lru_ba, lru_wi, lru_bi, lru_lambda, w_out, b_out, router_w, exp_w_gate, exp_w_up, exp_w_down, final_norm_g):
    assert norm1_g.shape[0] == 1
    mod, x1, v, aff = _mixer(x, c, ctx, c_ctx, norm1_g[0], norm2_g[0], ada_w[0], ada_b[0], w_in[0], b_in[0],
                             conv_dw_w[0], conv_dw_b[0], conv_ln_g[0], conv_ln_b[0], lru_conv_w[0],
                             lru_conv_b[0], lru_wa[0], lru_ba[0], lru_wi[0], lru_bi[0], lru_lambda[0],
                             w_out[0], b_out[0], router_w[0])
    return _moe_norm(x1, v, aff, mod[5], exp_w_gate[0], exp_w_up[0], exp_w_down[0], final_norm_g)


def _mixer(x, c, ctx, c_ctx, norm1_g, norm2_g, ada_w, ada_b, w_in, b_in, conv_w, conv_b, ln_g, ln_b,
           lru_cw, lru_cb, lru_wa, lru_ba, lru_wi, lru_bi, lru_lam, w_out, b_out, router_w):
    bsz, n, d = x.shape
    cond8 = jnp.zeros((SUBLANES, d), F32).at[:bsz].set(c).at[bsz].set(c_ctx)
    mods = _ada_mod(cond8, ada_w, ada_b)
    mods3 = mods.reshape(SUBLANES * N_MOD, 1, d)
    mod = [_mod_row(mods3, k) for k in range(N_MOD)]
    mod_c = [_mod_row(mods3, k, fixed_row=bsz) for k in range(2)]
    w_in_b = w_in.astype(BF16)
    w_out_b = w_out.astype(BF16)
    ch = lru_cb.shape[1]
    wgate = [jnp.concatenate([_block_diag(lru_wa[dd]), _block_diag(lru_wi[dd])], axis=1).astype(BF16)
             for dd in range(2)]
    bgate = [jnp.concatenate([lru_ba[dd], lru_bi[dd]]) for dd in range(2)]

    def lru(lx, h0, dd, reverse, merge_with=None):
        return _lru_dir(lx, h0, lru_cw[dd], lru_cb[dd], wgate[dd], bgate[dd], lru_lam[dd], reverse,
                        _tile(lx.shape[1], LRU_CHUNK), merge_with)

    _, c_lx, _ = _in_proj(ctx, norm1_g, mod_c[0], mod_c[1], w_in_b, b_in, _tile(ctx.shape[1], IN_PROJ_ROWS))
    zero_h = jnp.zeros((bsz, 1, ch), F32)
    _, hf0 = lru(c_lx, zero_h, 0, False)
    _, hb0 = lru(c_lx, zero_h, 1, True)

    x_glu, x_lx, x_glg = _in_proj(x, norm1_g, mod[0], mod[1], w_in_b, b_in, _tile(n, IN_PROJ_ROWS))
    hb, _ = lru(x_lx, hb0, 1, True)
    yl, _ = lru(x_lx, hf0, 0, False, merge_with=(hb, x_glg))
    conv_r, conv_c = _conv_grid(x_glu, conv_w, conv_b)
    x1, v, aff = _out_proj_route(conv_r, conv_c, ln_g, ln_b, yl, w_out_b, b_out, x, mod[2],
                                 norm2_g, mod[3], mod[4], router_w.T, _tile(n, OUT_PROJ_ROWS))
    return mod, x1, v, aff


def _moe_norm(x1, v, aff, gate2, wg, wu, wd, final_norm_g):
    bsz, n, d = x1.shape
    ne = aff.shape[1]
    cap = EC_CAPACITY * n // ne
    gate, pos, rstart = _topk_route(aff.reshape(bsz, ne, n // LANES, LANES), cap)
    gate_rows = jnp.swapaxes(gate, 0, 1).reshape(ne, bsz * cap // LANES, LANES)
    rs_ext = jnp.concatenate([rstart[..., 0], jnp.full((bsz, ne, 1), cap, jnp.int32)], axis=-1).reshape(-1)
    pos_r = jnp.swapaxes(pos, 1, 2)
    xs = _pack_tokens(rs_ext, pos_r, v, cap)
    ys = _expert_ffn(xs, gate_rows, wg, wu, wd, FFN_COLS)
    return _combine_norm(rs_ext, pos_r, ys, x1, gate2, final_norm_g, cap, COMBINE_TILES)
```
